```python
import math
import jax
import jax.numpy as jnp
from jax import lax
import numpy as np

D_MODEL = 1024
BATCH = 16
SEQ = 2048
DEPTH = 4

CTX_LEN = 256
GRID_W = 64

CONV_W = 4
CONV_LEFT = 2

BRANCH_W = D_MODEL
N_BRANCH = 3

SSD_HEAD_DIM = 64
SSD_HEADS = BRANCH_W // SSD_HEAD_DIM
SSD_WIDTH = SSD_HEADS * SSD_HEAD_DIM
SSD_GROUPS = 4
SSD_STATE = 128
SSD_CONV_CH = SSD_WIDTH + 2 * SSD_GROUPS * SSD_STATE
SSD_CHUNK = 128
DT_MIN = 1e-3
DT_MAX = 1e-1

LRU_WIDTH = BRANCH_W
LRU_BLOCKS = 16
LRU_BLOCK = LRU_WIDTH // LRU_BLOCKS
LRU_C = 8.0

GLA_HEADS = 4
GLA_KEY = D_MODEL // 2
GLA_VAL = BRANCH_W
GLA_DK = GLA_KEY // GLA_HEADS
GLA_DV = GLA_VAL // GLA_HEADS
GLA_GATE_RANK = 16
GLA_TAU = 16.0
GLA_CHUNK = 64

D_FF = 2816
FFN_RES_W = 0.5

N_MOD = 9

ALPHA = (2.0 * DEPTH) ** 0.25
BETA = (8.0 * DEPTH) ** -0.25
NORM_EPS = 1e-5

IN_SIZES = (SSD_WIDTH, SSD_CONV_CH, 2 * SSD_HEADS,
            LRU_WIDTH, LRU_WIDTH,
            GLA_KEY, GLA_KEY, GLA_VAL, GLA_VAL, 2 * GLA_GATE_RANK,
            N_BRANCH * D_MODEL)
IN_TOTAL = sum(IN_SIZES)

kernel_name = 'hybrid_ssd_rglru_gla_prefix_dit'


def layer_norm(x, g, b):
    xf = x.astype(jnp.float32)
    mu = jnp.mean(xf, axis=-1, keepdims=True)
    var = jnp.mean(jnp.square(xf - mu), axis=-1, keepdims=True)
    return ((xf - mu) * lax.rsqrt(var + NORM_EPS)).astype(g.dtype) * g + b


def rms_norm(x, g):
    xf = x.astype(jnp.float32)
    return (xf * lax.rsqrt(jnp.mean(jnp.square(xf), axis=-1, keepdims=True) + NORM_EPS)).astype(g.dtype) * g


def modulate(h, shift, scale):
    return h * (1.0 + scale) + shift


def post_norm(h, y, gate, res_w, g, b):
    return layer_norm(ALPHA * h + res_w * gate * y, g, b)


def swiglu(u, w_up, w_down):
    a, v = jnp.split(u @ w_up, 2, axis=-1)
    return (jax.nn.silu(a) * v) @ w_down


def ffn_sublayer(h, m, j, w_up, w_down, g, b):
    u = modulate(h, m[:, :, 3 * j], m[:, :, 3 * j + 1])
    return post_norm(h, swiglu(u, w_up, w_down), m[:, :, 3 * j + 2], FFN_RES_W, g, b)


def line_conv(x, w, b, line):
    n_b, t, ch = x.shape
    xl = x.reshape(n_b, t // line, line, ch)
    xp = jnp.pad(xl, ((0, 0), (0, 0), (CONV_LEFT, CONV_W - 1 - CONV_LEFT), (0, 0)))
    y = b + xp[:, :, 0:line] * w[0]
    for k in range(1, CONV_W):
        y = y + xp[:, :, k:k + line] * w[k]
    return y.reshape(n_b, t, ch)


def raster_to_colmajor(x, rows):
    n_b, t, ch = x.shape
    return x.reshape(n_b, rows, GRID_W, ch).transpose(0, 2, 1, 3).reshape(n_b, t, ch)


def colmajor_to_raster(x, rows):
    n_b, t, ch = x.shape
    return x.reshape(n_b, GRID_W, rows, ch).transpose(0, 2, 1, 3).reshape(n_b, t, ch)


def direction(t, d):
    return jnp.flip(t, axis=1) if d else t


def linear_scan(a, b, h0):
    b = b.at[:, 0].add(a[:, 0] * h0)

    def combine(lhs, rhs):
        return lhs[0] * rhs[0], rhs[0] * lhs[1] + rhs[1]

    return lax.associative_scan(combine, (a, b), axis=1)[1]


def ssd_chunk_scan(x, dt, A, B, C, h0):
    n_b, t, n_h, p = x.shape
    n_g, n_s = B.shape[-2], B.shape[-1]
    q = SSD_CHUNK
    nc = t // q
    hg = n_h // n_g
    xs = (x * dt[..., None]).reshape(n_b, nc, q, n_g, hg, p)
    a = (dt * A).reshape(n_b, nc, q, n_g, hg).transpose(0, 1, 3, 4, 2)
    a_cs = jnp.cumsum(a, axis=-1)
    bc = B.reshape(n_b, nc, q, n_g, n_s)
    cc = C.reshape(n_b, nc, q, n_g, n_s)
    causal = jnp.tril(jnp.ones((q, q), dtype=bool))
    seg = a_cs[..., :, None] - a_cs[..., None, :]
    decay = jnp.exp(jnp.where(causal, seg, -jnp.inf))
    scores = jnp.einsum('bclgn,bcsgn->bcgls', cc, bc)
    y_diag = jnp.einsum('bcghls,bcsghp->bclghp', scores[:, :, :, None] * decay, xs)
    to_end = jnp.exp(a_cs[..., -1:] - a_cs).transpose(0, 1, 4, 2, 3)
    chunk_states = jnp.einsum('bclgn,bclghp->bcghpn', bc, xs * to_end[..., None])
    chunk_decay = jnp.exp(a_cs[..., -1])[..., None, None]
    h0g = h0.reshape(n_b, n_g, hg, p, n_s)
    s_after = linear_scan(chunk_decay, chunk_states, h0g)
    s_before = jnp.concatenate([h0g[:, None], s_after[:, :-1]], axis=1)
    from_start = jnp.exp(a_cs).transpose(0, 1, 4, 2, 3)[..., None]
    y_off = jnp.einsum('bclgn,bcghpn->bclghp', cc, s_before) * from_start
    return (y_diag + y_off).reshape(n_b, t, n_h, p), s_after[:, -1].reshape(n_b, n_h, p, n_s)


def gla_chunk_scan(q, k, v, log_a, s0):
    n_b, t, n_h, dk = q.shape
    dv = v.shape[-1]
    cl = GLA_CHUNK
    nc = t // cl
    q = q.reshape(n_b, nc, cl, n_h, dk)
    k = k.reshape(n_b, nc, cl, n_h, dk)
    v = v.reshape(n_b, nc, cl, n_h, dv)
    bcum = jnp.cumsum(log_a.reshape(n_b, nc, cl, n_h, dk), axis=2)
    b_last = bcum[:, :, -1]
    q_in = q * jnp.exp(bcum)
    k_in = k * jnp.exp(-bcum)
    causal = jnp.tril(jnp.ones((cl, cl), dtype=bool))
    att = jnp.where(causal, jnp.einsum('bclhd,bcshd->bchls', q_in, k_in), 0.0)
    o = jnp.einsum('bchls,bcshv->bclhv', att, v)
    k_st = k * jnp.exp(b_last[:, :, None] - bcum)
    d_state = jnp.einsum('bclhd,bclhv->bchdv', k_st, v)
    s_after = linear_scan(jnp.exp(b_last)[..., None], d_state, s0)
    s_before = jnp.concatenate([s0[:, None], s_after[:, :-1]], axis=1)
    o = o + jnp.einsum('bclhd,bchdv->bclhv', q_in, s_before)
    return o.reshape(n_b, t, n_h, dv), s_after[:, -1]


def ssd_mixer(pc, pl, line_lat, conv_w, conv_b, dt_bias, a_log, d_skip, norm_g):
    def prep(p, line):
        z, xbc, dt_raw = p
        n_b, t, _ = z.shape
        xbc = jax.nn.silu(line_conv(xbc, conv_w, conv_b, line)).astype(jnp.float32)
        xs, bs, cs = jnp.split(xbc, [SSD_WIDTH, SSD_WIDTH + SSD_GROUPS * SSD_STATE], axis=-1)
        return (z, xs.reshape(n_b, t, SSD_HEADS, SSD_HEAD_DIM),
                bs.reshape(n_b, t, SSD_GROUPS, SSD_STATE), cs.reshape(n_b, t, SSD_GROUPS, SSD_STATE),
                dt_raw.astype(jnp.float32))

    zc, xc, bc, cc, dtc = prep(pc, pc[0].shape[1])
    zl, xl, bl, cl, dtl = prep(pl, line_lat)
    h0 = jnp.zeros((xl.shape[0], SSD_HEADS, SSD_HEAD_DIM, SSD_STATE), jnp.float32)
    ys = []
    for d in range(2):
        A = -jnp.exp(a_log[d].astype(jnp.float32))
        cols = slice(d * SSD_HEADS, (d + 1) * SSD_HEADS)
        dtc_d = jax.nn.softplus(dtc[..., cols] + dt_bias[d])
        dtl_d = jax.nn.softplus(dtl[..., cols] + dt_bias[d])
        y_c, h_c = ssd_chunk_scan(direction(xc, d), direction(dtc_d, d), A, direction(bc, d), direction(cc, d), h0)
        y_l, _ = ssd_chunk_scan(direction(xl, d), direction(dtl_d, d), A, direction(bl, d), direction(cl, d), h_c)
        skip = d_skip[d].astype(jnp.float32)[:, None]
        ys.append((direction(y_c, d) + skip * xc, direction(y_l, d) + skip * xl))

    def out(y, z):
        n_b, t = z.shape[:2]
        return rms_norm(y.reshape(n_b, t, SSD_WIDTH) * jax.nn.silu(z.astype(jnp.float32)), norm_g)

    return out(ys[0][0] + ys[1][0], zc), out(ys[0][1] + ys[1][1], zl)


def rglru_mixer(pc, pl, line_lat, conv_w, conv_b, w_a, b_a, w_x, b_x, lam):
    xc_raw, gc = pc
    xl_raw, gl = pl
    xc = line_conv(xc_raw, conv_w, conv_b, xc_raw.shape[1]).astype(jnp.float32)
    xl = line_conv(xl_raw, conv_w, conv_b, line_lat).astype(jnp.float32)

    def coeffs(x, d):
        xb = x.reshape(x.shape[:-1] + (LRU_BLOCKS, LRU_BLOCK))
        r = jax.nn.sigmoid(jnp.einsum('btnk,nkj->btnj', xb, w_a[d]).reshape(x.shape) + b_a[d])
        i = jax.nn.sigmoid(jnp.einsum('btnk,nkj->btnj', xb, w_x[d]).reshape(x.shape) + b_x[d])
        log_a = -LRU_C * r * jax.nn.softplus(-lam[d])
        return jnp.exp(log_a), jnp.sqrt(-jnp.expm1(2.0 * log_a)) * (i * x)

    h0 = jnp.zeros((xl.shape[0], LRU_WIDTH), jnp.float32)
    hs = []
    for d in range(2):
        ac, bcoef = coeffs(direction(xc, d), d)
        al, blcoef = coeffs(direction(xl, d), d)
        h_c = linear_scan(ac, bcoef, h0)
        h_l = linear_scan(al, blcoef, h_c[:, -1])
        hs.append((direction(h_c, d), direction(h_l, d)))
    y_c = (hs[0][0] + hs[1][0]).astype(gc.dtype) * jax.nn.gelu(gc)
    y_l = (hs[0][1] + hs[1][1]).astype(gl.dtype) * jax.nn.gelu(gl)
    return y_c, y_l


def gla_mixer(pc, pl, w_gate, b_gate, norm_g):
    def prep(p):
        q, k, v, g, a_lr = p
        n_b, t, _ = q.shape
        qh = q.astype(jnp.float32).reshape(n_b, t, GLA_HEADS, GLA_DK) * (GLA_DK ** -0.5)
        kh = k.astype(jnp.float32).reshape(n_b, t, GLA_HEADS, GLA_DK)
        vh = v.astype(jnp.float32).reshape(n_b, t, GLA_HEADS, GLA_DV)
        return qh, kh, vh, g, a_lr

    def log_decay(a_lr, d):
        z = a_lr[..., d * GLA_GATE_RANK:(d + 1) * GLA_GATE_RANK] @ w_gate[d] + b_gate[d]
        return (jax.nn.log_sigmoid(z.astype(jnp.float32)) / GLA_TAU).reshape(a_lr.shape[:2] + (GLA_HEADS, GLA_DK))

    qc, kc, vc, gc, ac = prep(pc)
    ql, kl, vl, gl, al = prep(pl)
    s0 = jnp.zeros((ql.shape[0], GLA_HEADS, GLA_DK, GLA_DV), jnp.float32)
    os_ = []
    for d in range(2):
        o_c, s_c = gla_chunk_scan(direction(qc, d), direction(kc, d), direction(vc, d), direction(log_decay(ac, d), d), s0)
        o_l, _ = gla_chunk_scan(direction(ql, d), direction(kl, d), direction(vl, d), direction(log_decay(al, d), d), s_c)
        os_.append((direction(o_c, d), direction(o_l, d)))

    def out(o, g):
        n_b, t = g.shape[:2]
        return rms_norm(o, norm_g).reshape(n_b, t, GLA_VAL) * jax.nn.silu(g)

    return out(os_[0][0] + os_[1][0], gc), out(os_[0][1] + os_[1][1], gl)


def merge_branches(branches, gate_logits, w_branch, w_out):
    n_b, t, _ = gate_logits.shape
    gates = jax.nn.sigmoid(gate_logits.reshape(n_b, t, N_BRANCH, D_MODEL))
    m = gates[:, :, 0] * (branches[0] @ w_branch[0])
    for n in range(1, N_BRANCH):
        m = m + gates[:, :, n] * (branches[n] @ w_branch[n])
    return m @ w_out


def token_mixer(u_ctx, u_lat, line_lat, need_ctx, w_in, ssd_conv_w, ssd_conv_b, ssd_dt_bias, ssd_a_log, ssd_d,
                ssd_norm_g, lru_conv_w, lru_conv_b, lru_w_a, lru_b_a, lru_w_x, lru_b_x, lru_lam,
                gla_w_gate, gla_b_gate, gla_norm_g, w_branch, w_out):
    offsets = [int(o) for o in np.cumsum(IN_SIZES)[:-1]]
    pc = jnp.split(u_ctx @ w_in, offsets, axis=-1)
    pl = jnp.split(u_lat @ w_in, offsets, axis=-1)
    ssd_c, ssd_l = ssd_mixer(pc[0:3], pl[0:3], line_lat, ssd_conv_w, ssd_conv_b, ssd_dt_bias, ssd_a_log, ssd_d, ssd_norm_g)
    lru_c, lru_l = rglru_mixer(pc[3:5], pl[3:5], line_lat, lru_conv_w, lru_conv_b, lru_w_a, lru_b_a, lru_w_x, lru_b_x, lru_lam)
    gla_c, gla_l = gla_mixer(pc[5:10], pl[5:10], gla_w_gate, gla_b_gate, gla_norm_g)
    y_lat = merge_branches((ssd_l, lru_l, gla_l), pl[10], w_branch, w_out)
    y_ctx = merge_branches((ssd_c, lru_c, gla_c), pc[10], w_branch, w_out) if need_ctx else None
    return y_ctx, y_lat


def setup_inputs(seed: int = 0) -> dict:
    key = jax.random.key(seed)
    keys = list(jax.random.split(key, 40))

    def nrm(shape, scale):
        return jax.random.normal(keys.pop(), shape, jnp.float32) * scale

    def unif(shape, lo, hi):
        return jax.random.uniform(keys.pop(), shape, jnp.float32, lo, hi)

    L, D = DEPTH, D_MODEL
    dt = jnp.exp(unif((L, 2, SSD_HEADS), math.log(DT_MIN), math.log(DT_MAX)))
    a_pow = unif((L, 2, LRU_WIDTH), 0.9, 0.999)
    a_lru = a_pow ** (1.0 / LRU_C)
    return {
        'x': nrm((BATCH, SEQ, D), 1.0),
        'c': nrm((BATCH, D), 1.0),
        'ctx': nrm((BATCH, CTX_LEN, D), 1.0),
        'c_ctx': nrm((D,), 1.0),
        'w_ada': nrm((L, D, N_MOD * D), 0.5 * D ** -0.5),
        'b_ada': nrm((L, N_MOD * D), 0.01),
        'ln_g': 1.0 + nrm((L, 3, D), 0.02),
        'ln_b': nrm((L, 3, D), 0.02),
        'ffn_w_up': nrm((L, 2, D, 2 * D_FF), D ** -0.5),
        'ffn_w_down': nrm((L, 2, D_FF, D), BETA * D_FF ** -0.5),
        'w_in': nrm((L, D, IN_TOTAL), D ** -0.5),
        'ssd_conv_w': nrm((L, CONV_W, SSD_CONV_CH), CONV_W ** -0.5),
        'ssd_conv_b': nrm((L, SSD_CONV_CH), 0.02),
        'ssd_dt_bias': dt + jnp.log(-jnp.expm1(-dt)),
        'ssd_a_log': jnp.log(unif((L, 2, SSD_HEADS), 1.0, 16.0)),
        'ssd_d': 1.0 + nrm((L, 2, SSD_HEADS), 0.1),
        'ssd_norm_g': 1.0 + nrm((L, SSD_WIDTH), 0.02),
        'lru_conv_w': nrm((L, CONV_W, LRU_WIDTH), CONV_W ** -0.5),
        'lru_conv_b': nrm((L, LRU_WIDTH), 0.02),
        'lru_w_a': nrm((L, 2, LRU_BLOCKS, LRU_BLOCK, LRU_BLOCK), LRU_BLOCK ** -0.5),
        'lru_b_a': nrm((L, 2, LRU_WIDTH), 0.02),
        'lru_w_x': nrm((L, 2, LRU_BLOCKS, LRU_BLOCK, LRU_BLOCK), LRU_BLOCK ** -0.5),
        'lru_b_x': nrm((L, 2, LRU_WIDTH), 0.02),
        'lru_lam': jnp.log(a_lru) - jnp.log1p(-a_lru),
        'gla_w_gate': nrm((L, 2, GLA_GATE_RANK, GLA_KEY), GLA_GATE_RANK ** -0.5),
        'gla_b_gate': nrm((L, 2, GLA_KEY), 0.02),
        'gla_norm_g': 1.0 + nrm((L, GLA_DV), 0.02),
        'w_branch': nrm((L, N_BRANCH, BRANCH_W, D), BRANCH_W ** -0.5),
        'w_out': nrm((L, D, D), BETA * D ** -0.5),
    }


def reference(x, c, ctx, c_ctx, w_ada, b_ada, ln_g, ln_b, ffn_w_up, ffn_w_down, w_in, ssd_conv_w, ssd_conv_b,
              ssd_dt_bias, ssd_a_log, ssd_d, ssd_norm_g, lru_conv_w, lru_conv_b, lru_w_a, lru_b_a, lru_w_x, lru_b_x,
              lru_lam, gla_w_gate, gla_b_gate, gla_norm_g, w_branch, w_out):
    n_b, t_lat, _ = x.shape
    rows = t_lat // GRID_W
    s_lat = jax.nn.silu(c)
    s_ctx = jax.nn.silu(c_ctx)
    for l in range(DEPTH):
        last = l == DEPTH - 1
        m_lat = (s_lat @ w_ada[l] + b_ada[l]).reshape(n_b, 1, N_MOD, D_MODEL)
        m_ctx = (s_ctx @ w_ada[l] + b_ada[l]).reshape(1, 1, N_MOD, D_MODEL)
        x = ffn_sublayer(x, m_lat, 0, ffn_w_up[l, 0], ffn_w_down[l, 0], ln_g[l, 0], ln_b[l, 0])
        ctx = ffn_sublayer(ctx, m_ctx, 0, ffn_w_up[l, 0], ffn_w_down[l, 0], ln_g[l, 0], ln_b[l, 0])
        u_lat = modulate(x, m_lat[:, :, 3], m_lat[:, :, 4])
        u_ctx = modulate(ctx, m_ctx[:, :, 3], m_ctx[:, :, 4])
        col_major = l % 2 == 1
        line = rows if col_major else GRID_W
        if col_major:
            u_lat = raster_to_colmajor(u_lat, rows)
        y_ctx, y_lat = token_mixer(u_ctx, u_lat, line, not last, w_in[l], ssd_conv_w[l], ssd_conv_b[l], ssd_dt_bias[l],
                                   ssd_a_log[l], ssd_d[l], ssd_norm_g[l], lru_conv_w[l], lru_conv_b[l], lru_w_a[l],
                                   lru_b_a[l], lru_w_x[l], lru_b_x[l], lru_lam[l], gla_w_gate[l], gla_b_gate[l],
                                   gla_norm_g[l], w_branch[l], w_out[l])
        if col_major:
            y_lat = colmajor_to_raster(y_lat, rows)
        x = post_norm(x, y_lat, m_lat[:, :, 5], 1.0, ln_g[l, 1], ln_b[l, 1])
        if not last:
            ctx = post_norm(ctx, y_ctx, m_ctx[:, :, 5], 1.0, ln_g[l, 1], ln_b[l, 1])
            ctx = ffn_sublayer(ctx, m_ctx, 2, ffn_w_up[l, 1], ffn_w_down[l, 1], ln_g[l, 2], ln_b[l, 2])
        x = ffn_sublayer(x, m_lat, 2, ffn_w_up[l, 1], ffn_w_down[l, 1], ln_g[l, 2], ln_b[l, 2])
    return x
```

```python
import functools
import math

import jax
import jax.numpy as jnp
from jax import lax
from jax.experimental import pallas as pl
from jax.experimental.pallas import tpu as pltpu

F32 = jnp.float32
BF16 = jnp.bfloat16

GRID_W = 64
CONV_W = 4
CONV_LEFT = 2
SSD_HEAD_DIM = 64
SSD_GROUPS = 4
SSD_STATE = 128
LRU_BLOCK = 64
LRU_C = 8.0
GLA_HEADS = 4
GLA_GATE_RANK = 16
GLA_TAU = 16.0
N_MOD = 9
N_BRANCH = 3
FFN_RES_W = 0.5
NORM_EPS = 1e-5

TOKEN_BLOCK = 256
SCAN_CHUNK = 128
MXU_TILE = 256
LANES = 128
SUBLANES = 8
VMEM_LIMIT = 56 * 1024 * 1024


def _cparams(n_axes):
    return pltpu.CompilerParams(dimension_semantics=("arbitrary",) * n_axes,
                                vmem_limit_bytes=VMEM_LIMIT)


def _resident(block_shape, index_map):
    return pl.BlockSpec(block_shape, index_map, pipeline_mode=pl.Buffered(1))


def _silu(x):
    return x * jax.nn.sigmoid(x)


def _softplus(x):
    return jnp.maximum(x, 0.0) + jnp.log1p(jnp.exp(-jnp.abs(x)))


def _log_sigmoid(x):
    return -_softplus(-x)


def _split3(x):
    x0 = x.astype(BF16)
    r1 = x - x0.astype(F32)
    x1 = r1.astype(BF16)
    r2 = r1 - x1.astype(F32)
    return x0, x1, r2.astype(BF16)


def _dot(a, b):
    return jnp.dot(a, b, preferred_element_type=F32)


def _exact_lhs_dot(t_bf16, x):
    x0, x1, x2 = _split3(x)
    return _dot(t_bf16, x0) + _dot(t_bf16, x1) + _dot(t_bf16, x2)


def _exact_rhs_dot(x, e3_bf16):
    return _dot(jnp.concatenate(_split3(x), axis=1), e3_bf16)


def _layer_norm(z, g, b):
    mu = jnp.mean(z, axis=-1, keepdims=True)
    zc = z - mu
    var = jnp.mean(zc * zc, axis=-1, keepdims=True)
    return zc * lax.rsqrt(var + NORM_EPS) * g + b


def _scan_order(i, rev, n_ctx, n_tot):
    back = jnp.where(i < n_ctx, n_ctx - 1 - i, n_ctx + n_tot - 1 - i)
    return jnp.where(rev == 0, i, back)


def _ada_kernel(s_ref, w_ref, b_ref, o_ref):
    s = _silu(s_ref[...])
    o_ref[...] = jnp.dot(s, w_ref[...], preferred_element_type=F32,
                         precision=lax.Precision.HIGHEST) + b_ref[...]


def _ada_call(cond, w_ada, b_ada):
    n_layers, d, nd = w_ada.shape
    rows = cond.shape[0]
    n_col = nd // d
    return pl.pallas_call(
        _ada_kernel,
        out_shape=jax.ShapeDtypeStruct((n_layers, rows, nd), F32),
        grid=(n_layers, n_col),
        in_specs=[
            pl.BlockSpec((rows, d), lambda l, j: (0, 0)),
            pl.BlockSpec((None, d, d), lambda l, j: (l, 0, j)),
            pl.BlockSpec((None, 1, d), lambda l, j: (l, 0, j)),
        ],
        out_specs=pl.BlockSpec((None, rows, d), lambda l, j: (l, 0, j)),
        compiler_params=_cparams(2),
        name="ada_mod",
    )(cond, w_ada, b_ada.reshape(n_layers, 1, nd))


def _ffn_kernel(x_ref, m_ref, wup_ref, wdn_ref, g_ref, b_ref, o_ref, *, j, alpha, d_ff, ff_chunk):
    x = x_ref[...]
    shift = m_ref[3 * j:3 * j + 1, :]
    scale = m_ref[3 * j + 1:3 * j + 2, :]
    gate = m_ref[3 * j + 2:3 * j + 3, :]
    u = (x * (1.0 + scale) + shift).astype(BF16)
    acc = None
    for c0 in range(0, d_ff, ff_chunk):
        a = _dot(u, wup_ref[:, c0:c0 + ff_chunk])
        v = _dot(u, wup_ref[:, d_ff + c0:d_ff + c0 + ff_chunk])
        hid = (_silu(a) * v).astype(BF16)
        part = _dot(hid, wdn_ref[c0:c0 + ff_chunk, :])
        acc = part if acc is None else acc + part
    z = alpha * x + (FFN_RES_W * gate) * acc
    o_ref[...] = _layer_norm(z, g_ref[...], b_ref[...])


def _ffn_call(h, mods, w_up, w_dn, ln_g, ln_b, *, j, n_ctx_blocks, alpha):
    n_b, tt, d = h.shape
    d_ff = w_dn.shape[0]
    ff_chunk = d_ff // 2 if (d_ff // 2) % LANES == 0 else d_ff
    ctx_row = mods.shape[0] - 1
    n_blocks = tt // TOKEN_BLOCK

    def mod_map(b, t):
        return (jnp.where(t < n_ctx_blocks, ctx_row, b), 0, 0)

    kern = functools.partial(_ffn_kernel, j=j, alpha=alpha, d_ff=d_ff, ff_chunk=ff_chunk)
    return pl.pallas_call(
        kern,
        out_shape=jax.ShapeDtypeStruct(h.shape, F32),
        grid=(n_b, n_blocks),
        in_specs=[
            pl.BlockSpec((None, TOKEN_BLOCK, d), lambda b, t: (b, t, 0)),
            pl.BlockSpec((None, N_MOD, d), mod_map),
            _resident((d, 2 * d_ff), lambda b, t: (0, 0)),
            _resident((d_ff, d), lambda b, t: (0, 0)),
            _resident((1, d), lambda b, t: (0, 0)),
            _resident((1, d), lambda b, t: (0, 0)),
        ],
        out_specs=pl.BlockSpec((None, TOKEN_BLOCK, d), lambda b, t: (b, t, 0)),
        compiler_params=_cparams(2),
        name="ffn_sublayer",
    )(h, mods, w_up, w_dn, ln_g.reshape(1, d), ln_b.reshape(1, d))


def _proj_layout(d, ssd_heads):
    ssd_w = d
    bc_w = SSD_GROUPS * SSD_STATE
    gla_key = d // 2
    names = [("ssd_x", ssd_w), ("ssd_b", bc_w), ("ssd_c", bc_w), ("lru_x", d), ("ssd_z", ssd_w),
             ("lru_g", d), ("gla_q", gla_key), ("gla_k", gla_key), ("gla_v", d), ("gla_g", d),
             ("gates", N_BRANCH * d), ("small", LANES)]
    off, out = 0, {}
    for name, width in names:
        out[name] = (off, width)
        off += width
    return out, off


def _proj_kernel(x_ref, m_ref, w_ref, cw_ref, cb_ref, o_ref, *, conv_cols, silu_cols, total_cols,
                 col_chunk, n_ctx_blocks, line_ctx, line_lat):
    x = x_ref[...]
    u = (x * (1.0 + m_ref[4:5, :]) + m_ref[3:4, :]).astype(BF16)
    t = pl.program_id(1)
    line = jnp.where(t < n_ctx_blocks, line_ctx, line_lat)
    rows = x.shape[0]
    for c0 in range(0, total_cols, col_chunk):
        cw = min(col_chunk, total_cols - c0)
        y = _dot(u, w_ref[:, c0:c0 + cw])
        if c0 < conv_cols:
            pos = lax.broadcasted_iota(jnp.int32, (rows, cw), 0) & (line - 1)
            acc = cb_ref[:, c0:c0 + cw] + y * cw_ref[CONV_LEFT:CONV_LEFT + 1, c0:c0 + cw]
            for k in range(CONV_W):
                off = k - CONV_LEFT
                if off == 0:
                    continue
                shifted = pltpu.roll(y, (-off) % rows, axis=0)
                valid = (pos + off >= 0) & (pos + off < line)
                acc = acc + jnp.where(valid, shifted, 0.0) * cw_ref[k:k + 1, c0:c0 + cw]
            y = _silu(acc) if c0 < silu_cols else acc
        o_ref[:, c0:c0 + cw] = y


def _proj_call(h, mods, w_perm, conv_w, conv_b, *, conv_cols, silu_cols, n_ctx_blocks, line_ctx, line_lat):
    n_b, tt, d = h.shape
    total = w_perm.shape[1]
    ctx_row = mods.shape[0] - 1
    n_blocks = tt // TOKEN_BLOCK

    def mod_map(b, t):
        return (jnp.where(t < n_ctx_blocks, ctx_row, b), 0, 0)

    kern = functools.partial(_proj_kernel, conv_cols=conv_cols, silu_cols=silu_cols, total_cols=total,
                             col_chunk=1024, n_ctx_blocks=n_ctx_blocks, line_ctx=line_ctx, line_lat=line_lat)
    return pl.pallas_call(
        kern,
        out_shape=jax.ShapeDtypeStruct((n_b, tt, total), F32),
        grid=(n_b, n_blocks),
        in_specs=[
            pl.BlockSpec((None, TOKEN_BLOCK, d), lambda b, t: (b, t, 0)),
            pl.BlockSpec((None, N_MOD, d), mod_map),
            _resident((d, total), lambda b, t: (0, 0)),
            _resident((CONV_W, conv_cols), lambda b, t: (0, 0)),
            _resident((1, conv_cols), lambda b, t: (0, 0)),
        ],
        out_specs=pl.BlockSpec((None, TOKEN_BLOCK, total), lambda b, t: (b, t, 0)),
        compiler_params=_cparams(2),
        name="mixer_in_proj",
    )(h, mods, w_perm, conv_w, conv_b)


def _ssd_kernel(x_ref, b_ref, c_ref, sm_ref, dtb_ref, alog_ref, exp_ref, y_ref, state_ref, *, n_heads):
    rev = pl.program_id(1)
    i = pl.program_id(2)
    q = x_ref.shape[0]
    hg = n_heads // SSD_GROUPS
    gw = hg * SSD_HEAD_DIM

    @pl.when(i == 0)
    def _():
        state_ref[...] = jnp.zeros_like(state_ref)

    fwd = rev == 0
    sm = sm_ref[...]
    dt_raw = jnp.where(fwd, sm[:, 0:n_heads], sm[:, n_heads:2 * n_heads])
    dt_bias = jnp.where(fwd, dtb_ref[0:1, :], dtb_ref[1:2, :])
    a_log = jnp.where(fwd, alog_ref[0:1, :], alog_ref[1:2, :])
    dt = _softplus(dt_raw + dt_bias)
    a = dt * (-jnp.exp(a_log))
    row = lax.broadcasted_iota(jnp.int32, (q, q), 0)
    col = lax.broadcasted_iota(jnp.int32, (q, q), 1)
    mask = (col - row) * jnp.where(fwd, 1, -1) <= 0
    tri = jnp.where(mask, 1.0, 0.0).astype(BF16)
    cs = _exact_lhs_dot(tri, a)
    tot = jnp.sum(a, axis=0, keepdims=True)
    pad = jnp.zeros((q, LANES - 2 * n_heads), F32)
    both_t = jnp.concatenate([cs, dt, pad], axis=1).T
    cs_t = both_t[0:n_heads]
    dt_t = both_t[n_heads:2 * n_heads]

    w_state = dt * jnp.exp(tot - cs)
    from_start = jnp.exp(cs)
    e_tot = jnp.broadcast_to(jnp.exp(tot), (SUBLANES, n_heads))
    wide = _exact_rhs_dot(jnp.concatenate([w_state, from_start, e_tot], axis=0), exp_ref[...])
    w_state_w = wide[0:q]
    from_start_w = wide[q:2 * q]
    e_tot_w = wide[2 * q:2 * q + 1]

    x = x_ref[...]
    x_bf = x.astype(BF16)
    xw_bf = (x * w_state_w).astype(BF16)
    for g in range(SSD_GROUPS):
        bg = b_ref[:, g * SSD_STATE:(g + 1) * SSD_STATE].astype(BF16)
        cg = c_ref[:, g * SSD_STATE:(g + 1) * SSD_STATE].astype(BF16)
        scores = lax.dot_general(cg, bg, (((1,), (1,)), ((), ())), preferred_element_type=F32)
        s_prev = state_ref[g]
        y_off = _dot(cg, s_prev.astype(BF16)) * from_start_w[:, g * gw:(g + 1) * gw]
        parts = []
        for hh in range(hg):
            head = g * hg + hh
            seg = cs[:, head:head + 1] - cs_t[head:head + 1, :]
            m = jnp.where(mask, scores * jnp.exp(jnp.where(mask, seg, 0.0)) * dt_t[head:head + 1, :], 0.0)
            parts.append(_dot(m.astype(BF16), x_bf[:, head * SSD_HEAD_DIM:(head + 1) * SSD_HEAD_DIM]))
        y_ref[:, g * gw:(g + 1) * gw] = y_off + jnp.concatenate(parts, axis=1)
        upd = lax.dot_general(bg, xw_bf[:, g * gw:(g + 1) * gw], (((0,), (0,)), ((), ())),
                              preferred_element_type=F32)
        state_ref[g] = s_prev * e_tot_w[:, g * gw:(g + 1) * gw] + upd


def _ssd_call(p, layout, dt_bias, a_log, expand, *, n_ctx_chunks):
    n_b, tt, _ = p.shape
    n_heads = dt_bias.shape[1]
    width = n_heads * SSD_HEAD_DIM
    n_chunks = tt // SCAN_CHUNK
    hg = n_heads // SSD_GROUPS

    def col_spec(name):
        off, w = layout[name]
        blk = off // w
        assert blk * w == off
        return pl.BlockSpec((None, SCAN_CHUNK, w),
                            lambda b, r, i: (b, _scan_order(i, r, n_ctx_chunks, n_chunks), blk))

    kern = functools.partial(_ssd_kernel, n_heads=n_heads)
    return pl.pallas_call(
        kern,
        out_shape=jax.ShapeDtypeStruct((2, n_b, tt, width), F32),
        grid=(n_b, 2, n_chunks),
        in_specs=[
            col_spec("ssd_x"), col_spec("ssd_b"), col_spec("ssd_c"), col_spec("small"),
            _resident((2, n_heads), lambda b, r, i: (0, 0)),
            _resident((2, n_heads), lambda b, r, i: (0, 0)),
            _resident(expand.shape, lambda b, r, i: (0, 0)),
        ],
        out_specs=pl.BlockSpec((None, None, SCAN_CHUNK, width),
                               lambda b, r, i: (r, b, _scan_order(i, r, n_ctx_chunks, n_chunks), 0)),
        scratch_shapes=[pltpu.VMEM((SSD_GROUPS, SSD_STATE, hg * SSD_HEAD_DIM), F32)],
        compiler_params=_cparams(3),
        name="ssd_scan",
    )(p, p, p, p, dt_bias, a_log, expand)


def _lru_scan_block(a_ref, b_ref, o_ref, carry_ref, reverse):
    rows, width = a_ref.shape
    a = a_ref[...]
    b = b_ref[...]
    sub = lax.broadcasted_iota(jnp.int32, (rows, width), 0) & (SUBLANES - 1)
    k = 1
    while k < SUBLANES:
        if reverse:
            a_sh = pltpu.roll(a, rows - k, axis=0)
            b_sh = pltpu.roll(b, rows - k, axis=0)
            ok = sub < SUBLANES - k
        else:
            a_sh = pltpu.roll(a, k, axis=0)
            b_sh = pltpu.roll(b, k, axis=0)
            ok = sub >= k
        b = b + a * jnp.where(ok, b_sh, 0.0)
        a = a * jnp.where(ok, a_sh, 1.0)
        k *= 2
    a_ref[...] = a
    b_ref[...] = b
    carry = carry_ref[...]
    n_groups = rows // SUBLANES
    order = range(n_groups - 1, -1, -1) if reverse else range(n_groups)
    for gi in order:
        sl = slice(gi * SUBLANES, (gi + 1) * SUBLANES)
        hblk = b_ref[sl, :] + a_ref[sl, :] * carry
        o_ref[sl, :] = hblk
        carry = hblk[0:1, :] if reverse else hblk[SUBLANES - 1:SUBLANES, :]
    carry_ref[...] = carry


def _lru_kernel(x_ref, w_ref, ba_ref, bx_ref, lam_ref, h_ref, a_s, b_s, carry_ref):
    rev = pl.program_id(1)
    i = pl.program_id(2)

    @pl.when(i == 0)
    def _():
        carry_ref[...] = jnp.zeros_like(carry_ref)

    x = x_ref[...]
    x_bf = x.astype(BF16)
    n_tiles = w_ref.shape[0]
    pre_a, pre_x = [], []
    for t in range(n_tiles):
        y = _dot(x_bf[:, t * MXU_TILE:(t + 1) * MXU_TILE], w_ref[t])
        pre_a.append(y[:, :MXU_TILE])
        pre_x.append(y[:, MXU_TILE:])
    r = jax.nn.sigmoid(jnp.concatenate(pre_a, axis=1) + ba_ref[...])
    ig = jax.nn.sigmoid(jnp.concatenate(pre_x, axis=1) + bx_ref[...])
    log_a = (-LRU_C) * r * _softplus(-lam_ref[...])
    a = jnp.exp(log_a)
    a_s[...] = a
    b_s[...] = jnp.sqrt(-jnp.tanh(log_a) * (a * a + 1.0)) * (ig * x)

    @pl.when(rev == 0)
    def _():
        _lru_scan_block(a_s, b_s, h_ref, carry_ref, reverse=False)

    @pl.when(rev == 1)
    def _():
        _lru_scan_block(a_s, b_s, h_ref, carry_ref, reverse=True)


def _lru_call(p, layout, w_tiles, b_a, b_x, lam, *, n_ctx_blocks):
    n_b, tt, _ = p.shape
    off, width = layout["lru_x"]
    blk = off // width
    assert blk * width == off
    n_blocks = tt // TOKEN_BLOCK
    n_tiles = w_tiles.shape[1]

    def tok_map(b, r, i):
        return (b, _scan_order(i, r, n_ctx_blocks, n_blocks), blk)

    return pl.pallas_call(
        _lru_kernel,
        out_shape=jax.ShapeDtypeStruct((2, n_b, tt, width), F32),
        grid=(n_b, 2, n_blocks),
        in_specs=[
            pl.BlockSpec((None, TOKEN_BLOCK, width), tok_map),
            pl.BlockSpec((None, n_tiles, MXU_TILE, 2 * MXU_TILE), lambda b, r, i: (r, 0, 0, 0)),
            pl.BlockSpec((None, 1, width), lambda b, r, i: (r, 0, 0)),
            pl.BlockSpec((None, 1, width), lambda b, r, i: (r, 0, 0)),
            pl.BlockSpec((None, 1, width), lambda b, r, i: (r, 0, 0)),
        ],
        out_specs=pl.BlockSpec((None, None, TOKEN_BLOCK, width),
                               lambda b, r, i: (r, b, _scan_order(i, r, n_ctx_blocks, n_blocks), 0)),
        scratch_shapes=[pltpu.VMEM((TOKEN_BLOCK, width), F32), pltpu.VMEM((TOKEN_BLOCK, width), F32),
                        pltpu.VMEM((1, width), F32)],
        compiler_params=_cparams(3),
        name="rglru_scan",
    )(p, w_tiles, b_a, b_x, lam)


def _gla_kernel(q_ref, k_ref, v_ref, sm_ref, wg_ref, bg_ref, o_ref, state_ref, *, alr_off, dk, dv):
    rev = pl.program_id(1)
    i = pl.program_id(2)
    q_len = q_ref.shape[0]
    n_heads = q_ref.shape[1] // dk

    @pl.when(i == 0)
    def _():
        state_ref[...] = jnp.zeros_like(state_ref)

    fwd = rev == 0
    sm = sm_ref[...]
    r = GLA_GATE_RANK
    a_lr = jnp.where(fwd, sm[:, alr_off:alr_off + r], sm[:, alr_off + r:alr_off + 2 * r])
    z = jnp.dot(a_lr, wg_ref[...], preferred_element_type=F32, precision=lax.Precision.HIGHEST) + bg_ref[...]
    log_a = _log_sigmoid(z) * (1.0 / GLA_TAU)
    row = lax.broadcasted_iota(jnp.int32, (q_len, q_len), 0)
    col = lax.broadcasted_iota(jnp.int32, (q_len, q_len), 1)
    mask = (col - row) * jnp.where(fwd, 1, -1) <= 0
    tri = jnp.where(mask, 1.0, 0.0).astype(BF16)
    bcum = _exact_lhs_dot(tri, log_a)
    btot = jnp.sum(log_a, axis=0, keepdims=True)
    bmid = bcum[q_len // 2:q_len // 2 + 1, :]
    q_in = q_ref[...] * (dk ** -0.5) * jnp.exp(bcum - bmid)
    k_in = k_ref[...] * jnp.exp(bmid - bcum)
    q_off = (q_in * jnp.exp(bmid)).astype(BF16)
    k_st = (k_in * jnp.exp(btot - bmid)).astype(BF16)
    q_in = q_in.astype(BF16)
    k_in = k_in.astype(BF16)
    e_tot = jnp.exp(btot)
    for hh in range(n_heads):
        ks = slice(hh * dk, (hh + 1) * dk)
        vs = slice(hh * dv, (hh + 1) * dv)
        v_bf = v_ref[:, vs].astype(BF16)
        att = lax.dot_general(q_in[:, ks], k_in[:, ks], (((1,), (1,)), ((), ())), preferred_element_type=F32)
        att = jnp.where(mask, att, 0.0).astype(BF16)
        s_prev = state_ref[hh]
        inter = lax.dot_general(q_off[:, ks], s_prev.astype(BF16), (((1,), (1,)), ((), ())),
                                preferred_element_type=F32)
        o_ref[:, vs] = _dot(att, v_bf) + inter
        upd = lax.dot_general(v_bf, k_st[:, ks], (((0,), (0,)), ((), ())), preferred_element_type=F32)
        state_ref[hh] = s_prev * e_tot[:, ks] + upd


def _gla_call(p, layout, w_gate, b_gate, *, alr_off, n_ctx_chunks):
    n_b, tt, _ = p.shape
    key_w = layout["gla_q"][1]
    val_w = layout["gla_v"][1]
    dk = key_w // GLA_HEADS
    dv = val_w // GLA_HEADS
    n_chunks = tt // SCAN_CHUNK

    def col_spec(name):
        off, w = layout[name]
        blk = off // w
        assert blk * w == off
        return pl.BlockSpec((None, SCAN_CHUNK, w),
                            lambda b, r, i: (b, _scan_order(i, r, n_ctx_chunks, n_chunks), blk))

    kern = functools.partial(_gla_kernel, alr_off=alr_off, dk=dk, dv=dv)
    return pl.pallas_call(
        kern,
        out_shape=jax.ShapeDtypeStruct((2, n_b, tt, val_w), F32),
        grid=(n_b, 2, n_chunks),
        in_specs=[
            col_spec("gla_q"), col_spec("gla_k"), col_spec("gla_v"), col_spec("small"),
            pl.BlockSpec((None, GLA_GATE_RANK, key_w), lambda b, r, i: (r, 0, 0)),
            pl.BlockSpec((None, 1, key_w), lambda b, r, i: (r, 0, 0)),
        ],
        out_specs=pl.BlockSpec((None, None, SCAN_CHUNK, val_w),
                               lambda b, r, i: (r, b, _scan_order(i, r, n_ctx_chunks, n_chunks), 0)),
        scratch_shapes=[pltpu.VMEM((GLA_HEADS, dv, dk), F32)],
        compiler_params=_cparams(3),
        name="gla_scan",
    )(p, p, p, p, w_gate, b_gate)


def _merge_kernel(h_ref, m_ref, ys0_ref, ys1_ref, xs_ref, z_ref, hl0_ref, hl1_ref, gl_ref,
                  og0_ref, og1_ref, gg_ref, gt0_ref, gt1_ref, gt2_ref, dskip_ref, sng_ref, gng_ref,
                  wbr_ref, wout_ref, lng_ref, lnb_ref, o_ref, *, alpha, dv):
    d = h_ref.shape[1]
    y = (ys0_ref[...] + ys1_ref[...] + dskip_ref[...] * xs_ref[...]) * _silu(z_ref[...])
    y = y * lax.rsqrt(jnp.mean(y * y, axis=-1, keepdims=True) + NORM_EPS) * sng_ref[...]
    m = jax.nn.sigmoid(gt0_ref[...]) * _dot(y.astype(BF16), wbr_ref[0])
    y = (hl0_ref[...] + hl1_ref[...]) * jax.nn.gelu(gl_ref[...])
    m = m + jax.nn.sigmoid(gt1_ref[...]) * _dot(y.astype(BF16), wbr_ref[1])
    parts = []
    for hh in range(d // dv):
        vs = slice(hh * dv, (hh + 1) * dv)
        o = og0_ref[:, vs] + og1_ref[:, vs]
        o = o * lax.rsqrt(jnp.mean(o * o, axis=-1, keepdims=True) + NORM_EPS) * gng_ref[...]
        parts.append(o * _silu(gg_ref[:, vs]))
    y = jnp.concatenate(parts, axis=1)
    m = m + jax.nn.sigmoid(gt2_ref[...]) * _dot(y.astype(BF16), wbr_ref[2])
    out = _dot(m.astype(BF16), wout_ref[...])
    z = alpha * h_ref[...] + m_ref[5:6, :] * out
    o_ref[...] = _layer_norm(z, lng_ref[...], lnb_ref[...])


def _merge_call(h, mods, p, layout, y_ssd, h_lru, o_gla, d_skip_w, ssd_norm_g, gla_norm_g, w_branch, w_out,
                ln_g, ln_b, *, n_ctx_blocks, skip_blocks, alpha):
    n_b, tt, d = h.shape
    n_blocks = tt // TOKEN_BLOCK - skip_blocks
    ctx_row = mods.shape[0] - 1
    dv = gla_norm_g.shape[-1]

    def mod_map(b, t):
        return (jnp.where(t + skip_blocks < n_ctx_blocks, ctx_row, b), 0, 0)

    def tok(width, blk):
        return pl.BlockSpec((None, TOKEN_BLOCK, width), lambda b, t: (b, t + skip_blocks, blk))

    def col_spec(name):
        off, w = layout[name]
        blk = off // w
        assert blk * w == off
        return tok(w, blk)

    def gate_spec(n):
        off = layout["gates"][0] + n * d
        assert off % d == 0
        return tok(d, off // d)

    def dir_spec(rv):
        return pl.BlockSpec((None, None, TOKEN_BLOCK, d), lambda b, t: (rv, b, t + skip_blocks, 0))

    kern = functools.partial(_merge_kernel, alpha=alpha, dv=dv)
    return pl.pallas_call(
        kern,
        out_shape=jax.ShapeDtypeStruct((n_b, n_blocks * TOKEN_BLOCK, d), F32),
        grid=(n_b, n_blocks),
        in_specs=[
            tok(d, 0),
            pl.BlockSpec((None, N_MOD, d), mod_map),
            dir_spec(0), dir_spec(1), col_spec("ssd_x"), col_spec("ssd_z"),
            dir_spec(0), dir_spec(1), col_spec("lru_g"),
            dir_spec(0), dir_spec(1), col_spec("gla_g"),
            gate_spec(0), gate_spec(1), gate_spec(2),
            _resident((1, d), lambda b, t: (0, 0)),
            _resident((1, d), lambda b, t: (0, 0)),
            _resident((1, dv), lambda b, t: (0, 0)),
            _resident((N_BRANCH, d, d), lambda b, t: (0, 0, 0)),
            _resident((d, d), lambda b, t: (0, 0)),
            _resident((1, d), lambda b, t: (0, 0)),
            _resident((1, d), lambda b, t: (0, 0)),
        ],
        out_specs=pl.BlockSpec((None, TOKEN_BLOCK, d), lambda b, t: (b, t, 0)),
        compiler_params=_cparams(2),
        name="mixer_merge",
    )(h, mods, y_ssd, y_ssd, p, p, h_lru, h_lru, p, o_gla, o_gla, p, p, p, p,
      d_skip_w, ssd_norm_g.reshape(1, d), gla_norm_g.reshape(1, dv), w_branch, w_out,
      ln_g.reshape(1, d), ln_b.reshape(1, d))


def _permute_w_in(w_in_l, d, ssd_heads, layout, total):
    ssd_w = d
    bc_w = SSD_GROUPS * SSD_STATE
    gla_key = d // 2
    sizes = (ssd_w, ssd_w + 2 * bc_w, 2 * ssd_heads, d, d, gla_key, gla_key, d, d, 2 * GLA_GATE_RANK, N_BRANCH * d)
    offs = [0]
    for s in sizes:
        offs.append(offs[-1] + s)
    z, xbc, dtr, lx, lg, gq, gk, gv, gg, alr, gates = [w_in_l[:, offs[n]:offs[n + 1]] for n in range(len(sizes))]
    small = jnp.concatenate([dtr, alr, jnp.zeros((d, LANES - dtr.shape[1] - alr.shape[1]), w_in_l.dtype)], axis=1)
    w = jnp.concatenate([xbc, lx, z, lg, gq, gk, gv, gg, gates, small], axis=1)
    assert w.shape[1] == total
    return w.astype(BF16)


def _lru_gate_tiles(w_a, w_x):
    per = MXU_TILE // LRU_BLOCK
    n_dir, nb, k, _ = w_a.shape
    n_tiles = nb // per

    def tiles(w):
        w = w.reshape(n_dir, n_tiles, per, k, k)
        eye = jnp.eye(per, dtype=w.dtype)
        t = jnp.einsum("dtpij,pq->dtpiqj", w, eye)
        return t.reshape(n_dir, n_tiles, MXU_TILE, MXU_TILE)

    return jnp.concatenate([tiles(w_a), tiles(w_x)], axis=-1).astype(BF16)


def _to_colmajor(h, n_ctx, rows):
    n_b, _, d = h.shape
    lat = h[:, n_ctx:].reshape(n_b, rows, GRID_W, d).transpose(0, 2, 1, 3).reshape(n_b, rows * GRID_W, d)
    return jnp.concatenate([h[:, :n_ctx], lat], axis=1)


def _to_raster(h, n_ctx, rows):
    n_b, _, d = h.shape
    lat = h[:, n_ctx:].reshape(n_b, GRID_W, rows, d).transpose(0, 2, 1, 3).reshape(n_b, rows * GRID_W, d)
    return jnp.concatenate([h[:, :n_ctx], lat], axis=1)


def kernel(x, c, ctx, c_ctx, w_ada, b_ada, ln_g, ln_b, ffn_w_up, ffn_w_down, w_in, ssd_conv_w, ssd_conv_b,
           ssd_dt_bias, ssd_a_log, ssd_d, ssd_norm_g, lru_conv_w, lru_conv_b, lru_w_a, lru_b_a, lru_w_x, lru_b_x,
           lru_lam, gla_w_gate, gla_b_gate, gla_norm_g, w_branch, w_out):
    n_b, t_lat, d = x.shape
    n_ctx = ctx.shape[1]
    depth = w_ada.shape[0]
    rows = t_lat // GRID_W
    ssd_heads = ssd_dt_bias.shape[-1]
    assert n_ctx % TOKEN_BLOCK == 0 and t_lat % TOKEN_BLOCK == 0
    for line in (n_ctx, GRID_W, rows):
        assert line & (line - 1) == 0 and TOKEN_BLOCK % min(line, TOKEN_BLOCK) == 0
    assert n_ctx <= TOKEN_BLOCK
    alpha = (2.0 * depth) ** 0.25
    n_ctx_blocks = n_ctx // TOKEN_BLOCK
    n_ctx_chunks = n_ctx // SCAN_CHUNK

    layout, total = _proj_layout(d, ssd_heads)
    conv_cols = layout["lru_x"][0] + layout["lru_x"][1]
    silu_cols = layout["lru_x"][0]
    alr_off = 2 * ssd_heads

    n_rows = -(-(n_b + 1) // SUBLANES) * SUBLANES
    cond = jnp.concatenate([c, jnp.zeros((n_rows - n_b - 1, d), F32), c_ctx[None, :]], axis=0)
    mods_all = _ada_call(cond, w_ada, b_ada).reshape(depth, n_rows, N_MOD, d)

    expand = jnp.repeat(jnp.eye(ssd_heads, dtype=BF16), SSD_HEAD_DIM, axis=1)
    expand = jnp.concatenate([expand] * 3, axis=0)

    h = jnp.concatenate([ctx, x], axis=1)
    for l in range(depth):
        last = l == depth - 1
        col_major = l % 2 == 1
        mods = mods_all[l]
        wup = ffn_w_up[l].astype(BF16)
        wdn = ffn_w_down[l].astype(BF16)

        if col_major:
            h = _to_colmajor(h, n_ctx, rows)
        h = _ffn_call(h, mods, wup[0], wdn[0], ln_g[l, 0], ln_b[l, 0], j=0, n_ctx_blocks=n_ctx_blocks, alpha=alpha)

        w_perm = _permute_w_in(w_in[l], d, ssd_heads, layout, total)
        conv_w = jnp.concatenate([ssd_conv_w[l], lru_conv_w[l]], axis=1)
        conv_b = jnp.concatenate([ssd_conv_b[l], lru_conv_b[l]], axis=0)[None, :]
        p = _proj_call(h, mods, w_perm, conv_w, conv_b, conv_cols=conv_cols, silu_cols=silu_cols,
                       n_ctx_blocks=n_ctx_blocks, line_ctx=n_ctx, line_lat=rows if col_major else GRID_W)

        y_ssd = _ssd_call(p, layout, ssd_dt_bias[l], ssd_a_log[l], expand, n_ctx_chunks=n_ctx_chunks)
        h_lru = _lru_call(p, layout, _lru_gate_tiles(lru_w_a[l], lru_w_x[l]), lru_b_a[l][:, None, :],
                          lru_b_x[l][:, None, :], lru_lam[l][:, None, :], n_ctx_blocks=n_ctx_blocks)
        o_gla = _gla_call(p, layout, gla_w_gate[l], gla_b_gate[l][:, None, :], alr_off=alr_off,
                          n_ctx_chunks=n_ctx_chunks)

        d_skip_w = jnp.repeat(ssd_d[l, 0] + ssd_d[l, 1], SSD_HEAD_DIM)[None, :]
        skip_blocks = n_ctx_blocks if last else 0
        h = _merge_call(h, mods, p, layout, y_ssd, h_lru, o_gla, d_skip_w, ssd_norm_g[l], gla_norm_g[l],
                        w_branch[l].astype(BF16), w_out[l].astype(BF16), ln_g[l, 1], ln_b[l, 1],
                        n_ctx_blocks=n_ctx_blocks, skip_blocks=skip_blocks, alpha=alpha)
        n_ctx_now = 0 if last else n_ctx
        h = _ffn_call(h, mods, wup[1], wdn[1], ln_g[l, 2], ln_b[l, 2], j=2,
                      n_ctx_blocks=n_ctx_now // TOKEN_BLOCK, alpha=alpha)
        if col_major:
            h = _to_raster(h, n_ctx_now, rows)
    return h
```

```python
import functools
import math

import jax
import jax.numpy as jnp
from jax import lax
from jax.experimental import pallas as pl
from jax.experimental.pallas import tpu as pltpu

F32 = jnp.float32
BF16 = jnp.bfloat16

GRID_W = 64
CONV_W = 4
CONV_LEFT = 2
SSD_HEAD_DIM = 64
SSD_GROUPS = 4
SSD_STATE = 128
LRU_BLOCK = 64
LRU_C = 8.0
GLA_HEADS = 4
GLA_GATE_RANK = 16
GLA_TAU = 16.0
N_MOD = 9
N_BRANCH = 3
FFN_RES_W = 0.5
NORM_EPS = 1e-5

TOKEN_BLOCK = 256
SCAN_CHUNK = 128
SCAN_BATCH = 2
MXU_TILE = 256
LANES = 128
SUBLANES = 8
VMEM_LIMIT = 56 * 1024 * 1024


def _cparams(n_axes):
    return pltpu.CompilerParams(dimension_semantics=("arbitrary",) * n_axes,
                                vmem_limit_bytes=VMEM_LIMIT)


def _resident(block_shape, index_map):
    return pl.BlockSpec(block_shape, index_map, pipeline_mode=pl.Buffered(1))


def _sigmoid(x):
    return 0.5 * jnp.tanh(0.5 * x) + 0.5


def _silu(x):
    return x * _sigmoid(x)


def _softplus(x):
    return jnp.maximum(x, 0.0) + jnp.log1p(jnp.exp(-jnp.abs(x)))


def _log_sigmoid(x):
    return -_softplus(-x)


def _split3(x):
    x0 = x.astype(BF16)
    r1 = x - x0.astype(F32)
    x1 = r1.astype(BF16)
    r2 = r1 - x1.astype(F32)
    return x0, x1, r2.astype(BF16)


def _dot(a, b):
    return jnp.dot(a, b, preferred_element_type=F32)


def _exact_lhs_dot(t_bf16, x):
    x0, x1, x2 = _split3(x)
    return _dot(t_bf16, x0) + _dot(t_bf16, x1) + _dot(t_bf16, x2)


def _exact_rhs_dot(x, e3_bf16):
    return _dot(jnp.concatenate(_split3(x), axis=1), e3_bf16)


def _layer_norm(z, g, b):
    mu = jnp.mean(z, axis=-1, keepdims=True)
    zc = z - mu
    var = jnp.mean(zc * zc, axis=-1, keepdims=True)
    return zc * lax.rsqrt(var + NORM_EPS) * g + b


def _scan_order(i, rev, n_ctx, n_tot):
    back = jnp.where(i < n_ctx, n_ctx - 1 - i, n_ctx + n_tot - 1 - i)
    return jnp.where(rev == 0, i, back)


def _ada_kernel(s_ref, w_ref, b_ref, o_ref):
    s = _silu(s_ref[...])
    o_ref[...] = jnp.dot(s, w_ref[...], preferred_element_type=F32,
                         precision=lax.Precision.HIGHEST) + b_ref[...]


def _ada_call(cond, w_ada, b_ada):
    n_layers, d, nd = w_ada.shape
    rows = cond.shape[0]
    n_col = nd // d
    return pl.pallas_call(
        _ada_kernel,
        out_shape=jax.ShapeDtypeStruct((n_layers, rows, nd), F32),
        grid=(n_layers, n_col),
        in_specs=[
            pl.BlockSpec((rows, d), lambda l, j: (0, 0)),
            pl.BlockSpec((None, d, d), lambda l, j: (l, 0, j)),
            pl.BlockSpec((None, 1, d), lambda l, j: (l, 0, j)),
        ],
        out_specs=pl.BlockSpec((None, rows, d), lambda l, j: (l, 0, j)),
        compiler_params=_cparams(2),
        name="ada_mod",
    )(cond, w_ada, b_ada.reshape(n_layers, 1, nd))


def _ffn_kernel(x_ref, m_ref, wup_ref, wdn_ref, g_ref, b_ref, o_ref, *, j, alpha, d_ff, ff_chunk):
    x = x_ref[...]
    shift = m_ref[3 * j:3 * j + 1, :]
    scale = m_ref[3 * j + 1:3 * j + 2, :]
    gate = m_ref[3 * j + 2:3 * j + 3, :]
    u = (x * (1.0 + scale) + shift).astype(BF16)
    acc = None
    for c0 in range(0, d_ff, ff_chunk):
        a = _dot(u, wup_ref[:, c0:c0 + ff_chunk])
        v = _dot(u, wup_ref[:, d_ff + c0:d_ff + c0 + ff_chunk])
        hid = (_silu(a) * v).astype(BF16)
        part = _dot(hid, wdn_ref[c0:c0 + ff_chunk, :])
        acc = part if acc is None else acc + part
    z = alpha * x + (FFN_RES_W * gate) * acc
    o_ref[...] = _layer_norm(z, g_ref[...], b_ref[...])


def _ffn_call(h, mods, w_up, w_dn, ln_g, ln_b, *, j, n_ctx_blocks, alpha):
    n_b, tt, d = h.shape
    d_ff = w_dn.shape[0]
    ff_chunk = d_ff // 2 if (d_ff // 2) % LANES == 0 else d_ff
    ctx_row = mods.shape[0] - 1
    n_blocks = tt // TOKEN_BLOCK

    def mod_map(b, t):
        return (jnp.where(t < n_ctx_blocks, ctx_row, b), 0, 0)

    kern = functools.partial(_ffn_kernel, j=j, alpha=alpha, d_ff=d_ff, ff_chunk=ff_chunk)
    return pl.pallas_call(
        kern,
        out_shape=jax.ShapeDtypeStruct(h.shape, F32),
        grid=(n_b, n_blocks),
        in_specs=[
            pl.BlockSpec((None, TOKEN_BLOCK, d), lambda b, t: (b, t, 0)),
            pl.BlockSpec((None, N_MOD, d), mod_map),
            _resident((d, 2 * d_ff), lambda b, t: (0, 0)),
            _resident((d_ff, d), lambda b, t: (0, 0)),
            _resident((1, d), lambda b, t: (0, 0)),
            _resident((1, d), lambda b, t: (0, 0)),
        ],
        out_specs=pl.BlockSpec((None, TOKEN_BLOCK, d), lambda b, t: (b, t, 0)),
        compiler_params=_cparams(2),
        name="ffn_sublayer",
    )(h, mods, w_up, w_dn, ln_g.reshape(1, d), ln_b.reshape(1, d))


def _proj_layout(d, ssd_heads):
    ssd_w = d
    bc_w = SSD_GROUPS * SSD_STATE
    gla_key = d // 2
    names = [("ssd_x", ssd_w), ("ssd_b", bc_w), ("ssd_c", bc_w), ("lru_x", d), ("ssd_z", ssd_w),
             ("lru_g", d), ("gla_q", gla_key), ("gla_k", gla_key), ("gla_v", d), ("gla_g", d),
             ("gates", N_BRANCH * d), ("small", LANES)]
    off, out = 0, {}
    for name, width in names:
        out[name] = (off, width)
        off += width
    return out, off


def _proj_kernel(x_ref, m_ref, w_ref, cw_ref, cb_ref, o_ref, sm_ref, *, conv_cols, silu_cols, main_cols,
                 col_chunk, n_ctx_blocks, line_ctx, line_lat):
    x = x_ref[...]
    u = (x * (1.0 + m_ref[4:5, :]) + m_ref[3:4, :]).astype(BF16)
    t = pl.program_id(1)
    line = jnp.where(t < n_ctx_blocks, line_ctx, line_lat)
    rows = x.shape[0]
    sm_ref[...] = _dot(u, w_ref[:, main_cols:main_cols + LANES])
    for c0 in range(0, main_cols, col_chunk):
        cw = min(col_chunk, main_cols - c0)
        y = _dot(u, w_ref[:, c0:c0 + cw])
        if c0 < conv_cols:
            pos = lax.broadcasted_iota(jnp.int32, (rows, cw), 0) & (line - 1)
            acc = cb_ref[:, c0:c0 + cw] + y * cw_ref[CONV_LEFT:CONV_LEFT + 1, c0:c0 + cw]
            for k in range(CONV_W):
                off = k - CONV_LEFT
                if off == 0:
                    continue
                shifted = pltpu.roll(y, (-off) % rows, axis=0)
                valid = (pos + off >= 0) & (pos + off < line)
                acc = acc + jnp.where(valid, shifted, 0.0) * cw_ref[k:k + 1, c0:c0 + cw]
            y = _silu(acc) if c0 < silu_cols else acc
        o_ref[:, c0:c0 + cw] = y.astype(o_ref.dtype)


def _proj_call(h, mods, w_perm, conv_w, conv_b, *, conv_cols, silu_cols, n_ctx_blocks, line_ctx, line_lat):
    n_b, tt, d = h.shape
    total = w_perm.shape[1]
    ctx_row = mods.shape[0] - 1
    n_blocks = tt // TOKEN_BLOCK

    def mod_map(b, t):
        return (jnp.where(t < n_ctx_blocks, ctx_row, b), 0, 0)

    main = total - LANES
    kern = functools.partial(_proj_kernel, conv_cols=conv_cols, silu_cols=silu_cols, main_cols=main,
                             col_chunk=1024, n_ctx_blocks=n_ctx_blocks, line_ctx=line_ctx, line_lat=line_lat)
    return pl.pallas_call(
        kern,
        out_shape=(jax.ShapeDtypeStruct((n_b, tt, main), BF16), jax.ShapeDtypeStruct((n_b, tt, LANES), F32)),
        grid=(n_b, n_blocks),
        in_specs=[
            pl.BlockSpec((None, TOKEN_BLOCK, d), lambda b, t: (b, t, 0)),
            pl.BlockSpec((None, N_MOD, d), mod_map),
            _resident((d, total), lambda b, t: (0, 0)),
            _resident((CONV_W, conv_cols), lambda b, t: (0, 0)),
            _resident((1, conv_cols), lambda b, t: (0, 0)),
        ],
        out_specs=(pl.BlockSpec((None, TOKEN_BLOCK, main), lambda b, t: (b, t, 0)),
                   pl.BlockSpec((None, TOKEN_BLOCK, LANES), lambda b, t: (b, t, 0))),
        compiler_params=_cparams(2),
        name="mixer_in_proj",
    )(h, mods, w_perm, conv_w, conv_b)


def _ssd_kernel(x_ref, b_ref, c_ref, sm_ref, dtb_ref, alog_ref, exp_ref, y_ref, state_ref, *, n_heads):
    rev = pl.program_id(1)
    i = pl.program_id(2)

    @pl.when(i == 0)
    def _():
        state_ref[...] = jnp.zeros_like(state_ref)

    for bi in range(x_ref.shape[0]):
        _ssd_chunk(rev == 0, x_ref.at[bi], b_ref.at[bi], c_ref.at[bi], sm_ref.at[bi], dtb_ref, alog_ref, exp_ref,
                   y_ref.at[bi], state_ref.at[bi], n_heads)


def _ssd_chunk(fwd, x_ref, b_ref, c_ref, sm_ref, dtb_ref, alog_ref, exp_ref, y_ref, state_ref, n_heads):
    q = x_ref.shape[0]
    hg = n_heads // SSD_GROUPS
    gw = hg * SSD_HEAD_DIM
    sm = sm_ref[...]
    dt_raw = jnp.where(fwd, sm[:, 0:n_heads], sm[:, n_heads:2 * n_heads])
    dt_bias = jnp.where(fwd, dtb_ref[0:1, :], dtb_ref[1:2, :])
    a_log = jnp.where(fwd, alog_ref[0:1, :], alog_ref[1:2, :])
    dt = _softplus(dt_raw + dt_bias)
    a = dt * (-jnp.exp(a_log))
    row = lax.broadcasted_iota(jnp.int32, (q, q), 0)
    col = lax.broadcasted_iota(jnp.int32, (q, q), 1)
    mask = (col - row) * jnp.where(fwd, 1, -1) <= 0
    tri = jnp.where(mask, 1.0, 0.0).astype(BF16)
    cs = _exact_lhs_dot(tri, a)
    tot = jnp.sum(a, axis=0, keepdims=True)
    pad = jnp.zeros((q, LANES - 2 * n_heads), F32)
    both_t = jnp.concatenate([cs, dt, pad], axis=1).T
    cs_t = both_t[0:n_heads]
    dt_t = both_t[n_heads:2 * n_heads]

    w_state = dt * jnp.exp(tot - cs)
    from_start = jnp.exp(cs)
    e_tot = jnp.broadcast_to(jnp.exp(tot), (SUBLANES, n_heads))
    wide = _exact_rhs_dot(jnp.concatenate([w_state, from_start, e_tot], axis=0), exp_ref[...])
    w_state_w = wide[0:q]
    from_start_w = wide[q:2 * q]
    e_tot_w = wide[2 * q:2 * q + 1]

    x_bf = x_ref[...]
    xw_bf = (x_bf.astype(F32) * w_state_w).astype(BF16)
    for g in range(SSD_GROUPS):
        bg = b_ref[:, g * SSD_STATE:(g + 1) * SSD_STATE]
        cg = c_ref[:, g * SSD_STATE:(g + 1) * SSD_STATE]
        scores = lax.dot_general(cg, bg, (((1,), (1,)), ((), ())), preferred_element_type=F32)
        s_prev = state_ref[g]
        y_off = _dot(cg, s_prev.astype(BF16)) * from_start_w[:, g * gw:(g + 1) * gw]
        parts = []
        for hh in range(hg):
            head = g * hg + hh
            seg = cs[:, head:head + 1] - cs_t[head:head + 1, :]
            m = jnp.where(mask, scores * jnp.exp(jnp.where(mask, seg, 0.0)) * dt_t[head:head + 1, :], 0.0)
            parts.append(_dot(m.astype(BF16), x_bf[:, head * SSD_HEAD_DIM:(head + 1) * SSD_HEAD_DIM]))
        y_ref[:, g * gw:(g + 1) * gw] = (y_off + jnp.concatenate(parts, axis=1)).astype(y_ref.dtype)
        upd = lax.dot_general(bg, xw_bf[:, g * gw:(g + 1) * gw], (((0,), (0,)), ((), ())),
                              preferred_element_type=F32)
        state_ref[g] = s_prev * e_tot_w[:, g * gw:(g + 1) * gw] + upd


def _scan_col_spec(layout, name, nb, rows, n_ctx, n_tot):
    off, w = layout[name]
    assert off % w == 0
    blk = off // w
    return pl.BlockSpec((nb, rows, w), lambda b, r, i: (b, _scan_order(i, r, n_ctx, n_tot), blk))


def _scan_small_spec(nb, rows, n_ctx, n_tot):
    return pl.BlockSpec((nb, rows, LANES), lambda b, r, i: (b, _scan_order(i, r, n_ctx, n_tot), 0))


def _scan_out_spec(nb, rows, width, n_ctx, n_tot):
    return pl.BlockSpec((None, nb, rows, width), lambda b, r, i: (r, b, _scan_order(i, r, n_ctx, n_tot), 0))


def _ssd_call(p, small, layout, dt_bias, a_log, expand, *, n_ctx_chunks):
    n_b, tt, _ = p.shape
    n_heads = dt_bias.shape[1]
    width = n_heads * SSD_HEAD_DIM
    n_chunks = tt // SCAN_CHUNK
    hg = n_heads // SSD_GROUPS
    nb = SCAN_BATCH if n_b % SCAN_BATCH == 0 else 1
    col_spec = functools.partial(_scan_col_spec, layout, nb=nb, rows=SCAN_CHUNK, n_ctx=n_ctx_chunks, n_tot=n_chunks)

    kern = functools.partial(_ssd_kernel, n_heads=n_heads)
    return pl.pallas_call(
        kern,
        out_shape=jax.ShapeDtypeStruct((2, n_b, tt, width), BF16),
        grid=(n_b // nb, 2, n_chunks),
        in_specs=[
            col_spec("ssd_x"), col_spec("ssd_b"), col_spec("ssd_c"),
            _scan_small_spec(nb, SCAN_CHUNK, n_ctx_chunks, n_chunks),
            _resident((2, n_heads), lambda b, r, i: (0, 0)),
            _resident((2, n_heads), lambda b, r, i: (0, 0)),
            _resident(expand.shape, lambda b, r, i: (0, 0)),
        ],
        out_specs=_scan_out_spec(nb, SCAN_CHUNK, width, n_ctx_chunks, n_chunks),
        scratch_shapes=[pltpu.VMEM((nb, SSD_GROUPS, SSD_STATE, hg * SSD_HEAD_DIM), F32)],
        compiler_params=_cparams(3),
        name="ssd_scan",
    )(p, p, p, small, dt_bias, a_log, expand)


def _lru_scan_block(a_ref, b_ref, o_ref, carry_ref, reverse):
    rows, width = a_ref.shape
    a = a_ref[...]
    b = b_ref[...]
    sub = lax.broadcasted_iota(jnp.int32, (rows, width), 0) & (SUBLANES - 1)
    k = 1
    while k < SUBLANES:
        if reverse:
            a_sh = pltpu.roll(a, rows - k, axis=0)
            b_sh = pltpu.roll(b, rows - k, axis=0)
            ok = sub < SUBLANES - k
        else:
            a_sh = pltpu.roll(a, k, axis=0)
            b_sh = pltpu.roll(b, k, axis=0)
            ok = sub >= k
        b = b + a * jnp.where(ok, b_sh, 0.0)
        a = a * jnp.where(ok, a_sh, 1.0)
        k *= 2
    a_ref[...] = a
    b_ref[...] = b
    carry = carry_ref[...]
    n_groups = rows // SUBLANES
    order = range(n_groups - 1, -1, -1) if reverse else range(n_groups)
    for gi in order:
        sl = slice(gi * SUBLANES, (gi + 1) * SUBLANES)
        hblk = b_ref[sl, :] + a_ref[sl, :] * carry
        b_ref[sl, :] = hblk
        carry = hblk[0:1, :] if reverse else hblk[SUBLANES - 1:SUBLANES, :]
    carry_ref[...] = carry
    o_ref[...] = b_ref[...].astype(o_ref.dtype)


def _lru_kernel(x_ref, w_ref, ba_ref, bx_ref, lam_ref, h_ref, a_s, b_s, carry_ref):
    rev = pl.program_id(1)
    i = pl.program_id(2)
    nb = x_ref.shape[0]

    @pl.when(i == 0)
    def _():
        carry_ref[...] = jnp.zeros_like(carry_ref)

    n_tiles = w_ref.shape[0]
    neg_c_sp = (-LRU_C) * _softplus(-lam_ref[...])
    for bi in range(nb):
        x_bf = x_ref[bi]
        pre_a, pre_x = [], []
        for t in range(n_tiles):
            y = _dot(x_bf[:, t * MXU_TILE:(t + 1) * MXU_TILE], w_ref[t])
            pre_a.append(y[:, :MXU_TILE])
            pre_x.append(y[:, MXU_TILE:])
        r = _sigmoid(jnp.concatenate(pre_a, axis=1) + ba_ref[...])
        ig = _sigmoid(jnp.concatenate(pre_x, axis=1) + bx_ref[...])
        log_a = r * neg_c_sp
        a = jnp.exp(log_a)
        a_s[bi] = a
        b_s[bi] = jnp.sqrt(-jnp.tanh(log_a) * (a * a + 1.0)) * (ig * x_bf.astype(F32))

    @pl.when(rev == 0)
    def _():
        for bi in range(nb):
            _lru_scan_block(a_s.at[bi], b_s.at[bi], h_ref.at[bi], carry_ref.at[bi], reverse=False)

    @pl.when(rev == 1)
    def _():
        for bi in range(nb):
            _lru_scan_block(a_s.at[bi], b_s.at[bi], h_ref.at[bi], carry_ref.at[bi], reverse=True)


def _lru_call(p, layout, w_tiles, b_a, b_x, lam, *, n_ctx_blocks):
    n_b, tt, _ = p.shape
    width = layout["lru_x"][1]
    n_blocks = tt // TOKEN_BLOCK
    n_tiles = w_tiles.shape[1]
    nb = SCAN_BATCH if n_b % SCAN_BATCH == 0 else 1

    return pl.pallas_call(
        _lru_kernel,
        out_shape=jax.ShapeDtypeStruct((2, n_b, tt, width), BF16),
        grid=(n_b // nb, 2, n_blocks),
        in_specs=[
            _scan_col_spec(layout, "lru_x", nb, TOKEN_BLOCK, n_ctx_blocks, n_blocks),
            pl.BlockSpec((None, n_tiles, MXU_TILE, 2 * MXU_TILE), lambda b, r, i: (r, 0, 0, 0)),
            pl.BlockSpec((None, 1, width), lambda b, r, i: (r, 0, 0)),
            pl.BlockSpec((None, 1, width), lambda b, r, i: (r, 0, 0)),
            pl.BlockSpec((None, 1, width), lambda b, r, i: (r, 0, 0)),
        ],
        out_specs=_scan_out_spec(nb, TOKEN_BLOCK, width, n_ctx_blocks, n_blocks),
        scratch_shapes=[pltpu.VMEM((nb, TOKEN_BLOCK, width), F32), pltpu.VMEM((nb, TOKEN_BLOCK, width), F32),
                        pltpu.VMEM((nb, 1, width), F32)],
        compiler_params=_cparams(3),
        name="rglru_scan",
    )(p, w_tiles, b_a, b_x, lam)


def _gla_kernel(q_ref, k_ref, v_ref, sm_ref, wg_ref, bg_ref, o_ref, state_ref, *, alr_off, dk, dv):
    rev = pl.program_id(1)
    i = pl.program_id(2)

    @pl.when(i == 0)
    def _():
        state_ref[...] = jnp.zeros_like(state_ref)

    for bi in range(q_ref.shape[0]):
        _gla_chunk(rev == 0, q_ref.at[bi], k_ref.at[bi], v_ref.at[bi], sm_ref.at[bi], wg_ref, bg_ref,
                   o_ref.at[bi], state_ref.at[bi], alr_off, dk, dv)


def _gla_chunk(fwd, q_ref, k_ref, v_ref, sm_ref, wg_ref, bg_ref, o_ref, state_ref, alr_off, dk, dv):
    q_len = q_ref.shape[0]
    n_heads = q_ref.shape[1] // dk
    sm = sm_ref[...]
    r = GLA_GATE_RANK
    a_lr = jnp.where(fwd, sm[:, alr_off:alr_off + r], sm[:, alr_off + r:alr_off + 2 * r])
    z = jnp.dot(a_lr, wg_ref[...], preferred_element_type=F32, precision=lax.Precision.HIGHEST) + bg_ref[...]
    log_a = _log_sigmoid(z) * (1.0 / GLA_TAU)
    row = lax.broadcasted_iota(jnp.int32, (q_len, q_len), 0)
    col = lax.broadcasted_iota(jnp.int32, (q_len, q_len), 1)
    mask = (col - row) * jnp.where(fwd, 1, -1) <= 0
    tri = jnp.where(mask, 1.0, 0.0).astype(BF16)
    bcum = _exact_lhs_dot(tri, log_a)
    btot = jnp.sum(log_a, axis=0, keepdims=True)
    bmid = bcum[q_len // 2:q_len // 2 + 1, :]
    q_in = q_ref[...].astype(F32) * (dk ** -0.5) * jnp.exp(bcum - bmid)
    k_in = k_ref[...].astype(F32) * jnp.exp(bmid - bcum)
    q_off = (q_in * jnp.exp(bmid)).astype(BF16)
    k_st = (k_in * jnp.exp(btot - bmid)).astype(BF16)
    q_in = q_in.astype(BF16)
    k_in = k_in.astype(BF16)
    e_tot = jnp.exp(btot)
    for hh in range(n_heads):
        ks = slice(hh * dk, (hh + 1) * dk)
        vs = slice(hh * dv, (hh + 1) * dv)
        v_bf = v_ref[:, vs]
        att = lax.dot_general(q_in[:, ks], k_in[:, ks], (((1,), (1,)), ((), ())), preferred_element_type=F32)
        att = jnp.where(mask, att, 0.0).astype(BF16)
        s_prev = state_ref[hh]
        inter = lax.dot_general(q_off[:, ks], s_prev.astype(BF16), (((1,), (1,)), ((), ())),
                                preferred_element_type=F32)
        o_ref[:, vs] = (_dot(att, v_bf) + inter).astype(o_ref.dtype)
        upd = lax.dot_general(v_bf, k_st[:, ks], (((0,), (0,)), ((), ())), preferred_element_type=F32)
        state_ref[hh] = s_prev * e_tot[:, ks] + upd


def _gla_call(p, small, layout, w_gate, b_gate, *, alr_off, n_ctx_chunks):
    n_b, tt, _ = p.shape
    key_w = layout["gla_q"][1]
    val_w = layout["gla_v"][1]
    dk = key_w // GLA_HEADS
    dv = val_w // GLA_HEADS
    n_chunks = tt // SCAN_CHUNK
    nb = SCAN_BATCH if n_b % SCAN_BATCH == 0 else 1
    col_spec = functools.partial(_scan_col_spec, layout, nb=nb, rows=SCAN_CHUNK, n_ctx=n_ctx_chunks, n_tot=n_chunks)

    kern = functools.partial(_gla_kernel, alr_off=alr_off, dk=dk, dv=dv)
    return pl.pallas_call(
        kern,
        out_shape=jax.ShapeDtypeStruct((2, n_b, tt, val_w), BF16),
        grid=(n_b // nb, 2, n_chunks),
        in_specs=[
            col_spec("gla_q"), col_spec("gla_k"), col_spec("gla_v"),
            _scan_small_spec(nb, SCAN_CHUNK, n_ctx_chunks, n_chunks),
            pl.BlockSpec((None, GLA_GATE_RANK, key_w), lambda b, r, i: (r, 0, 0)),
            pl.BlockSpec((None, 1, key_w), lambda b, r, i: (r, 0, 0)),
        ],
        out_specs=_scan_out_spec(nb, SCAN_CHUNK, val_w, n_ctx_chunks, n_chunks),
        scratch_shapes=[pltpu.VMEM((nb, GLA_HEADS, dv, dk), F32)],
        compiler_params=_cparams(3),
        name="gla_scan",
    )(p, p, p, small, w_gate, b_gate)


def _merge_kernel(h_ref, m_ref, ys0_ref, ys1_ref, xs_ref, z_ref, hl0_ref, hl1_ref, gl_ref,
                  og0_ref, og1_ref, gg_ref, gt0_ref, gt1_ref, gt2_ref, dskip_ref, sng_ref, gng_ref,
                  wbr_ref, wout_ref, lng_ref, lnb_ref, o_ref, *, alpha, dv):
    d = h_ref.shape[1]

    def f32(ref, cols=slice(None)):
        return ref[:, cols].astype(F32)

    y = (f32(ys0_ref) + f32(ys1_ref) + dskip_ref[...] * f32(xs_ref)) * _silu(f32(z_ref))
    y = y * lax.rsqrt(jnp.mean(y * y, axis=-1, keepdims=True) + NORM_EPS) * sng_ref[...]
    m = _sigmoid(f32(gt0_ref)) * _dot(y.astype(BF16), wbr_ref[0])
    y = (f32(hl0_ref) + f32(hl1_ref)) * jax.nn.gelu(f32(gl_ref))
    m = m + _sigmoid(f32(gt1_ref)) * _dot(y.astype(BF16), wbr_ref[1])
    parts = []
    for hh in range(d // dv):
        vs = slice(hh * dv, (hh + 1) * dv)
        o = f32(og0_ref, vs) + f32(og1_ref, vs)
        o = o * lax.rsqrt(jnp.mean(o * o, axis=-1, keepdims=True) + NORM_EPS) * gng_ref[...]
        parts.append(o * _silu(f32(gg_ref, vs)))
    y = jnp.concatenate(parts, axis=1)
    m = m + _sigmoid(f32(gt2_ref)) * _dot(y.astype(BF16), wbr_ref[2])
    out = _dot(m.astype(BF16), wout_ref[...])
    z = alpha * h_ref[...] + m_ref[5:6, :] * out
    o_ref[...] = _layer_norm(z, lng_ref[...], lnb_ref[...])


def _merge_call(h, mods, p, layout, y_ssd, h_lru, o_gla, d_skip_w, ssd_norm_g, gla_norm_g, w_branch, w_out,
                ln_g, ln_b, *, n_ctx_blocks, skip_blocks, alpha):
    n_b, tt, d = h.shape
    n_blocks = tt // TOKEN_BLOCK - skip_blocks
    ctx_row = mods.shape[0] - 1
    dv = gla_norm_g.shape[-1]

    def mod_map(b, t):
        return (jnp.where(t + skip_blocks < n_ctx_blocks, ctx_row, b), 0, 0)

    def tok(width, blk):
        return pl.BlockSpec((None, TOKEN_BLOCK, width), lambda b, t: (b, t + skip_blocks, blk))

    def col_spec(name):
        off, w = layout[name]
        blk = off // w
        assert blk * w == off
        return tok(w, blk)

    def gate_spec(n):
        off = layout["gates"][0] + n * d
        assert off % d == 0
        return tok(d, off // d)

    def dir_spec(rv):
        return pl.BlockSpec((None, None, TOKEN_BLOCK, d), lambda b, t: (rv, b, t + skip_blocks, 0))

    kern = functools.partial(_merge_kernel, alpha=alpha, dv=dv)
    return pl.pallas_call(
        kern,
        out_shape=jax.ShapeDtypeStruct((n_b, n_blocks * TOKEN_BLOCK, d), F32),
        grid=(n_b, n_blocks),
        in_specs=[
            tok(d, 0),
            pl.BlockSpec((None, N_MOD, d), mod_map),
            dir_spec(0), dir_spec(1), col_spec("ssd_x"), col_spec("ssd_z"),
            dir_spec(0), dir_spec(1), col_spec("lru_g"),
            dir_spec(0), dir_spec(1), col_spec("gla_g"),
            gate_spec(0), gate_spec(1), gate_spec(2),
            _resident((1, d), lambda b, t: (0, 0)),
            _resident((1, d), lambda b, t: (0, 0)),
            _resident((1, dv), lambda b, t: (0, 0)),
            _resident((N_BRANCH, d, d), lambda b, t: (0, 0, 0)),
            _resident((d, d), lambda b, t: (0, 0)),
            _resident((1, d), lambda b, t: (0, 0)),
            _resident((1, d), lambda b, t: (0, 0)),
        ],
        out_specs=pl.BlockSpec((None, TOKEN_BLOCK, d), lambda b, t: (b, t, 0)),
        compiler_params=_cparams(2),
        name="mixer_merge",
    )(h, mods, y_ssd, y_ssd, p, p, h_lru, h_lru, p, o_gla, o_gla, p, p, p, p,
      d_skip_w, ssd_norm_g.reshape(1, d), gla_norm_g.reshape(1, dv), w_branch, w_out,
      ln_g.reshape(1, d), ln_b.reshape(1, d))


def _permute_w_in(w_in_l, d, ssd_heads, layout, total):
    ssd_w = d
    bc_w = SSD_GROUPS * SSD_STATE
    gla_key = d // 2
    sizes = (ssd_w, ssd_w + 2 * bc_w, 2 * ssd_heads, d, d, gla_key, gla_key, d, d, 2 * GLA_GATE_RANK, N_BRANCH * d)
    offs = [0]
    for s in sizes:
        offs.append(offs[-1] + s)
    z, xbc, dtr, lx, lg, gq, gk, gv, gg, alr, gates = [w_in_l[:, offs[n]:offs[n + 1]] for n in range(len(sizes))]
    small = jnp.concatenate([dtr, alr, jnp.zeros((d, LANES - dtr.shape[1] - alr.shape[1]), w_in_l.dtype)], axis=1)
    w = jnp.concatenate([xbc, lx, z, lg, gq, gk, gv, gg, gates, small], axis=1)
    assert w.shape[1] == total
    return w.astype(BF16)


def _lru_gate_tiles(w_a, w_x):
    per = MXU_TILE // LRU_BLOCK
    n_dir, nb, k, _ = w_a.shape
    n_tiles = nb // per

    def tiles(w):
        w = w.reshape(n_dir, n_tiles, per, k, k)
        eye = jnp.eye(per, dtype=w.dtype)
        t = jnp.einsum("dtpij,pq->dtpiqj", w, eye)
        return t.reshape(n_dir, n_tiles, MXU_TILE, MXU_TILE)

    return jnp.concatenate([tiles(w_a), tiles(w_x)], axis=-1).astype(BF16)


def _to_colmajor(h, n_ctx, rows):
    n_b, _, d = h.shape
    lat = h[:, n_ctx:].reshape(n_b, rows, GRID_W, d).transpose(0, 2, 1, 3).reshape(n_b, rows * GRID_W, d)
    return jnp.concatenate([h[:, :n_ctx], lat], axis=1)


def _to_raster(h, n_ctx, rows):
    n_b, _, d = h.shape
    lat = h[:, n_ctx:].reshape(n_b, GRID_W, rows, d).transpose(0, 2, 1, 3).reshape(n_b, rows * GRID_W, d)
    return jnp.concatenate([h[:, :n_ctx], lat], axis=1)


def kernel(x, c, ctx, c_ctx, w_ada, b_ada, ln_g, ln_b, ffn_w_up, ffn_w_down, w_in, ssd_conv_w, ssd_conv_b,
           ssd_dt_bias, ssd_a_log, ssd_d, ssd_norm_g, lru_conv_w, lru_conv_b, lru_w_a, lru_b_a, lru_w_x, lru_b_x,
           lru_lam, gla_w_gate, gla_b_gate, gla_norm_g, w_branch, w_out):
    n_b, t_lat, d = x.shape
    n_ctx = ctx.shape[1]
    depth = w_ada.shape[0]
    rows = t_lat // GRID_W
    ssd_heads = ssd_dt_bias.shape[-1]
    assert n_ctx % TOKEN_BLOCK == 0 and t_lat % TOKEN_BLOCK == 0
    for line in (n_ctx, GRID_W, rows):
        assert line & (line - 1) == 0 and TOKEN_BLOCK % min(line, TOKEN_BLOCK) == 0
    assert n_ctx <= TOKEN_BLOCK
    alpha = (2.0 * depth) ** 0.25
    n_ctx_blocks = n_ctx // TOKEN_BLOCK
    n_ctx_chunks = n_ctx // SCAN_CHUNK

    layout, total = _proj_layout(d, ssd_heads)
    conv_cols = layout["lru_x"][0] + layout["lru_x"][1]
    silu_cols = layout["lru_x"][0]
    alr_off = 2 * ssd_heads

    n_rows = -(-(n_b + 1) // SUBLANES) * SUBLANES
    cond = jnp.concatenate([c, jnp.zeros((n_rows - n_b - 1, d), F32), c_ctx[None, :]], axis=0)
    mods_all = _ada_call(cond, w_ada, b_ada).reshape(depth, n_rows, N_MOD, d)

    expand = jnp.repeat(jnp.eye(ssd_heads, dtype=BF16), SSD_HEAD_DIM, axis=1)
    expand = jnp.concatenate([expand] * 3, axis=0)

    h = jnp.concatenate([ctx, x], axis=1)
    for l in range(depth):
        last = l == depth - 1
        col_major = l % 2 == 1
        mods = mods_all[l]
        wup = ffn_w_up[l].astype(BF16)
        wdn = ffn_w_down[l].astype(BF16)

        if col_major:
            h = _to_colmajor(h, n_ctx, rows)
        h = _ffn_call(h, mods, wup[0], wdn[0], ln_g[l, 0], ln_b[l, 0], j=0, n_ctx_blocks=n_ctx_blocks, alpha=alpha)

        w_perm = _permute_w_in(w_in[l], d, ssd_heads, layout, total)
        conv_w = jnp.concatenate([ssd_conv_w[l], lru_conv_w[l]], axis=1)
        conv_b = jnp.concatenate([ssd_conv_b[l], lru_conv_b[l]], axis=0)[None, :]
        p, small = _proj_call(h, mods, w_perm, conv_w, conv_b, conv_cols=conv_cols, silu_cols=silu_cols,
                              n_ctx_blocks=n_ctx_blocks, line_ctx=n_ctx, line_lat=rows if col_major else GRID_W)

        y_ssd = _ssd_call(p, small, layout, ssd_dt_bias[l], ssd_a_log[l], expand, n_ctx_chunks=n_ctx_chunks)
        h_lru = _lru_call(p, layout, _lru_gate_tiles(lru_w_a[l], lru_w_x[l]), lru_b_a[l][:, None, :],
                          lru_b_x[l][:, None, :], lru_lam[l][:, None, :], n_ctx_blocks=n_ctx_blocks)
        o_gla = _gla_call(p, small, layout, gla_w_gate[l], gla_b_gate[l][:, None, :], alr_off=alr_off,
                          n_ctx_chunks=n_ctx_chunks)

        d_skip_w = jnp.repeat(ssd_d[l, 0] + ssd_d[l, 1], SSD_HEAD_DIM)[None, :]
        skip_blocks = n_ctx_blocks if last else 0
        h = _merge_call(h, mods, p, layout, y_ssd, h_lru, o_gla, d_skip_w, ssd_norm_g[l], gla_norm_g[l],
                        w_branch[l].astype(BF16), w_out[l].astype(BF16), ln_g[l, 1], ln_b[l, 1],
                        n_ctx_blocks=n_ctx_blocks, skip_blocks=skip_blocks, alpha=alpha)
        n_ctx_now = 0 if last else n_ctx
        h = _ffn_call(h, mods, wup[1], wdn[1], ln_g[l, 2], ln_b[l, 2], j=2,
                      n_ctx_blocks=n_ctx_now // TOKEN_BLOCK, alpha=alpha)
        if col_major:
            h = _to_raster(h, n_ctx_now, rows)
    return h
```

```python
import functools
import math

import jax
import jax.numpy as jnp
from jax import lax
from jax.experimental import pallas as pl
from jax.experimental.pallas import tpu as pltpu

F32 = jnp.float32
BF16 = jnp.bfloat16

GRID_W = 64
CONV_W = 4
CONV_LEFT = 2
SSD_HEAD_DIM = 64
SSD_GROUPS = 4
SSD_STATE = 128
LRU_BLOCK = 64
LRU_C = 8.0
GLA_HEADS = 4
GLA_GATE_RANK = 16
GLA_TAU = 16.0
N_MOD = 9
N_BRANCH = 3
FFN_RES_W = 0.5
NORM_EPS = 1e-5

TOKEN_BLOCK = 256
SCAN_CHUNK = 128
SCAN_BATCH = 2
MXU_TILE = 256
LANES = 128
SUBLANES = 8
VMEM_LIMIT = 56 * 1024 * 1024


def _cparams(n_axes):
    return pltpu.CompilerParams(dimension_semantics=("arbitrary",) * n_axes,
                                vmem_limit_bytes=VMEM_LIMIT)


def _resident(block_shape, index_map):
    return pl.BlockSpec(block_shape, index_map, pipeline_mode=pl.Buffered(1))


def _sigmoid(x):
    return 0.5 * jnp.tanh(0.5 * x) + 0.5


def _silu(x):
    return x * _sigmoid(x)


def _softplus(x):
    return jnp.maximum(x, 0.0) + jnp.log1p(jnp.exp(-jnp.abs(x)))


def _log_sigmoid(x):
    return -_softplus(-x)


def _split_terms(x, n):
    terms = []
    for _ in range(n - 1):
        t = x.astype(BF16)
        terms.append(t)
        x = x - t.astype(F32)
    terms.append(x.astype(BF16))
    return terms


_dot = functools.partial(jnp.dot, preferred_element_type=F32)


def _exact_lhs_dot(t_bf16, x, n_terms):
    return _dot(jnp.concatenate([t_bf16] * n_terms, axis=1), jnp.concatenate(_split_terms(x, n_terms), axis=0))


def _exact_rhs_dot(x, e3_bf16):
    return _dot(jnp.concatenate(_split_terms(x, 3), axis=1), e3_bf16)


def _f32_dot_small_k(a, w):
    at = _split_terms(a, 3)
    wt = _split_terms(w, 3)
    pairs = [(0, 0), (0, 1), (0, 2), (1, 0), (1, 1), (2, 0)]
    lhs = jnp.concatenate([at[i] for i, _ in pairs], axis=1)
    rhs = jnp.concatenate([wt[j] for _, j in pairs], axis=0)
    return _dot(lhs, rhs)


def _layer_norm(z, g, b):
    mu = jnp.mean(z, axis=-1, keepdims=True)
    zc = z - mu
    var = jnp.mean(zc * zc, axis=-1, keepdims=True)
    return zc * lax.rsqrt(var + NORM_EPS) * g + b


def _scan_order(i, rev, n_ctx, n_tot):
    back = jnp.where(i < n_ctx, n_ctx - 1 - i, n_ctx + n_tot - 1 - i)
    return jnp.where(rev == 0, i, back)


def _ada_kernel(s_ref, w_ref, b_ref, o_ref):
    s = _silu(s_ref[...])
    o_ref[...] = jnp.dot(s, w_ref[...], preferred_element_type=F32,
                         precision=lax.Precision.HIGHEST) + b_ref[...]


def _ada_call(cond, w_ada, b_ada):
    n_layers, d, nd = w_ada.shape
    rows = cond.shape[0]
    n_col = nd // d
    return pl.pallas_call(
        _ada_kernel,
        out_shape=jax.ShapeDtypeStruct((n_layers, rows, nd), F32),
        grid=(n_layers, n_col),
        in_specs=[
            pl.BlockSpec((rows, d), lambda l, j: (0, 0)),
            pl.BlockSpec((None, d, d), lambda l, j: (l, 0, j)),
            pl.BlockSpec((None, 1, d), lambda l, j: (l, 0, j)),
        ],
        out_specs=pl.BlockSpec((None, rows, d), lambda l, j: (l, 0, j)),
        compiler_params=_cparams(2),
        name="ada_mod",
    )(cond, w_ada, b_ada.reshape(n_layers, 1, nd))


def _ffn_kernel(x_ref, m_ref, wup_ref, wdn_ref, g_ref, b_ref, o_ref, *, j, alpha, d_ff, ff_chunk):
    x = x_ref[...]
    shift = m_ref[3 * j:3 * j + 1, :]
    scale = m_ref[3 * j + 1:3 * j + 2, :]
    gate = m_ref[3 * j + 2:3 * j + 3, :]
    u = (x * (1.0 + scale) + shift).astype(BF16)
    acc = None
    for c0 in range(0, d_ff, ff_chunk):
        a = _dot(u, wup_ref[:, c0:c0 + ff_chunk])
        v = _dot(u, wup_ref[:, d_ff + c0:d_ff + c0 + ff_chunk])
        hid = (_silu(a) * v).astype(BF16)
        part = _dot(hid, wdn_ref[c0:c0 + ff_chunk, :])
        acc = part if acc is None else acc + part
    z = alpha * x + (FFN_RES_W * gate) * acc
    o_ref[...] = _layer_norm(z, g_ref[...], b_ref[...])


def _ffn_call(h, mods, w_up, w_dn, ln_g, ln_b, *, j, n_ctx_blocks, alpha):
    n_b, tt, d = h.shape
    d_ff = w_dn.shape[0]
    ff_chunk = d_ff // 2 if (d_ff // 2) % LANES == 0 else d_ff
    ctx_row = mods.shape[0] - 1
    n_blocks = tt // TOKEN_BLOCK

    def mod_map(b, t):
        return (jnp.where(t < n_ctx_blocks, ctx_row, b), 0, 0)

    kern = functools.partial(_ffn_kernel, j=j, alpha=alpha, d_ff=d_ff, ff_chunk=ff_chunk)
    return pl.pallas_call(
        kern,
        out_shape=jax.ShapeDtypeStruct(h.shape, F32),
        grid=(n_b, n_blocks),
        in_specs=[
            pl.BlockSpec((None, TOKEN_BLOCK, d), lambda b, t: (b, t, 0)),
            pl.BlockSpec((None, N_MOD, d), mod_map),
            _resident((d, 2 * d_ff), lambda b, t: (0, 0)),
            _resident((d_ff, d), lambda b, t: (0, 0)),
            _resident((1, d), lambda b, t: (0, 0)),
            _resident((1, d), lambda b, t: (0, 0)),
        ],
        out_specs=pl.BlockSpec((None, TOKEN_BLOCK, d), lambda b, t: (b, t, 0)),
        compiler_params=_cparams(2),
        name="ffn_sublayer",
    )(h, mods, w_up, w_dn, ln_g.reshape(1, d), ln_b.reshape(1, d))


def _proj_layout(d, ssd_heads):
    ssd_w = d
    bc_w = SSD_GROUPS * SSD_STATE
    gla_key = d // 2
    names = [("ssd_x", ssd_w), ("ssd_b", bc_w), ("ssd_c", bc_w), ("lru_x", d), ("ssd_z", ssd_w),
             ("lru_g", d), ("gla_q", gla_key), ("gla_k", gla_key), ("gla_v", d), ("gla_g", d),
             ("gates", N_BRANCH * d), ("small", LANES)]
    off, out = 0, {}
    for name, width in names:
        out[name] = (off, width)
        off += width
    return out, off


def _proj_kernel(x_ref, m_ref, w_ref, cw_ref, cb_ref, o_ref, sm_ref, *, conv_cols, silu_cols, main_cols,
                 col_chunk, conv_chunk, n_ctx_blocks, line_ctx, line_lat):
    x = x_ref[...]
    u = (x * (1.0 + m_ref[4:5, :]) + m_ref[3:4, :]).astype(BF16)
    t = pl.program_id(1)
    line = jnp.where(t < n_ctx_blocks, line_ctx, line_lat)
    rows = x.shape[0]
    sm_ref[...] = _dot(u, w_ref[:, main_cols:main_cols + LANES])
    starts = list(range(0, conv_cols, conv_chunk)) + list(range(conv_cols, main_cols, col_chunk))
    for c0 in starts:
        cw = conv_chunk if c0 < conv_cols else min(col_chunk, main_cols - c0)
        y = _dot(u, w_ref[:, c0:c0 + cw])
        if c0 < conv_cols:
            pos = lax.broadcasted_iota(jnp.int32, (rows, cw), 0) & (line - 1)
            acc = cb_ref[:, c0:c0 + cw] + y * cw_ref[CONV_LEFT:CONV_LEFT + 1, c0:c0 + cw]
            for k in range(CONV_W):
                off = k - CONV_LEFT
                if off == 0:
                    continue
                shifted = pltpu.roll(y, (-off) % rows, axis=0)
                valid = (pos + off >= 0) & (pos + off < line)
                acc = acc + jnp.where(valid, shifted, 0.0) * cw_ref[k:k + 1, c0:c0 + cw]
            y = _silu(acc) if c0 < silu_cols else acc
        o_ref[:, c0:c0 + cw] = y.astype(o_ref.dtype)


def _proj_call(h, mods, w_perm, conv_w, conv_b, *, conv_cols, silu_cols, n_ctx_blocks, line_ctx, line_lat):
    n_b, tt, d = h.shape
    total = w_perm.shape[1]
    ctx_row = mods.shape[0] - 1
    n_blocks = tt // TOKEN_BLOCK

    def mod_map(b, t):
        return (jnp.where(t < n_ctx_blocks, ctx_row, b), 0, 0)

    main = total - LANES
    kern = functools.partial(_proj_kernel, conv_cols=conv_cols, silu_cols=silu_cols, main_cols=main,
                             col_chunk=1024, conv_chunk=256, n_ctx_blocks=n_ctx_blocks, line_ctx=line_ctx,
                             line_lat=line_lat)
    return pl.pallas_call(
        kern,
        out_shape=(jax.ShapeDtypeStruct((n_b, tt, main), BF16), jax.ShapeDtypeStruct((n_b, tt, LANES), F32)),
        grid=(n_b, n_blocks),
        in_specs=[
            pl.BlockSpec((None, TOKEN_BLOCK, d), lambda b, t: (b, t, 0)),
            pl.BlockSpec((None, N_MOD, d), mod_map),
            _resident((d, total), lambda b, t: (0, 0)),
            _resident((CONV_W, conv_cols), lambda b, t: (0, 0)),
            _resident((1, conv_cols), lambda b, t: (0, 0)),
        ],
        out_specs=(pl.BlockSpec((None, TOKEN_BLOCK, main), lambda b, t: (b, t, 0)),
                   pl.BlockSpec((None, TOKEN_BLOCK, LANES), lambda b, t: (b, t, 0))),
        compiler_params=_cparams(2),
        name="mixer_in_proj",
    )(h, mods, w_perm, conv_w, conv_b)


def _ssd_kernel(x_ref, b_ref, c_ref, sm_ref, dtb_ref, alog_ref, exp_ref, y_ref, state_ref, *, n_heads):
    rev = pl.program_id(1)
    i = pl.program_id(2)

    @pl.when(i == 0)
    def _():
        state_ref[...] = jnp.zeros_like(state_ref)

    for bi in range(x_ref.shape[0]):
        _ssd_chunk(rev == 0, x_ref.at[bi], b_ref.at[bi], c_ref.at[bi], sm_ref.at[bi], dtb_ref, alog_ref, exp_ref,
                   y_ref.at[bi], state_ref.at[bi], n_heads)


def _ssd_chunk(fwd, x_ref, b_ref, c_ref, sm_ref, dtb_ref, alog_ref, exp_ref, y_ref, state_ref, n_heads):
    q = x_ref.shape[0]
    hg = n_heads // SSD_GROUPS
    gw = hg * SSD_HEAD_DIM
    sm = sm_ref[...]
    dt_raw = jnp.where(fwd, sm[:, 0:n_heads], sm[:, n_heads:2 * n_heads])
    dt_bias = jnp.where(fwd, dtb_ref[0:1, :], dtb_ref[1:2, :])
    a_log = jnp.where(fwd, alog_ref[0:1, :], alog_ref[1:2, :])
    dt = _softplus(dt_raw + dt_bias)
    a = dt * (-jnp.exp(a_log))
    row = lax.broadcasted_iota(jnp.int32, (q, q), 0)
    col = lax.broadcasted_iota(jnp.int32, (q, q), 1)
    mask = (col - row) * jnp.where(fwd, 1, -1) <= 0
    tri = jnp.where(mask, 1.0, 0.0).astype(BF16)
    cs = _exact_lhs_dot(tri, a, 3)
    tot = jnp.sum(a, axis=0, keepdims=True)
    pad = jnp.zeros((q, LANES - 2 * n_heads), F32)
    both_t = jnp.concatenate([cs, dt, pad], axis=1).T
    cs_t = both_t[0:n_heads]
    dt_t = both_t[n_heads:2 * n_heads]

    w_state = dt * jnp.exp(tot - cs)
    from_start = jnp.exp(cs)
    e_tot = jnp.broadcast_to(jnp.exp(tot), (SUBLANES, n_heads))
    wide = _exact_rhs_dot(jnp.concatenate([w_state, from_start, e_tot], axis=0), exp_ref[...])
    w_state_w = wide[0:q]
    from_start_w = wide[q:2 * q]
    e_tot_w = wide[2 * q:2 * q + 1]

    x_bf = x_ref[...]
    xw_bf = (x_bf.astype(F32) * w_state_w).astype(BF16)
    for g in range(SSD_GROUPS):
        bt = b_ref[:, g * SSD_STATE:(g + 1) * SSD_STATE].T
        cg = c_ref[:, g * SSD_STATE:(g + 1) * SSD_STATE]
        scores = _dot(cg, bt)
        s_prev = state_ref[g]
        y_off = _dot(cg, s_prev.astype(BF16)) * from_start_w[:, g * gw:(g + 1) * gw]
        parts = []
        for hh in range(hg):
            head = g * hg + hh
            seg = jnp.where(mask, cs[:, head:head + 1] - cs_t[head:head + 1, :], -1e30)
            m = scores * jnp.exp(seg) * dt_t[head:head + 1, :]
            parts.append(_dot(m.astype(BF16), x_bf[:, head * SSD_HEAD_DIM:(head + 1) * SSD_HEAD_DIM]))
        y_ref[:, g * gw:(g + 1) * gw] = (y_off + jnp.concatenate(parts, axis=1)).astype(y_ref.dtype)
        upd = _dot(bt, xw_bf[:, g * gw:(g + 1) * gw])
        state_ref[g] = s_prev * e_tot_w[:, g * gw:(g + 1) * gw] + upd


def _scan_col_spec(layout, name, nb, rows, n_ctx, n_tot):
    off, w = layout[name]
    assert off % w == 0
    blk = off // w
    return pl.BlockSpec((nb, rows, w), lambda b, r, i: (b, _scan_order(i, r, n_ctx, n_tot), blk))


def _scan_small_spec(nb, rows, n_ctx, n_tot):
    return pl.BlockSpec((nb, rows, LANES), lambda b, r, i: (b, _scan_order(i, r, n_ctx, n_tot), 0))


def _scan_out_spec(nb, rows, width, n_ctx, n_tot):
    return pl.BlockSpec((None, nb, rows, width), lambda b, r, i: (r, b, _scan_order(i, r, n_ctx, n_tot), 0))


def _ssd_call(p, small, layout, dt_bias, a_log, expand, *, n_ctx_chunks):
    n_b, tt, _ = p.shape
    n_heads = dt_bias.shape[1]
    width = n_heads * SSD_HEAD_DIM
    n_chunks = tt // SCAN_CHUNK
    hg = n_heads // SSD_GROUPS
    nb = SCAN_BATCH if n_b % SCAN_BATCH == 0 else 1
    col_spec = functools.partial(_scan_col_spec, layout, nb=nb, rows=SCAN_CHUNK, n_ctx=n_ctx_chunks, n_tot=n_chunks)

    kern = functools.partial(_ssd_kernel, n_heads=n_heads)
    return pl.pallas_call(
        kern,
        out_shape=jax.ShapeDtypeStruct((2, n_b, tt, width), BF16),
        grid=(n_b // nb, 2, n_chunks),
        in_specs=[
            col_spec("ssd_x"), col_spec("ssd_b"), col_spec("ssd_c"),
            _scan_small_spec(nb, SCAN_CHUNK, n_ctx_chunks, n_chunks),
            _resident((2, n_heads), lambda b, r, i: (0, 0)),
            _resident((2, n_heads), lambda b, r, i: (0, 0)),
            _resident(expand.shape, lambda b, r, i: (0, 0)),
        ],
        out_specs=_scan_out_spec(nb, SCAN_CHUNK, width, n_ctx_chunks, n_chunks),
        scratch_shapes=[pltpu.VMEM((nb, SSD_GROUPS, SSD_STATE, hg * SSD_HEAD_DIM), F32)],
        compiler_params=_cparams(3),
        name="ssd_scan",
    )(p, p, p, small, dt_bias, a_log, expand)


def _lru_scan_block(a_ref, b_ref, o_ref, carry_ref, reverse):
    rows, width = a_ref.shape
    sub = lax.broadcasted_iota(jnp.int32, (SUBLANES, width), 0)
    carry = carry_ref[...]
    n_groups = rows // SUBLANES
    order = range(n_groups - 1, -1, -1) if reverse else range(n_groups)
    for gi in order:
        sl = slice(gi * SUBLANES, (gi + 1) * SUBLANES)
        a = a_ref[sl, :]
        b = b_ref[sl, :]
        k = 1
        while k < SUBLANES:
            if reverse:
                a_sh = pltpu.roll(a, SUBLANES - k, axis=0)
                b_sh = pltpu.roll(b, SUBLANES - k, axis=0)
                ok = sub < SUBLANES - k
            else:
                a_sh = pltpu.roll(a, k, axis=0)
                b_sh = pltpu.roll(b, k, axis=0)
                ok = sub >= k
            b = b + a * jnp.where(ok, b_sh, 0.0)
            a = a * jnp.where(ok, a_sh, 1.0)
            k *= 2
        hblk = b + a * carry
        b_ref[sl, :] = hblk
        carry = hblk[0:1, :] if reverse else hblk[SUBLANES - 1:SUBLANES, :]
    carry_ref[...] = carry
    o_ref[...] = b_ref[...].astype(o_ref.dtype)


def _lru_kernel(x_ref, w_ref, ba_ref, bx_ref, lam_ref, h_ref, a_s, b_s, carry_ref):
    rev = pl.program_id(1)
    i = pl.program_id(2)
    nb = x_ref.shape[0]

    @pl.when(i == 0)
    def _():
        carry_ref[...] = jnp.zeros_like(carry_ref)

    n_tiles = w_ref.shape[0]
    neg_c_sp = (-LRU_C) * _softplus(-lam_ref[...])
    for bi in range(nb):
        x_bf = x_ref[bi]
        pre_a, pre_x = [], []
        for t in range(n_tiles):
            y = _dot(x_bf[:, t * MXU_TILE:(t + 1) * MXU_TILE], w_ref[t])
            pre_a.append(y[:, :MXU_TILE])
            pre_x.append(y[:, MXU_TILE:])
        r = _sigmoid(jnp.concatenate(pre_a, axis=1) + ba_ref[...])
        ig = _sigmoid(jnp.concatenate(pre_x, axis=1) + bx_ref[...])
        log_a = r * neg_c_sp
        a = jnp.exp(log_a)
        a_s[bi] = a
        b_s[bi] = jnp.sqrt(-jnp.tanh(log_a) * (a * a + 1.0)) * (ig * x_bf.astype(F32))

    @pl.when(rev == 0)
    def _():
        for bi in range(nb):
            _lru_scan_block(a_s.at[bi], b_s.at[bi], h_ref.at[bi], carry_ref.at[bi], reverse=False)

    @pl.when(rev == 1)
    def _():
        for bi in range(nb):
            _lru_scan_block(a_s.at[bi], b_s.at[bi], h_ref.at[bi], carry_ref.at[bi], reverse=True)


def _lru_call(p, layout, w_tiles, b_a, b_x, lam, *, n_ctx_blocks):
    n_b, tt, _ = p.shape
    width = layout["lru_x"][1]
    n_blocks = tt // TOKEN_BLOCK
    n_tiles = w_tiles.shape[1]
    nb = SCAN_BATCH if n_b % SCAN_BATCH == 0 else 1

    return pl.pallas_call(
        _lru_kernel,
        out_shape=jax.ShapeDtypeStruct((2, n_b, tt, width), BF16),
        grid=(n_b // nb, 2, n_blocks),
        in_specs=[
            _scan_col_spec(layout, "lru_x", nb, TOKEN_BLOCK, n_ctx_blocks, n_blocks),
            pl.BlockSpec((None, n_tiles, MXU_TILE, 2 * MXU_TILE), lambda b, r, i: (r, 0, 0, 0)),
            pl.BlockSpec((None, 1, width), lambda b, r, i: (r, 0, 0)),
            pl.BlockSpec((None, 1, width), lambda b, r, i: (r, 0, 0)),
            pl.BlockSpec((None, 1, width), lambda b, r, i: (r, 0, 0)),
        ],
        out_specs=_scan_out_spec(nb, TOKEN_BLOCK, width, n_ctx_blocks, n_blocks),
        scratch_shapes=[pltpu.VMEM((nb, TOKEN_BLOCK, width), F32), pltpu.VMEM((nb, TOKEN_BLOCK, width), F32),
                        pltpu.VMEM((nb, 1, width), F32)],
        compiler_params=_cparams(3),
        name="rglru_scan",
    )(p, w_tiles, b_a, b_x, lam)


def _gla_kernel(q_ref, k_ref, v_ref, sm_ref, wg_ref, bg_ref, o_ref, state_ref, *, alr_off, dk, dv):
    rev = pl.program_id(1)
    i = pl.program_id(2)

    @pl.when(i == 0)
    def _():
        state_ref[...] = jnp.zeros_like(state_ref)

    for bi in range(q_ref.shape[0]):
        _gla_chunk(rev == 0, q_ref.at[bi], k_ref.at[bi], v_ref.at[bi], sm_ref.at[bi], wg_ref, bg_ref,
                   o_ref.at[bi], state_ref.at[bi], alr_off, dk, dv)


def _gla_chunk(fwd, q_ref, k_ref, v_ref, sm_ref, wg_ref, bg_ref, o_ref, state_ref, alr_off, dk, dv):
    q_len = q_ref.shape[0]
    n_heads = q_ref.shape[1] // dk
    sm = sm_ref[...]
    r = GLA_GATE_RANK
    a_lr = jnp.where(fwd, sm[:, alr_off:alr_off + r], sm[:, alr_off + r:alr_off + 2 * r])
    z = _f32_dot_small_k(a_lr, wg_ref[...]) + bg_ref[...]
    log_a = _log_sigmoid(z) * (1.0 / GLA_TAU)
    row = lax.broadcasted_iota(jnp.int32, (q_len, q_len), 0)
    col = lax.broadcasted_iota(jnp.int32, (q_len, q_len), 1)
    mask = (col - row) * jnp.where(fwd, 1, -1) <= 0
    tri = jnp.where(mask, 1.0, 0.0).astype(BF16)
    bcum = _exact_lhs_dot(tri, log_a, 2)
    btot = jnp.sum(log_a, axis=0, keepdims=True)
    bmid = bcum[q_len // 2:q_len // 2 + 1, :]
    q_in = q_ref[...].astype(F32) * (dk ** -0.5) * jnp.exp(bcum - bmid)
    k_in = k_ref[...].astype(F32) * jnp.exp(bmid - bcum)
    q_off = (q_in * jnp.exp(bmid)).astype(BF16)
    k_st = (k_in * jnp.exp(btot - bmid)).astype(BF16)
    q_in = q_in.astype(BF16)
    k_in = k_in.astype(BF16)
    e_tot = jnp.exp(btot)
    for hh in range(n_heads):
        ks = slice(hh * dk, (hh + 1) * dk)
        vs = slice(hh * dv, (hh + 1) * dv)
        v_bf = v_ref[:, vs]
        att = lax.dot_general(q_in[:, ks], k_in[:, ks], (((1,), (1,)), ((), ())), preferred_element_type=F32)
        att = jnp.where(mask, att, 0.0).astype(BF16)
        s_prev = state_ref[hh]
        inter = lax.dot_general(q_off[:, ks], s_prev.astype(BF16), (((1,), (1,)), ((), ())),
                                preferred_element_type=F32)
        o_ref[:, vs] = (_dot(att, v_bf) + inter).astype(o_ref.dtype)
        upd = lax.dot_general(v_bf, k_st[:, ks], (((0,), (0,)), ((), ())), preferred_element_type=F32)
        state_ref[hh] = s_prev * e_tot[:, ks] + upd


def _gla_call(p, small, layout, w_gate, b_gate, *, alr_off, n_ctx_chunks):
    n_b, tt, _ = p.shape
    key_w = layout["gla_q"][1]
    val_w = layout["gla_v"][1]
    dk = key_w // GLA_HEADS
    dv = val_w // GLA_HEADS
    n_chunks = tt // SCAN_CHUNK
    nb = SCAN_BATCH if n_b % SCAN_BATCH == 0 else 1
    col_spec = functools.partial(_scan_col_spec, layout, nb=nb, rows=SCAN_CHUNK, n_ctx=n_ctx_chunks, n_tot=n_chunks)

    kern = functools.partial(_gla_kernel, alr_off=alr_off, dk=dk, dv=dv)
    return pl.pallas_call(
        kern,
        out_shape=jax.ShapeDtypeStruct((2, n_b, tt, val_w), BF16),
        grid=(n_b // nb, 2, n_chunks),
        in_specs=[
            col_spec("gla_q"), col_spec("gla_k"), col_spec("gla_v"),
            _scan_small_spec(nb, SCAN_CHUNK, n_ctx_chunks, n_chunks),
            pl.BlockSpec((None, GLA_GATE_RANK, key_w), lambda b, r, i: (r, 0, 0)),
            pl.BlockSpec((None, 1, key_w), lambda b, r, i: (r, 0, 0)),
        ],
        out_specs=_scan_out_spec(nb, SCAN_CHUNK, val_w, n_ctx_chunks, n_chunks),
        scratch_shapes=[pltpu.VMEM((nb, GLA_HEADS, dv, dk), F32)],
        compiler_params=_cparams(3),
        name="gla_scan",
    )(p, p, p, small, w_gate, b_gate)


def _merge_kernel(h_ref, m_ref, ys0_ref, ys1_ref, xs_ref, z_ref, hl0_ref, hl1_ref, gl_ref,
                  og0_ref, og1_ref, gg_ref, gt0_ref, gt1_ref, gt2_ref, dskip_ref, sng_ref, gng_ref,
                  wbr_ref, wout_ref, lng_ref, lnb_ref, o_ref, *, alpha, dv):
    d = h_ref.shape[1]

    def f32(ref, cols=slice(None)):
        return ref[:, cols].astype(F32)

    y = (f32(ys0_ref) + f32(ys1_ref) + dskip_ref[...] * f32(xs_ref)) * _silu(f32(z_ref))
    y = y * lax.rsqrt(jnp.mean(y * y, axis=-1, keepdims=True) + NORM_EPS) * sng_ref[...]
    m = _sigmoid(f32(gt0_ref)) * _dot(y.astype(BF16), wbr_ref[0])
    y = (f32(hl0_ref) + f32(hl1_ref)) * jax.nn.gelu(f32(gl_ref))
    m = m + _sigmoid(f32(gt1_ref)) * _dot(y.astype(BF16), wbr_ref[1])
    parts = []
    for hh in range(d // dv):
        vs = slice(hh * dv, (hh + 1) * dv)
        o = f32(og0_ref, vs) + f32(og1_ref, vs)
        o = o * lax.rsqrt(jnp.mean(o * o, axis=-1, keepdims=True) + NORM_EPS) * gng_ref[...]
        parts.append(o * _silu(f32(gg_ref, vs)))
    y = jnp.concatenate(parts, axis=1)
    m = m + _sigmoid(f32(gt2_ref)) * _dot(y.astype(BF16), wbr_ref[2])
    out = _dot(m.astype(BF16), wout_ref[...])
    z = alpha * h_ref[...] + m_ref[5:6, :] * out
    o_ref[...] = _layer_norm(z, lng_ref[...], lnb_ref[...])


def _merge_call(h, mods, p, layout, y_ssd, h_lru, o_gla, d_skip_w, ssd_norm_g, gla_norm_g, w_branch, w_out,
                ln_g, ln_b, *, n_ctx_blocks, skip_blocks, alpha):
    n_b, tt, d = h.shape
    n_blocks = tt // TOKEN_BLOCK - skip_blocks
    ctx_row = mods.shape[0] - 1
    dv = gla_norm_g.shape[-1]

    def mod_map(b, t):
        return (jnp.where(t + skip_blocks < n_ctx_blocks, ctx_row, b), 0, 0)

    def tok(width, blk):
        return pl.BlockSpec((None, TOKEN_BLOCK, width), lambda b, t: (b, t + skip_blocks, blk))

    def col_spec(name):
        off, w = layout[name]
        blk = off // w
        assert blk * w == off
        return tok(w, blk)

    def gate_spec(n):
        off = layout["gates"][0] + n * d
        assert off % d == 0
        return tok(d, off // d)

    def dir_spec(rv):
        return pl.BlockSpec((None, None, TOKEN_BLOCK, d), lambda b, t: (rv, b, t + skip_blocks, 0))

    kern = functools.partial(_merge_kernel, alpha=alpha, dv=dv)
    return pl.pallas_call(
        kern,
        out_shape=jax.ShapeDtypeStruct((n_b, n_blocks * TOKEN_BLOCK, d), F32),
        grid=(n_b, n_blocks),
        in_specs=[
            tok(d, 0),
            pl.BlockSpec((None, N_MOD, d), mod_map),
            dir_spec(0), dir_spec(1), col_spec("ssd_x"), col_spec("ssd_z"),
            dir_spec(0), dir_spec(1), col_spec("lru_g"),
            dir_spec(0), dir_spec(1), col_spec("gla_g"),
            gate_spec(0), gate_spec(1), gate_spec(2),
            _resident((1, d), lambda b, t: (0, 0)),
            _resident((1, d), lambda b, t: (0, 0)),
            _resident((1, dv), lambda b, t: (0, 0)),
            _resident((N_BRANCH, d, d), lambda b, t: (0, 0, 0)),
            _resident((d, d), lambda b, t: (0, 0)),
            _resident((1, d), lambda b, t: (0, 0)),
            _resident((1, d), lambda b, t: (0, 0)),
        ],
        out_specs=pl.BlockSpec((None, TOKEN_BLOCK, d), lambda b, t: (b, t, 0)),
        compiler_params=_cparams(2),
        name="mixer_merge",
    )(h, mods, y_ssd, y_ssd, p, p, h_lru, h_lru, p, o_gla, o_gla, p, p, p, p,
      d_skip_w, ssd_norm_g.reshape(1, d), gla_norm_g.reshape(1, dv), w_branch, w_out,
      ln_g.reshape(1, d), ln_b.reshape(1, d))


def _permute_w_in(w_in_l, d, ssd_heads, layout, total):
    ssd_w = d
    bc_w = SSD_GROUPS * SSD_STATE
    gla_key = d // 2
    sizes = (ssd_w, ssd_w + 2 * bc_w, 2 * ssd_heads, d, d, gla_key, gla_key, d, d, 2 * GLA_GATE_RANK, N_BRANCH * d)
    offs = [0]
    for s in sizes:
        offs.append(offs[-1] + s)
    w_in_l = w_in_l.astype(BF16)
    z, xbc, dtr, lx, lg, gq, gk, gv, gg, alr, gates = [w_in_l[:, offs[n]:offs[n + 1]] for n in range(len(sizes))]
    small = jnp.concatenate([dtr, alr, jnp.zeros((d, LANES - dtr.shape[1] - alr.shape[1]), w_in_l.dtype)], axis=1)
    w = jnp.concatenate([xbc, lx, z, lg, gq, gk, gv, gg, gates, small], axis=1)
    assert w.shape[1] == total
    return w.astype(BF16)


def _lru_gate_tiles(w_a, w_x):
    per = MXU_TILE // LRU_BLOCK
    n_dir, nb, k, _ = w_a.shape
    n_tiles = nb // per

    def tiles(w):
        w = w.reshape(n_dir, n_tiles, per, k, k)
        eye = jnp.eye(per, dtype=w.dtype)
        t = jnp.einsum("dtpij,pq->dtpiqj", w, eye)
        return t.reshape(n_dir, n_tiles, MXU_TILE, MXU_TILE)

    return jnp.concatenate([tiles(w_a), tiles(w_x)], axis=-1).astype(BF16)


def _to_colmajor(h, n_ctx, rows):
    n_b, _, d = h.shape
    lat = h[:, n_ctx:].reshape(n_b, rows, GRID_W, d).transpose(0, 2, 1, 3).reshape(n_b, rows * GRID_W, d)
    return jnp.concatenate([h[:, :n_ctx], lat], axis=1)


def _to_raster(h, n_ctx, rows):
    n_b, _, d = h.shape
    lat = h[:, n_ctx:].reshape(n_b, GRID_W, rows, d).transpose(0, 2, 1, 3).reshape(n_b, rows * GRID_W, d)
    return jnp.concatenate([h[:, :n_ctx], lat], axis=1)


def kernel(x, c, ctx, c_ctx, w_ada, b_ada, ln_g, ln_b, ffn_w_up, ffn_w_down, w_in, ssd_conv_w, ssd_conv_b,
           ssd_dt_bias, ssd_a_log, ssd_d, ssd_norm_g, lru_conv_w, lru_conv_b, lru_w_a, lru_b_a, lru_w_x, lru_b_x,
           lru_lam, gla_w_gate, gla_b_gate, gla_norm_g, w_branch, w_out):
    n_b, t_lat, d = x.shape
    n_ctx = ctx.shape[1]
    depth = w_ada.shape[0]
    rows = t_lat // GRID_W
    ssd_heads = ssd_dt_bias.shape[-1]
    assert n_ctx % TOKEN_BLOCK == 0 and t_lat % TOKEN_BLOCK == 0
    for line in (n_ctx, GRID_W, rows):
        assert line & (line - 1) == 0 and TOKEN_BLOCK % min(line, TOKEN_BLOCK) == 0
    assert n_ctx <= TOKEN_BLOCK
    alpha = (2.0 * depth) ** 0.25
    n_ctx_blocks = n_ctx // TOKEN_BLOCK
    n_ctx_chunks = n_ctx // SCAN_CHUNK

    layout, total = _proj_layout(d, ssd_heads)
    conv_cols = layout["lru_x"][0] + layout["lru_x"][1]
    silu_cols = layout["lru_x"][0]
    alr_off = 2 * ssd_heads

    n_rows = -(-(n_b + 1) // SUBLANES) * SUBLANES
    cond = jnp.concatenate([c, jnp.zeros((n_rows - n_b - 1, d), F32), c_ctx[None, :]], axis=0)
    mods_all = _ada_call(cond, w_ada, b_ada).reshape(depth, n_rows, N_MOD, d)

    expand = jnp.repeat(jnp.eye(ssd_heads, dtype=BF16), SSD_HEAD_DIM, axis=1)
    expand = jnp.concatenate([expand] * 3, axis=0)
    h = jnp.concatenate([ctx, x], axis=1)
    for l in range(depth):
        last = l == depth - 1
        col_major = l % 2 == 1
        mods = mods_all[l]
        wup = ffn_w_up[l].astype(BF16)
        wdn = ffn_w_down[l].astype(BF16)

        if col_major:
            h = _to_colmajor(h, n_ctx, rows)
        h = _ffn_call(h, mods, wup[0], wdn[0], ln_g[l, 0], ln_b[l, 0], j=0, n_ctx_blocks=n_ctx_blocks, alpha=alpha)

        w_perm = _permute_w_in(w_in[l], d, ssd_heads, layout, total)
        conv_w = jnp.concatenate([ssd_conv_w[l], lru_conv_w[l]], axis=1)
        conv_b = jnp.concatenate([ssd_conv_b[l], lru_conv_b[l]], axis=0)[None, :]
        p, small = _proj_call(h, mods, w_perm, conv_w, conv_b, conv_cols=conv_cols, silu_cols=silu_cols,
                              n_ctx_blocks=n_ctx_blocks, line_ctx=n_ctx, line_lat=rows if col_major else GRID_W)

        y_ssd = _ssd_call(p, small, layout, ssd_dt_bias[l], ssd_a_log[l], expand, n_ctx_chunks=n_ctx_chunks)
        h_lru = _lru_call(p, layout, _lru_gate_tiles(lru_w_a[l], lru_w_x[l]), lru_b_a[l][:, None, :],
                          lru_b_x[l][:, None, :], lru_lam[l][:, None, :], n_ctx_blocks=n_ctx_blocks)
        o_gla = _gla_call(p, small, layout, gla_w_gate[l], gla_b_gate[l][:, None, :], alr_off=alr_off,
                          n_ctx_chunks=n_ctx_chunks)

        d_skip_w = jnp.repeat(ssd_d[l, 0] + ssd_d[l, 1], SSD_HEAD_DIM)[None, :]
        skip_blocks = n_ctx_blocks if last else 0
        h = _merge_call(h, mods, p, layout, y_ssd, h_lru, o_gla, d_skip_w, ssd_norm_g[l], gla_norm_g[l],
                        w_branch[l].astype(BF16), w_out[l].astype(BF16), ln_g[l, 1], ln_b[l, 1],
                        n_ctx_blocks=n_ctx_blocks, skip_blocks=skip_blocks, alpha=alpha)
        n_ctx_now = 0 if last else n_ctx
        h = _ffn_call(h, mods, wup[1], wdn[1], ln_g[l, 2], ln_b[l, 2], j=2,
                      n_ctx_blocks=n_ctx_now // TOKEN_BLOCK, alpha=alpha)
        if col_major:
            h = _to_raster(h, n_ctx_now, rows)
    return h
```

```python
import functools
import math

import jax
import jax.numpy as jnp
from jax import lax
from jax.experimental import pallas as pl
from jax.experimental.pallas import tpu as pltpu

F32 = jnp.float32
BF16 = jnp.bfloat16

GRID_W = 64
CONV_W = 4
CONV_LEFT = 2
SSD_HEAD_DIM = 64
SSD_GROUPS = 4
SSD_STATE = 128
LRU_BLOCK = 64
LRU_C = 8.0
GLA_HEADS = 4
GLA_GATE_RANK = 16
GLA_TAU = 16.0
N_MOD = 9
N_BRANCH = 3
FFN_RES_W = 0.5
NORM_EPS = 1e-5

TOKEN_BLOCK = 256
SCAN_CHUNK = 128
SCAN_BATCH = 2
MXU_TILE = 256
LANES = 128
SUBLANES = 8
VMEM_LIMIT = 56 * 1024 * 1024


def _cparams(n_axes):
    return pltpu.CompilerParams(dimension_semantics=("arbitrary",) * n_axes,
                                vmem_limit_bytes=VMEM_LIMIT)


def _resident(block_shape, index_map):
    return pl.BlockSpec(block_shape, index_map, pipeline_mode=pl.Buffered(1))


def _sigmoid(x):
    return 0.5 * jnp.tanh(0.5 * x) + 0.5


def _silu(x):
    return x * _sigmoid(x)


def _softplus(x):
    return jnp.maximum(x, 0.0) + jnp.log1p(jnp.exp(-jnp.abs(x)))


def _log_sigmoid(x):
    return -_softplus(-x)


def _split_terms(x, n):
    terms = []
    for _ in range(n - 1):
        t = x.astype(BF16)
        terms.append(t)
        x = x - t.astype(F32)
    terms.append(x.astype(BF16))
    return terms


_dot = functools.partial(jnp.dot, preferred_element_type=F32)


def _exact_lhs_dot(t_bf16, x, n_terms):
    return _dot(jnp.concatenate([t_bf16] * n_terms, axis=1), jnp.concatenate(_split_terms(x, n_terms), axis=0))


def _exact_rhs_dot(x, e3_bf16):
    return _dot(jnp.concatenate(_split_terms(x, 3), axis=1), e3_bf16)


def _f32_dot_small_k(a, w):
    at = _split_terms(a, 3)
    wt = _split_terms(w, 3)
    pairs = [(0, 0), (0, 1), (0, 2), (1, 0), (1, 1), (2, 0)]
    lhs = jnp.concatenate([at[i] for i, _ in pairs], axis=1)
    rhs = jnp.concatenate([wt[j] for _, j in pairs], axis=0)
    return _dot(lhs, rhs)


def _layer_norm(z, g, b):
    mu = jnp.mean(z, axis=-1, keepdims=True)
    zc = z - mu
    var = jnp.mean(zc * zc, axis=-1, keepdims=True)
    return zc * lax.rsqrt(var + NORM_EPS) * g + b


def _scan_order(i, rev, n_ctx, n_tot):
    back = jnp.where(i < n_ctx, n_ctx - 1 - i, n_ctx + n_tot - 1 - i)
    return jnp.where(rev == 0, i, back)


def _ada_kernel(s_ref, w_ref, b_ref, o_ref):
    s = _silu(s_ref[...])
    o_ref[...] = jnp.dot(s, w_ref[...], preferred_element_type=F32,
                         precision=lax.Precision.HIGHEST) + b_ref[...]


def _ada_call(cond, w_ada, b_ada):
    n_layers, d, nd = w_ada.shape
    rows = cond.shape[0]
    n_col = nd // d
    return pl.pallas_call(
        _ada_kernel,
        out_shape=jax.ShapeDtypeStruct((n_layers, rows, nd), F32),
        grid=(n_layers, n_col),
        in_specs=[
            pl.BlockSpec((rows, d), lambda l, j: (0, 0)),
            pl.BlockSpec((None, d, d), lambda l, j: (l, 0, j)),
            pl.BlockSpec((None, 1, d), lambda l, j: (l, 0, j)),
        ],
        out_specs=pl.BlockSpec((None, rows, d), lambda l, j: (l, 0, j)),
        compiler_params=_cparams(2),
        name="ada_mod",
    )(cond, w_ada, b_ada.reshape(n_layers, 1, nd))


def _ffn_kernel(x_ref, m_ref, wup_ref, wdn_ref, g_ref, b_ref, o_ref, *, j, alpha, d_ff, ff_chunk):
    x = x_ref[...]
    shift = m_ref[3 * j:3 * j + 1, :]
    scale = m_ref[3 * j + 1:3 * j + 2, :]
    gate = m_ref[3 * j + 2:3 * j + 3, :]
    u = (x * (1.0 + scale) + shift).astype(BF16)
    acc = None
    for c0 in range(0, d_ff, ff_chunk):
        a = _dot(u, wup_ref[:, c0:c0 + ff_chunk])
        v = _dot(u, wup_ref[:, d_ff + c0:d_ff + c0 + ff_chunk])
        hid = (_silu(a) * v).astype(BF16)
        part = _dot(hid, wdn_ref[c0:c0 + ff_chunk, :])
        acc = part if acc is None else acc + part
    z = alpha * x + (FFN_RES_W * gate) * acc
    o_ref[...] = _layer_norm(z, g_ref[...], b_ref[...])


def _ffn_call(h, mods, w_up, w_dn, ln_g, ln_b, *, j, n_ctx_blocks, alpha):
    n_b, tt, d = h.shape
    d_ff = w_dn.shape[0]
    ff_chunk = d_ff // 2 if (d_ff // 2) % LANES == 0 else d_ff
    ctx_row = mods.shape[0] - 1
    n_blocks = tt // TOKEN_BLOCK

    def mod_map(b, t):
        return (jnp.where(t < n_ctx_blocks, ctx_row, b), 0, 0)

    kern = functools.partial(_ffn_kernel, j=j, alpha=alpha, d_ff=d_ff, ff_chunk=ff_chunk)
    return pl.pallas_call(
        kern,
        out_shape=jax.ShapeDtypeStruct(h.shape, F32),
        grid=(n_b, n_blocks),
        in_specs=[
            pl.BlockSpec((None, TOKEN_BLOCK, d), lambda b, t: (b, t, 0)),
            pl.BlockSpec((None, N_MOD, d), mod_map),
            _resident((d, 2 * d_ff), lambda b, t: (0, 0)),
            _resident((d_ff, d), lambda b, t: (0, 0)),
            _resident((1, d), lambda b, t: (0, 0)),
            _resident((1, d), lambda b, t: (0, 0)),
        ],
        out_specs=pl.BlockSpec((None, TOKEN_BLOCK, d), lambda b, t: (b, t, 0)),
        compiler_params=_cparams(2),
        name="ffn_sublayer",
    )(h, mods, w_up, w_dn, ln_g.reshape(1, d), ln_b.reshape(1, d))


def _proj_layout(d, ssd_heads):
    ssd_w = d
    bc_w = SSD_GROUPS * SSD_STATE
    gla_key = d // 2
    names = [("ssd_x", ssd_w), ("ssd_b", bc_w), ("ssd_c", bc_w), ("lru_x", d), ("ssd_z", ssd_w),
             ("lru_g", d), ("gla_q", gla_key), ("gla_k", gla_key), ("gla_v", d), ("gla_g", d),
             ("gates", N_BRANCH * d), ("small", LANES)]
    off, out = 0, {}
    for name, width in names:
        out[name] = (off, width)
        off += width
    return out, off


_ACTIVATIONS = {"none": lambda v: v, "silu": _silu, "sigmoid": _sigmoid, "gelu": jax.nn.gelu}


def _proj_kernel(x_ref, m_ref, w_ref, cw_ref, cb_ref, o_ref, sm_ref, ybuf, *, segments, main_cols,
                 n_ctx_blocks, line_ctx, line_lat):
    x = x_ref[...]
    u = (x * (1.0 + m_ref[4:5, :]) + m_ref[3:4, :]).astype(BF16)
    t = pl.program_id(1)
    line = jnp.where(t < n_ctx_blocks, line_ctx, line_lat)
    rows = x.shape[0]
    halo = SUBLANES
    sm_ref[...] = _dot(u, w_ref[:, main_cols:main_cols + LANES])
    pos = lax.broadcasted_iota(jnp.int32, (rows, LANES), 0) & (line - 1)
    taps = [(k, k - CONV_LEFT) for k in range(CONV_W) if k != CONV_LEFT]
    valid = {off: (pos + off >= 0) & (pos + off < line) for _, off in taps}
    zeros = jnp.zeros((halo, LANES), F32)
    for c0, cw, conv, act in segments:
        y = _dot(u, w_ref[:, c0:c0 + cw])
        if not conv:
            o_ref[:, c0:c0 + cw] = _ACTIVATIONS[act](y).astype(o_ref.dtype)
            continue
        for j in range(cw // LANES):
            ybuf[j, 0:halo, :] = zeros
            ybuf[j, halo:halo + rows, :] = y[:, j * LANES:(j + 1) * LANES]
            ybuf[j, halo + rows:2 * halo + rows, :] = zeros
        for j in range(cw // LANES):
            cols = slice(c0 + j * LANES, c0 + (j + 1) * LANES)
            acc = cb_ref[:, cols] + ybuf[j, halo:halo + rows, :] * cw_ref[CONV_LEFT:CONV_LEFT + 1, cols]
            for k, off in taps:
                shifted = ybuf[j, halo + off:halo + off + rows, :]
                acc = acc + jnp.where(valid[off], shifted, 0.0) * cw_ref[k:k + 1, cols]
            o_ref[:, cols] = _ACTIVATIONS[act](acc).astype(o_ref.dtype)


def _proj_segments(layout, col_chunk):
    kinds = {"ssd_x": (True, "silu"), "ssd_b": (True, "silu"), "ssd_c": (True, "silu"), "lru_x": (True, "none"),
             "ssd_z": (False, "silu"), "lru_g": (False, "gelu"), "gla_q": (False, "none"), "gla_k": (False, "none"),
             "gla_v": (False, "none"), "gla_g": (False, "silu"), "gates": (False, "sigmoid")}
    segs = []
    for name, (conv, act) in kinds.items():
        off, width = layout[name]
        for c0 in range(off, off + width, col_chunk):
            segs.append((c0, min(col_chunk, off + width - c0), conv, act))
    return tuple(segs)


def _proj_call(h, mods, w_perm, conv_w, conv_b, layout, *, n_ctx_blocks, line_ctx, line_lat):
    n_b, tt, d = h.shape
    total = w_perm.shape[1]
    conv_cols = conv_w.shape[1]
    ctx_row = mods.shape[0] - 1
    n_blocks = tt // TOKEN_BLOCK
    col_chunk = 1024

    def mod_map(b, t):
        return (jnp.where(t < n_ctx_blocks, ctx_row, b), 0, 0)

    main = total - LANES
    kern = functools.partial(_proj_kernel, segments=_proj_segments(layout, col_chunk), main_cols=main,
                             n_ctx_blocks=n_ctx_blocks, line_ctx=line_ctx, line_lat=line_lat)
    return pl.pallas_call(
        kern,
        out_shape=(jax.ShapeDtypeStruct((n_b, tt, main), BF16), jax.ShapeDtypeStruct((n_b, tt, LANES), F32)),
        grid=(n_b, n_blocks),
        in_specs=[
            pl.BlockSpec((None, TOKEN_BLOCK, d), lambda b, t: (b, t, 0)),
            pl.BlockSpec((None, N_MOD, d), mod_map),
            _resident((d, total), lambda b, t: (0, 0)),
            _resident((CONV_W, conv_cols), lambda b, t: (0, 0)),
            _resident((1, conv_cols), lambda b, t: (0, 0)),
        ],
        out_specs=(pl.BlockSpec((None, TOKEN_BLOCK, main), lambda b, t: (b, t, 0)),
                   pl.BlockSpec((None, TOKEN_BLOCK, LANES), lambda b, t: (b, t, 0))),
        scratch_shapes=[pltpu.VMEM((col_chunk // LANES, TOKEN_BLOCK + 2 * SUBLANES, LANES), F32)],
        compiler_params=_cparams(2),
        name="mixer_in_proj",
    )(h, mods, w_perm, conv_w, conv_b)


def _ssd_kernel(x_ref, b_ref, c_ref, sm_ref, dtb_ref, alog_ref, exp_ref, y_ref, state_ref, *, n_heads):
    rev = pl.program_id(1)
    i = pl.program_id(2)

    @pl.when(i == 0)
    def _():
        state_ref[...] = jnp.zeros_like(state_ref)

    for bi in range(x_ref.shape[0]):
        _ssd_chunk(rev == 0, x_ref.at[bi], b_ref.at[bi], c_ref.at[bi], sm_ref.at[bi], dtb_ref, alog_ref, exp_ref,
                   y_ref.at[bi], state_ref.at[bi], n_heads)


def _ssd_chunk(fwd, x_ref, b_ref, c_ref, sm_ref, dtb_ref, alog_ref, exp_ref, y_ref, state_ref, n_heads):
    q = x_ref.shape[0]
    hg = n_heads // SSD_GROUPS
    gw = hg * SSD_HEAD_DIM
    sm = sm_ref[...]
    dt_raw = jnp.where(fwd, sm[:, 0:n_heads], sm[:, n_heads:2 * n_heads])
    dt_bias = jnp.where(fwd, dtb_ref[0:1, :], dtb_ref[1:2, :])
    a_log = jnp.where(fwd, alog_ref[0:1, :], alog_ref[1:2, :])
    dt = _softplus(dt_raw + dt_bias)
    a = dt * (-jnp.exp(a_log))
    row = lax.broadcasted_iota(jnp.int32, (q, q), 0)
    col = lax.broadcasted_iota(jnp.int32, (q, q), 1)
    mask = (col - row) * jnp.where(fwd, 1, -1) <= 0
    tri = jnp.where(mask, 1.0, 0.0).astype(BF16)
    cs = _exact_lhs_dot(tri, a, 3)
    tot = jnp.sum(a, axis=0, keepdims=True)
    pad = jnp.zeros((q, LANES - 2 * n_heads), F32)
    both_t = jnp.concatenate([cs, dt, pad], axis=1).T
    cs_t = both_t[0:n_heads]
    dt_t = both_t[n_heads:2 * n_heads]

    w_state = dt * jnp.exp(tot - cs)
    from_start = jnp.exp(cs)
    e_tot = jnp.broadcast_to(jnp.exp(tot), (SUBLANES, n_heads))
    wide = _exact_rhs_dot(jnp.concatenate([w_state, from_start, e_tot], axis=0), exp_ref[...])
    w_state_w = wide[0:q]
    from_start_w = wide[q:2 * q]
    e_tot_w = wide[2 * q:2 * q + 1]

    x_bf = x_ref[...]
    xw_bf = (x_bf.astype(F32) * w_state_w).astype(BF16)
    for g in range(SSD_GROUPS):
        bt = b_ref[:, g * SSD_STATE:(g + 1) * SSD_STATE].T
        cg = c_ref[:, g * SSD_STATE:(g + 1) * SSD_STATE]
        scores = _dot(cg, bt)
        s_prev = state_ref[g]
        y_off = _dot(cg, s_prev.astype(BF16)) * from_start_w[:, g * gw:(g + 1) * gw]
        parts = []
        for hh in range(hg):
            head = g * hg + hh
            seg = jnp.where(mask, cs[:, head:head + 1] - cs_t[head:head + 1, :], -1e30)
            m = scores * jnp.exp(seg) * dt_t[head:head + 1, :]
            parts.append(_dot(m.astype(BF16), x_bf[:, head * SSD_HEAD_DIM:(head + 1) * SSD_HEAD_DIM]))
        y_ref[:, g * gw:(g + 1) * gw] = (y_off + jnp.concatenate(parts, axis=1)).astype(y_ref.dtype)
        upd = _dot(bt, xw_bf[:, g * gw:(g + 1) * gw])
        state_ref[g] = s_prev * e_tot_w[:, g * gw:(g + 1) * gw] + upd


def _scan_col_spec(layout, name, nb, rows, n_ctx, n_tot):
    off, w = layout[name]
    assert off % w == 0
    blk = off // w
    return pl.BlockSpec((nb, rows, w), lambda b, r, i: (b, _scan_order(i, r, n_ctx, n_tot), blk))


def _scan_small_spec(nb, rows, n_ctx, n_tot):
    return pl.BlockSpec((nb, rows, LANES), lambda b, r, i: (b, _scan_order(i, r, n_ctx, n_tot), 0))


def _scan_out_spec(nb, rows, width, n_ctx, n_tot):
    return pl.BlockSpec((None, nb, rows, width), lambda b, r, i: (r, b, _scan_order(i, r, n_ctx, n_tot), 0))


def _ssd_call(p, small, layout, dt_bias, a_log, expand, *, n_ctx_chunks):
    n_b, tt, _ = p.shape
    n_heads = dt_bias.shape[1]
    width = n_heads * SSD_HEAD_DIM
    n_chunks = tt // SCAN_CHUNK
    hg = n_heads // SSD_GROUPS
    nb = SCAN_BATCH if n_b % SCAN_BATCH == 0 else 1
    col_spec = functools.partial(_scan_col_spec, layout, nb=nb, rows=SCAN_CHUNK, n_ctx=n_ctx_chunks, n_tot=n_chunks)

    kern = functools.partial(_ssd_kernel, n_heads=n_heads)
    return pl.pallas_call(
        kern,
        out_shape=jax.ShapeDtypeStruct((2, n_b, tt, width), BF16),
        grid=(n_b // nb, 2, n_chunks),
        in_specs=[
            col_spec("ssd_x"), col_spec("ssd_b"), col_spec("ssd_c"),
            _scan_small_spec(nb, SCAN_CHUNK, n_ctx_chunks, n_chunks),
            _resident((2, n_heads), lambda b, r, i: (0, 0)),
            _resident((2, n_heads), lambda b, r, i: (0, 0)),
            _resident(expand.shape, lambda b, r, i: (0, 0)),
        ],
        out_specs=_scan_out_spec(nb, SCAN_CHUNK, width, n_ctx_chunks, n_chunks),
        scratch_shapes=[pltpu.VMEM((nb, SSD_GROUPS, SSD_STATE, hg * SSD_HEAD_DIM), F32)],
        compiler_params=_cparams(3),
        name="ssd_scan",
    )(p, p, p, small, dt_bias, a_log, expand)


def _lru_scan_block(a_ref, b_ref, o_ref, carry_ref, reverse):
    rows, width = a_ref.shape
    sub = lax.broadcasted_iota(jnp.int32, (SUBLANES, width), 0)
    carry = carry_ref[...]
    n_groups = rows // SUBLANES
    order = range(n_groups - 1, -1, -1) if reverse else range(n_groups)
    for gi in order:
        sl = slice(gi * SUBLANES, (gi + 1) * SUBLANES)
        a = a_ref[sl, :]
        b = b_ref[sl, :]
        k = 1
        while k < SUBLANES:
            if reverse:
                a_sh = pltpu.roll(a, SUBLANES - k, axis=0)
                b_sh = pltpu.roll(b, SUBLANES - k, axis=0)
                ok = sub < SUBLANES - k
            else:
                a_sh = pltpu.roll(a, k, axis=0)
                b_sh = pltpu.roll(b, k, axis=0)
                ok = sub >= k
            b = b + a * jnp.where(ok, b_sh, 0.0)
            a = a * jnp.where(ok, a_sh, 1.0)
            k *= 2
        hblk = b + a * carry
        b_ref[sl, :] = hblk
        carry = hblk[0:1, :] if reverse else hblk[SUBLANES - 1:SUBLANES, :]
    carry_ref[...] = carry
    o_ref[...] = b_ref[...].astype(o_ref.dtype)


def _lru_kernel(x_ref, w_ref, ba_ref, bx_ref, lam_ref, h_ref, a_s, b_s, carry_ref):
    rev = pl.program_id(1)
    i = pl.program_id(2)
    nb = x_ref.shape[0]

    @pl.when(i == 0)
    def _():
        carry_ref[...] = jnp.zeros_like(carry_ref)

    n_tiles = w_ref.shape[0]
    neg_c_sp = (-LRU_C) * _softplus(-lam_ref[...])
    for bi in range(nb):
        x_bf = x_ref[bi]
        pre_a, pre_x = [], []
        for t in range(n_tiles):
            y = _dot(x_bf[:, t * MXU_TILE:(t + 1) * MXU_TILE], w_ref[t])
            pre_a.append(y[:, :MXU_TILE])
            pre_x.append(y[:, MXU_TILE:])
        r = _sigmoid(jnp.concatenate(pre_a, axis=1) + ba_ref[...])
        ig = _sigmoid(jnp.concatenate(pre_x, axis=1) + bx_ref[...])
        log_a = r * neg_c_sp
        a = jnp.exp(log_a)
        a_s[bi] = a
        b_s[bi] = jnp.sqrt(-jnp.tanh(log_a) * (a * a + 1.0)) * (ig * x_bf.astype(F32))

    @pl.when(rev == 0)
    def _():
        for bi in range(nb):
            _lru_scan_block(a_s.at[bi], b_s.at[bi], h_ref.at[bi], carry_ref.at[bi], reverse=False)

    @pl.when(rev == 1)
    def _():
        for bi in range(nb):
            _lru_scan_block(a_s.at[bi], b_s.at[bi], h_ref.at[bi], carry_ref.at[bi], reverse=True)


def _lru_call(p, layout, w_tiles, b_a, b_x, lam, *, n_ctx_blocks):
    n_b, tt, _ = p.shape
    width = layout["lru_x"][1]
    n_blocks = tt // TOKEN_BLOCK
    n_tiles = w_tiles.shape[1]
    nb = SCAN_BATCH if n_b % SCAN_BATCH == 0 else 1

    return pl.pallas_call(
        _lru_kernel,
        out_shape=jax.ShapeDtypeStruct((2, n_b, tt, width), BF16),
        grid=(n_b // nb, 2, n_blocks),
        in_specs=[
            _scan_col_spec(layout, "lru_x", nb, TOKEN_BLOCK, n_ctx_blocks, n_blocks),
            pl.BlockSpec((None, n_tiles, MXU_TILE, 2 * MXU_TILE), lambda b, r, i: (r, 0, 0, 0)),
            pl.BlockSpec((None, 1, width), lambda b, r, i: (r, 0, 0)),
            pl.BlockSpec((None, 1, width), lambda b, r, i: (r, 0, 0)),
            pl.BlockSpec((None, 1, width), lambda b, r, i: (r, 0, 0)),
        ],
        out_specs=_scan_out_spec(nb, TOKEN_BLOCK, width, n_ctx_blocks, n_blocks),
        scratch_shapes=[pltpu.VMEM((nb, TOKEN_BLOCK, width), F32), pltpu.VMEM((nb, TOKEN_BLOCK, width), F32),
                        pltpu.VMEM((nb, 1, width), F32)],
        compiler_params=_cparams(3),
        name="rglru_scan",
    )(p, w_tiles, b_a, b_x, lam)


def _gla_kernel(q_ref, k_ref, v_ref, sm_ref, wg_ref, bg_ref, o_ref, state_ref, *, alr_off, dk, dv):
    rev = pl.program_id(1)
    i = pl.program_id(2)

    @pl.when(i == 0)
    def _():
        state_ref[...] = jnp.zeros_like(state_ref)

    for bi in range(q_ref.shape[0]):
        _gla_chunk(rev == 0, q_ref.at[bi], k_ref.at[bi], v_ref.at[bi], sm_ref.at[bi], wg_ref, bg_ref,
                   o_ref.at[bi], state_ref.at[bi], alr_off, dk, dv)


def _gla_chunk(fwd, q_ref, k_ref, v_ref, sm_ref, wg_ref, bg_ref, o_ref, state_ref, alr_off, dk, dv):
    q_len = q_ref.shape[0]
    n_heads = q_ref.shape[1] // dk
    sm = sm_ref[...]
    r = GLA_GATE_RANK
    a_lr = jnp.where(fwd, sm[:, alr_off:alr_off + r], sm[:, alr_off + r:alr_off + 2 * r])
    z = _f32_dot_small_k(a_lr, wg_ref[...]) + bg_ref[...]
    log_a = _log_sigmoid(z) * (1.0 / GLA_TAU)
    row = lax.broadcasted_iota(jnp.int32, (q_len, q_len), 0)
    col = lax.broadcasted_iota(jnp.int32, (q_len, q_len), 1)
    mask = (col - row) * jnp.where(fwd, 1, -1) <= 0
    tri = jnp.where(mask, 1.0, 0.0).astype(BF16)
    bcum = _exact_lhs_dot(tri, log_a, 2)
    btot = jnp.sum(log_a, axis=0, keepdims=True)
    bmid = bcum[q_len // 2:q_len // 2 + 1, :]
    q_in = q_ref[...].astype(F32) * (dk ** -0.5) * jnp.exp(bcum - bmid)
    k_in = k_ref[...].astype(F32) * jnp.exp(bmid - bcum)
    q_off = (q_in * jnp.exp(bmid)).astype(BF16)
    k_st = (k_in * jnp.exp(btot - bmid)).astype(BF16)
    q_in = q_in.astype(BF16)
    k_in = k_in.astype(BF16)
    e_tot = jnp.exp(btot)
    for hh in range(n_heads):
        ks = slice(hh * dk, (hh + 1) * dk)
        vs = slice(hh * dv, (hh + 1) * dv)
        v_bf = v_ref[:, vs]
        att = lax.dot_general(q_in[:, ks], k_in[:, ks], (((1,), (1,)), ((), ())), preferred_element_type=F32)
        att = jnp.where(mask, att, 0.0).astype(BF16)
        s_prev = state_ref[hh]
        inter = lax.dot_general(q_off[:, ks], s_prev.astype(BF16), (((1,), (1,)), ((), ())),
                                preferred_element_type=F32)
        o_ref[:, vs] = (_dot(att, v_bf) + inter).astype(o_ref.dtype)
        upd = lax.dot_general(v_bf, k_st[:, ks], (((0,), (0,)), ((), ())), preferred_element_type=F32)
        state_ref[hh] = s_prev * e_tot[:, ks] + upd


def _gla_call(p, small, layout, w_gate, b_gate, *, alr_off, n_ctx_chunks):
    n_b, tt, _ = p.shape
    key_w = layout["gla_q"][1]
    val_w = layout["gla_v"][1]
    dk = key_w // GLA_HEADS
    dv = val_w // GLA_HEADS
    n_chunks = tt // SCAN_CHUNK
    nb = SCAN_BATCH if n_b % SCAN_BATCH == 0 else 1
    col_spec = functools.partial(_scan_col_spec, layout, nb=nb, rows=SCAN_CHUNK, n_ctx=n_ctx_chunks, n_tot=n_chunks)

    kern = functools.partial(_gla_kernel, alr_off=alr_off, dk=dk, dv=dv)
    return pl.pallas_call(
        kern,
        out_shape=jax.ShapeDtypeStruct((2, n_b, tt, val_w), BF16),
        grid=(n_b // nb, 2, n_chunks),
        in_specs=[
            col_spec("gla_q"), col_spec("gla_k"), col_spec("gla_v"),
            _scan_small_spec(nb, SCAN_CHUNK, n_ctx_chunks, n_chunks),
            pl.BlockSpec((None, GLA_GATE_RANK, key_w), lambda b, r, i: (r, 0, 0)),
            pl.BlockSpec((None, 1, key_w), lambda b, r, i: (r, 0, 0)),
        ],
        out_specs=_scan_out_spec(nb, SCAN_CHUNK, val_w, n_ctx_chunks, n_chunks),
        scratch_shapes=[pltpu.VMEM((nb, GLA_HEADS, dv, dk), F32)],
        compiler_params=_cparams(3),
        name="gla_scan",
    )(p, p, p, small, w_gate, b_gate)


def _merge_kernel(h_ref, m_ref, ys0_ref, ys1_ref, xs_ref, z_ref, hl0_ref, hl1_ref, gl_ref,
                  og0_ref, og1_ref, gg_ref, gt0_ref, gt1_ref, gt2_ref, dskip_ref, sng_ref, gng_ref,
                  wbr_ref, wout_ref, lng_ref, lnb_ref, o_ref, *, alpha, dv):
    d = h_ref.shape[1]

    def f32(ref, cols=slice(None)):
        return ref[:, cols].astype(F32)

    y = (f32(ys0_ref) + f32(ys1_ref) + dskip_ref[...] * f32(xs_ref)) * f32(z_ref)
    y = y * lax.rsqrt(jnp.mean(y * y, axis=-1, keepdims=True) + NORM_EPS) * sng_ref[...]
    m = f32(gt0_ref) * _dot(y.astype(BF16), wbr_ref[0])
    y = (f32(hl0_ref) + f32(hl1_ref)) * f32(gl_ref)
    m = m + f32(gt1_ref) * _dot(y.astype(BF16), wbr_ref[1])
    parts = []
    for hh in range(d // dv):
        vs = slice(hh * dv, (hh + 1) * dv)
        o = f32(og0_ref, vs) + f32(og1_ref, vs)
        o = o * lax.rsqrt(jnp.mean(o * o, axis=-1, keepdims=True) + NORM_EPS) * gng_ref[...]
        parts.append(o * f32(gg_ref, vs))
    y = jnp.concatenate(parts, axis=1)
    m = m + f32(gt2_ref) * _dot(y.astype(BF16), wbr_ref[2])
    out = _dot(m.astype(BF16), wout_ref[...])
    z = alpha * h_ref[...] + m_ref[5:6, :] * out
    o_ref[...] = _layer_norm(z, lng_ref[...], lnb_ref[...])


def _merge_call(h, mods, p, layout, y_ssd, h_lru, o_gla, d_skip_w, ssd_norm_g, gla_norm_g, w_branch, w_out,
                ln_g, ln_b, *, n_ctx_blocks, skip_blocks, alpha):
    n_b, tt, d = h.shape
    n_blocks = tt // TOKEN_BLOCK - skip_blocks
    ctx_row = mods.shape[0] - 1
    dv = gla_norm_g.shape[-1]

    def mod_map(b, t):
        return (jnp.where(t + skip_blocks < n_ctx_blocks, ctx_row, b), 0, 0)

    def tok(width, blk):
        return pl.BlockSpec((None, TOKEN_BLOCK, width), lambda b, t: (b, t + skip_blocks, blk))

    def col_spec(name):
        off, w = layout[name]
        blk = off // w
        assert blk * w == off
        return tok(w, blk)

    def gate_spec(n):
        off = layout["gates"][0] + n * d
        assert off % d == 0
        return tok(d, off // d)

    def dir_spec(rv):
        return pl.BlockSpec((None, None, TOKEN_BLOCK, d), lambda b, t: (rv, b, t + skip_blocks, 0))

    kern = functools.partial(_merge_kernel, alpha=alpha, dv=dv)
    return pl.pallas_call(
        kern,
        out_shape=jax.ShapeDtypeStruct((n_b, n_blocks * TOKEN_BLOCK, d), F32),
        grid=(n_b, n_blocks),
        in_specs=[
            tok(d, 0),
            pl.BlockSpec((None, N_MOD, d), mod_map),
            dir_spec(0), dir_spec(1), col_spec("ssd_x"), col_spec("ssd_z"),
            dir_spec(0), dir_spec(1), col_spec("lru_g"),
            dir_spec(0), dir_spec(1), col_spec("gla_g"),
            gate_spec(0), gate_spec(1), gate_spec(2),
            _resident((1, d), lambda b, t: (0, 0)),
            _resident((1, d), lambda b, t: (0, 0)),
            _resident((1, dv), lambda b, t: (0, 0)),
            _resident((N_BRANCH, d, d), lambda b, t: (0, 0, 0)),
            _resident((d, d), lambda b, t: (0, 0)),
            _resident((1, d), lambda b, t: (0, 0)),
            _resident((1, d), lambda b, t: (0, 0)),
        ],
        out_specs=pl.BlockSpec((None, TOKEN_BLOCK, d), lambda b, t: (b, t, 0)),
        compiler_params=_cparams(2),
        name="mixer_merge",
    )(h, mods, y_ssd, y_ssd, p, p, h_lru, h_lru, p, o_gla, o_gla, p, p, p, p,
      d_skip_w, ssd_norm_g.reshape(1, d), gla_norm_g.reshape(1, dv), w_branch, w_out,
      ln_g.reshape(1, d), ln_b.reshape(1, d))


def _permute_w_in(w_in_l, d, ssd_heads, layout, total):
    ssd_w = d
    bc_w = SSD_GROUPS * SSD_STATE
    gla_key = d // 2
    sizes = (ssd_w, ssd_w + 2 * bc_w, 2 * ssd_heads, d, d, gla_key, gla_key, d, d, 2 * GLA_GATE_RANK, N_BRANCH * d)
    offs = [0]
    for s in sizes:
        offs.append(offs[-1] + s)
    w_in_l = w_in_l.astype(BF16)
    z, xbc, dtr, lx, lg, gq, gk, gv, gg, alr, gates = [w_in_l[:, offs[n]:offs[n + 1]] for n in range(len(sizes))]
    small = jnp.concatenate([dtr, alr, jnp.zeros((d, LANES - dtr.shape[1] - alr.shape[1]), w_in_l.dtype)], axis=1)
    w = jnp.concatenate([xbc, lx, z, lg, gq, gk, gv, gg, gates, small], axis=1)
    assert w.shape[1] == total
    return w.astype(BF16)


def _lru_gate_tiles(w_a, w_x):
    per = MXU_TILE // LRU_BLOCK
    n_dir, nb, k, _ = w_a.shape
    n_tiles = nb // per

    def tiles(w):
        w = w.reshape(n_dir, n_tiles, per, k, k)
        eye = jnp.eye(per, dtype=w.dtype)
        t = jnp.einsum("dtpij,pq->dtpiqj", w, eye)
        return t.reshape(n_dir, n_tiles, MXU_TILE, MXU_TILE)

    return jnp.concatenate([tiles(w_a), tiles(w_x)], axis=-1).astype(BF16)


def _to_colmajor(h, n_ctx, rows):
    n_b, _, d = h.shape
    lat = h[:, n_ctx:].reshape(n_b, rows, GRID_W, d).transpose(0, 2, 1, 3).reshape(n_b, rows * GRID_W, d)
    return jnp.concatenate([h[:, :n_ctx], lat], axis=1)


def _to_raster(h, n_ctx, rows):
    n_b, _, d = h.shape
    lat = h[:, n_ctx:].reshape(n_b, GRID_W, rows, d).transpose(0, 2, 1, 3).reshape(n_b, rows * GRID_W, d)
    return jnp.concatenate([h[:, :n_ctx], lat], axis=1)


def kernel(x, c, ctx, c_ctx, w_ada, b_ada, ln_g, ln_b, ffn_w_up, ffn_w_down, w_in, ssd_conv_w, ssd_conv_b,
           ssd_dt_bias, ssd_a_log, ssd_d, ssd_norm_g, lru_conv_w, lru_conv_b, lru_w_a, lru_b_a, lru_w_x, lru_b_x,
           lru_lam, gla_w_gate, gla_b_gate, gla_norm_g, w_branch, w_out):
    n_b, t_lat, d = x.shape
    n_ctx = ctx.shape[1]
    depth = w_ada.shape[0]
    rows = t_lat // GRID_W
    ssd_heads = ssd_dt_bias.shape[-1]
    assert n_ctx % TOKEN_BLOCK == 0 and t_lat % TOKEN_BLOCK == 0
    for line in (n_ctx, GRID_W, rows):
        assert line & (line - 1) == 0 and TOKEN_BLOCK % min(line, TOKEN_BLOCK) == 0
    assert n_ctx <= TOKEN_BLOCK
    alpha = (2.0 * depth) ** 0.25
    n_ctx_blocks = n_ctx // TOKEN_BLOCK
    n_ctx_chunks = n_ctx // SCAN_CHUNK

    layout, total = _proj_layout(d, ssd_heads)
    alr_off = 2 * ssd_heads

    n_rows = -(-(n_b + 1) // SUBLANES) * SUBLANES
    cond = jnp.concatenate([c, jnp.zeros((n_rows - n_b - 1, d), F32), c_ctx[None, :]], axis=0)
    mods_all = _ada_call(cond, w_ada, b_ada).reshape(depth, n_rows, N_MOD, d)

    expand = jnp.repeat(jnp.eye(ssd_heads, dtype=BF16), SSD_HEAD_DIM, axis=1)
    expand = jnp.concatenate([expand] * 3, axis=0)
    h = jnp.concatenate([ctx, x], axis=1)
    for l in range(depth):
        last = l == depth - 1
        col_major = l % 2 == 1
        mods = mods_all[l]
        wup = ffn_w_up[l].astype(BF16)
        wdn = ffn_w_down[l].astype(BF16)

        if col_major:
            h = _to_colmajor(h, n_ctx, rows)
        h = _ffn_call(h, mods, wup[0], wdn[0], ln_g[l, 0], ln_b[l, 0], j=0, n_ctx_blocks=n_ctx_blocks, alpha=alpha)

        w_perm = _permute_w_in(w_in[l], d, ssd_heads, layout, total)
        conv_w = jnp.concatenate([ssd_conv_w[l], lru_conv_w[l]], axis=1)
        conv_b = jnp.concatenate([ssd_conv_b[l], lru_conv_b[l]], axis=0)[None, :]
        p, small = _proj_call(h, mods, w_perm, conv_w, conv_b, layout, n_ctx_blocks=n_ctx_blocks, line_ctx=n_ctx,
                              line_lat=rows if col_major else GRID_W)

        y_ssd = _ssd_call(p, small, layout, ssd_dt_bias[l], ssd_a_log[l], expand, n_ctx_chunks=n_ctx_chunks)
        h_lru = _lru_call(p, layout, _lru_gate_tiles(lru_w_a[l], lru_w_x[l]), lru_b_a[l][:, None, :],
                          lru_b_x[l][:, None, :], lru_lam[l][:, None, :], n_ctx_blocks=n_ctx_blocks)
        o_gla = _gla_call(p, small, layout, gla_w_gate[l], gla_b_gate[l][:, None, :], alr_off=alr_off,
                          n_ctx_chunks=n_ctx_chunks)

        d_skip_w = jnp.repeat(ssd_d[l, 0] + ssd_d[l, 1], SSD_HEAD_DIM)[None, :]
        skip_blocks = n_ctx_blocks if last else 0
        h = _merge_call(h, mods, p, layout, y_ssd, h_lru, o_gla, d_skip_w, ssd_norm_g[l], gla_norm_g[l],
                        w_branch[l].astype(BF16), w_out[l].astype(BF16), ln_g[l, 1], ln_b[l, 1],
                        n_ctx_blocks=n_ctx_blocks, skip_blocks=skip_blocks, alpha=alpha)
        n_ctx_now = 0 if last else n_ctx
        h = _ffn_call(h, mods, wup[1], wdn[1], ln_g[l, 2], ln_b[l, 2], j=2,
                      n_ctx_blocks=n_ctx_now // TOKEN_BLOCK, alpha=alpha)
        if col_major:
            h = _to_raster(h, n_ctx_now, rows)
    return h
```

```python
import functools
import math

import jax
import jax.numpy as jnp
from jax import lax
from jax.experimental import pallas as pl
from jax.experimental.pallas import tpu as pltpu

F32 = jnp.float32
BF16 = jnp.bfloat16

GRID_W = 64
CONV_W = 4
CONV_LEFT = 2
SSD_HEAD_DIM = 64
SSD_GROUPS = 4
SSD_STATE = 128
LRU_BLOCK = 64
LRU_C = 8.0
GLA_HEADS = 4
GLA_GATE_RANK = 16
GLA_TAU = 16.0
N_MOD = 9
N_BRANCH = 3
FFN_RES_W = 0.5
NORM_EPS = 1e-5

TOKEN_BLOCK = 256
SCAN_CHUNK = 128
SCAN_BATCH = 2
MXU_TILE = 256
LANES = 128
SUBLANES = 8
VMEM_LIMIT = 56 * 1024 * 1024


def _cparams(n_axes):
    return pltpu.CompilerParams(dimension_semantics=("arbitrary",) * n_axes,
                                vmem_limit_bytes=VMEM_LIMIT)


def _resident(block_shape, index_map):
    return pl.BlockSpec(block_shape, index_map, pipeline_mode=pl.Buffered(1))


def _stacked(arr, lead):
    n = len(lead)
    tail = tuple(arr.shape[n:])
    index = tuple(lead) + (0,) * len(tail)
    return pl.BlockSpec((None,) * n + tail, lambda *_: index, pipeline_mode=pl.Buffered(1))


def _sigmoid(x):
    return 0.5 * jnp.tanh(0.5 * x) + 0.5


def _silu(x):
    return x * _sigmoid(x)


def _softplus(x):
    return jnp.maximum(x, 0.0) + jnp.log1p(jnp.exp(-jnp.abs(x)))


def _log_sigmoid(x):
    return -_softplus(-x)


def _split_terms(x, n):
    terms = []
    for _ in range(n - 1):
        t = x.astype(BF16)
        terms.append(t)
        x = x - t.astype(F32)
    terms.append(x.astype(BF16))
    return terms


_dot = functools.partial(jnp.dot, preferred_element_type=F32)


def _exact_lhs_dot(t_bf16, x, n_terms):
    return _dot(jnp.concatenate([t_bf16] * n_terms, axis=1), jnp.concatenate(_split_terms(x, n_terms), axis=0))


def _exact_rhs_dot(x, e3_bf16):
    return _dot(jnp.concatenate(_split_terms(x, 3), axis=1), e3_bf16)


def _f32_dot_small_k(a, w):
    at = _split_terms(a, 3)
    wt = _split_terms(w, 3)
    pairs = [(0, 0), (0, 1), (0, 2), (1, 0), (1, 1), (2, 0)]
    lhs = jnp.concatenate([at[i] for i, _ in pairs], axis=1)
    rhs = jnp.concatenate([wt[j] for _, j in pairs], axis=0)
    return _dot(lhs, rhs)


def _layer_norm(z, g, b):
    mu = jnp.mean(z, axis=-1, keepdims=True)
    zc = z - mu
    var = jnp.mean(zc * zc, axis=-1, keepdims=True)
    return zc * lax.rsqrt(var + NORM_EPS) * g + b


def _scan_order(i, rev, n_ctx, n_tot):
    back = jnp.where(i < n_ctx, n_ctx - 1 - i, n_ctx + n_tot - 1 - i)
    return jnp.where(rev == 0, i, back)


def _ada_kernel(s_ref, w_ref, b_ref, o_ref):
    s = _silu(s_ref[...])
    o_ref[...] = jnp.dot(s, w_ref[...], preferred_element_type=F32,
                         precision=lax.Precision.HIGHEST) + b_ref[...]


def _ada_call(cond, w_ada, b_ada):
    n_layers, d, nd = w_ada.shape
    rows = cond.shape[0]
    n_col = nd // d
    return pl.pallas_call(
        _ada_kernel,
        out_shape=jax.ShapeDtypeStruct((n_layers, rows, nd), F32),
        grid=(n_layers, n_col),
        in_specs=[
            pl.BlockSpec((rows, d), lambda l, j: (0, 0)),
            pl.BlockSpec((None, d, d), lambda l, j: (l, 0, j)),
            pl.BlockSpec((None, 1, d), lambda l, j: (l, 0, j)),
        ],
        out_specs=pl.BlockSpec((None, rows, d), lambda l, j: (l, 0, j)),
        compiler_params=_cparams(2),
        name="ada_mod",
    )(cond, w_ada, b_ada.reshape(n_layers, 1, nd))


def _ffn_kernel(x_ref, m_ref, wup_ref, wdn_ref, g_ref, b_ref, o_ref, *, j, alpha, d_ff, ff_chunk):
    x = x_ref[...]
    shift = m_ref[3 * j:3 * j + 1, :]
    scale = m_ref[3 * j + 1:3 * j + 2, :]
    gate = m_ref[3 * j + 2:3 * j + 3, :]
    u = (x * (1.0 + scale) + shift).astype(BF16)
    acc = None
    for c0 in range(0, d_ff, ff_chunk):
        a = _dot(u, wup_ref[:, c0:c0 + ff_chunk])
        v = _dot(u, wup_ref[:, d_ff + c0:d_ff + c0 + ff_chunk])
        hid = (_silu(a) * v).astype(BF16)
        part = _dot(hid, wdn_ref[c0:c0 + ff_chunk, :])
        acc = part if acc is None else acc + part
    z = alpha * x + (FFN_RES_W * gate) * acc
    o_ref[...] = _layer_norm(z, g_ref[...], b_ref[...])


def _ffn_call(h, mods, w_up, w_dn, lead, ln_g, ln_b, *, j, n_ctx_blocks, alpha):
    n_b, tt, d = h.shape
    d_ff = w_dn.shape[-2]
    ff_chunk = d_ff // 2 if (d_ff // 2) % LANES == 0 else d_ff
    ctx_row = mods.shape[0] - 1
    n_blocks = tt // TOKEN_BLOCK

    def mod_map(b, t):
        return (jnp.where(t < n_ctx_blocks, ctx_row, b), 0, 0)

    kern = functools.partial(_ffn_kernel, j=j, alpha=alpha, d_ff=d_ff, ff_chunk=ff_chunk)
    return pl.pallas_call(
        kern,
        out_shape=jax.ShapeDtypeStruct(h.shape, F32),
        grid=(n_b, n_blocks),
        in_specs=[
            pl.BlockSpec((None, TOKEN_BLOCK, d), lambda b, t: (b, t, 0)),
            pl.BlockSpec((None, N_MOD, d), mod_map),
            _stacked(w_up, lead),
            _stacked(w_dn, lead),
            _resident((1, d), lambda b, t: (0, 0)),
            _resident((1, d), lambda b, t: (0, 0)),
        ],
        out_specs=pl.BlockSpec((None, TOKEN_BLOCK, d), lambda b, t: (b, t, 0)),
        compiler_params=_cparams(2),
        name="ffn_sublayer",
    )(h, mods, w_up, w_dn, ln_g.reshape(1, d), ln_b.reshape(1, d))


def _proj_layout(d, ssd_heads):
    ssd_w = d
    bc_w = SSD_GROUPS * SSD_STATE
    gla_key = d // 2
    names = [("ssd_x", ssd_w), ("ssd_b", bc_w), ("ssd_c", bc_w), ("lru_x", d), ("ssd_z", ssd_w),
             ("lru_g", d), ("gla_q", gla_key), ("gla_k", gla_key), ("gla_v", d), ("gla_g", d),
             ("gates", N_BRANCH * d), ("small", LANES)]
    off, out = 0, {}
    for name, width in names:
        out[name] = (off, width)
        off += width
    return out, off


_ACTIVATIONS = {"none": lambda v: v, "silu": _silu, "sigmoid": _sigmoid, "gelu": jax.nn.gelu}


def _proj_kernel(x_ref, m_ref, w_ref, cw_ref, cb_ref, o_ref, sm_ref, ybuf, *, segments, main_cols,
                 n_ctx_blocks, line_ctx, line_lat):
    x = x_ref[...]
    u = (x * (1.0 + m_ref[4:5, :]) + m_ref[3:4, :]).astype(BF16)
    t = pl.program_id(1)
    line = jnp.where(t < n_ctx_blocks, line_ctx, line_lat)
    rows = x.shape[0]
    halo = SUBLANES
    sm_ref[...] = _dot(u, w_ref[:, main_cols:main_cols + LANES])
    pos = lax.broadcasted_iota(jnp.int32, (rows, LANES), 0) & (line - 1)
    taps = [(k, k - CONV_LEFT) for k in range(CONV_W) if k != CONV_LEFT]
    valid = {off: (pos + off >= 0) & (pos + off < line) for _, off in taps}
    zeros = jnp.zeros((halo, LANES), F32)
    for c0, cw, conv, act in segments:
        y = _dot(u, w_ref[:, c0:c0 + cw])
        if not conv:
            o_ref[:, c0:c0 + cw] = _ACTIVATIONS[act](y).astype(o_ref.dtype)
            continue
        for j in range(cw // LANES):
            ybuf[j, 0:halo, :] = zeros
            ybuf[j, halo:halo + rows, :] = y[:, j * LANES:(j + 1) * LANES]
            ybuf[j, halo + rows:2 * halo + rows, :] = zeros
        for j in range(cw // LANES):
            cols = slice(c0 + j * LANES, c0 + (j + 1) * LANES)
            acc = cb_ref[:, cols] + ybuf[j, halo:halo + rows, :] * cw_ref[CONV_LEFT:CONV_LEFT + 1, cols]
            for k, off in taps:
                shifted = ybuf[j, halo + off:halo + off + rows, :]
                acc = acc + jnp.where(valid[off], shifted, 0.0) * cw_ref[k:k + 1, cols]
            o_ref[:, cols] = _ACTIVATIONS[act](acc).astype(o_ref.dtype)


def _proj_segments(layout, col_chunk):
    kinds = {"ssd_x": (True, "silu"), "ssd_b": (True, "silu"), "ssd_c": (True, "silu"), "lru_x": (True, "none"),
             "ssd_z": (False, "silu"), "lru_g": (False, "gelu"), "gla_q": (False, "none"), "gla_k": (False, "none"),
             "gla_v": (False, "none"), "gla_g": (False, "silu"), "gates": (False, "sigmoid")}
    segs = []
    for name, (conv, act) in kinds.items():
        off, width = layout[name]
        for c0 in range(off, off + width, col_chunk):
            segs.append((c0, min(col_chunk, off + width - c0), conv, act))
    return tuple(segs)


def _proj_call(h, mods, w_perm, layer, conv_w, conv_b, layout, *, n_ctx_blocks, line_ctx, line_lat):
    n_b, tt, d = h.shape
    total = w_perm.shape[-1]
    conv_cols = conv_w.shape[1]
    ctx_row = mods.shape[0] - 1
    n_blocks = tt // TOKEN_BLOCK
    col_chunk = 1024

    def mod_map(b, t):
        return (jnp.where(t < n_ctx_blocks, ctx_row, b), 0, 0)

    main = total - LANES
    kern = functools.partial(_proj_kernel, segments=_proj_segments(layout, col_chunk), main_cols=main,
                             n_ctx_blocks=n_ctx_blocks, line_ctx=line_ctx, line_lat=line_lat)
    return pl.pallas_call(
        kern,
        out_shape=(jax.ShapeDtypeStruct((n_b, tt, main), BF16), jax.ShapeDtypeStruct((n_b, tt, LANES), F32)),
        grid=(n_b, n_blocks),
        in_specs=[
            pl.BlockSpec((None, TOKEN_BLOCK, d), lambda b, t: (b, t, 0)),
            pl.BlockSpec((None, N_MOD, d), mod_map),
            _stacked(w_perm, (layer,)),
            _resident((CONV_W, conv_cols), lambda b, t: (0, 0)),
            _resident((1, conv_cols), lambda b, t: (0, 0)),
        ],
        out_specs=(pl.BlockSpec((None, TOKEN_BLOCK, main), lambda b, t: (b, t, 0)),
                   pl.BlockSpec((None, TOKEN_BLOCK, LANES), lambda b, t: (b, t, 0))),
        scratch_shapes=[pltpu.VMEM((col_chunk // LANES, TOKEN_BLOCK + 2 * SUBLANES, LANES), F32)],
        compiler_params=_cparams(2),
        name="mixer_in_proj",
    )(h, mods, w_perm, conv_w, conv_b)


def _ssd_chunk(fwd, x_ref, b_ref, c_ref, sm_ref, dtb_ref, alog_ref, exp_ref, y_ref, state_ref, n_heads):
    q = x_ref.shape[0]
    hg = n_heads // SSD_GROUPS
    gw = hg * SSD_HEAD_DIM
    sm = sm_ref[...]
    dt_raw = jnp.where(fwd, sm[:, 0:n_heads], sm[:, n_heads:2 * n_heads])
    dt_bias = jnp.where(fwd, dtb_ref[0:1, :], dtb_ref[1:2, :])
    a_log = jnp.where(fwd, alog_ref[0:1, :], alog_ref[1:2, :])
    dt = _softplus(dt_raw + dt_bias)
    a = dt * (-jnp.exp(a_log))
    row = lax.broadcasted_iota(jnp.int32, (q, q), 0)
    col = lax.broadcasted_iota(jnp.int32, (q, q), 1)
    mask = (col - row) * jnp.where(fwd, 1, -1) <= 0
    tri = jnp.where(mask, 1.0, 0.0).astype(BF16)
    cs = _exact_lhs_dot(tri, a, 3)
    tot = jnp.sum(a, axis=0, keepdims=True)
    pad = jnp.zeros((q, LANES - 2 * n_heads), F32)
    both_t = jnp.concatenate([cs, dt, pad], axis=1).T
    cs_t = both_t[0:n_heads]
    dt_t = both_t[n_heads:2 * n_heads]

    w_state = dt * jnp.exp(tot - cs)
    from_start = jnp.exp(cs)
    e_tot = jnp.broadcast_to(jnp.exp(tot), (SUBLANES, n_heads))
    wide = _exact_rhs_dot(jnp.concatenate([w_state, from_start, e_tot], axis=0), exp_ref[...])
    w_state_w = wide[0:q]
    from_start_w = wide[q:2 * q]
    e_tot_w = wide[2 * q:2 * q + 1]

    x_bf = x_ref[...]
    xw_bf = (x_bf.astype(F32) * w_state_w).astype(BF16)
    for g in range(SSD_GROUPS):
        bt = b_ref[:, g * SSD_STATE:(g + 1) * SSD_STATE].T
        cg = c_ref[:, g * SSD_STATE:(g + 1) * SSD_STATE]
        scores = _dot(cg, bt)
        s_prev = state_ref[g]
        y_off = _dot(cg, s_prev.astype(BF16)) * from_start_w[:, g * gw:(g + 1) * gw]
        parts = []
        for hh in range(hg):
            head = g * hg + hh
            seg = jnp.where(mask, cs[:, head:head + 1] - cs_t[head:head + 1, :], -1e30)
            m = scores * jnp.exp(seg) * dt_t[head:head + 1, :]
            parts.append(_dot(m.astype(BF16), x_bf[:, head * SSD_HEAD_DIM:(head + 1) * SSD_HEAD_DIM]))
        y_ref[:, g * gw:(g + 1) * gw] = (y_off + jnp.concatenate(parts, axis=1)).astype(y_ref.dtype)
        upd = _dot(bt, xw_bf[:, g * gw:(g + 1) * gw])
        state_ref[g] = s_prev * e_tot_w[:, g * gw:(g + 1) * gw] + upd


def _scan_col_spec(layout, name, nb, rows, n_ctx, n_tot):
    off, w = layout[name]
    assert off % w == 0
    blk = off // w
    return pl.BlockSpec((nb, rows, w), lambda b, r, i: (b, _scan_order(i, r, n_ctx, n_tot), blk))


def _scan_small_spec(nb, rows, n_ctx, n_tot):
    return pl.BlockSpec((nb, rows, LANES), lambda b, r, i: (b, _scan_order(i, r, n_ctx, n_tot), 0))


def _scan_out_spec(nb, rows, width, n_ctx, n_tot):
    return pl.BlockSpec((None, nb, rows, width), lambda b, r, i: (r, b, _scan_order(i, r, n_ctx, n_tot), 0))


def _lru_scan_block(a_ref, b_ref, o_ref, carry_ref, reverse):
    rows, width = a_ref.shape
    sub = lax.broadcasted_iota(jnp.int32, (SUBLANES, width), 0)
    carry = carry_ref[...]
    n_groups = rows // SUBLANES
    order = range(n_groups - 1, -1, -1) if reverse else range(n_groups)
    for gi in order:
        sl = slice(gi * SUBLANES, (gi + 1) * SUBLANES)
        a = a_ref[sl, :]
        b = b_ref[sl, :]
        k = 1
        while k < SUBLANES:
            if reverse:
                a_sh = pltpu.roll(a, SUBLANES - k, axis=0)
                b_sh = pltpu.roll(b, SUBLANES - k, axis=0)
                ok = sub < SUBLANES - k
            else:
                a_sh = pltpu.roll(a, k, axis=0)
                b_sh = pltpu.roll(b, k, axis=0)
                ok = sub >= k
            b = b + a * jnp.where(ok, b_sh, 0.0)
            a = a * jnp.where(ok, a_sh, 1.0)
            k *= 2
        hblk = b + a * carry
        b_ref[sl, :] = hblk
        carry = hblk[0:1, :] if reverse else hblk[SUBLANES - 1:SUBLANES, :]
    carry_ref[...] = carry
    o_ref[...] = b_ref[...].astype(o_ref.dtype)


def _lru_kernel(x_ref, w_ref, ba_ref, bx_ref, lam_ref, h_ref, a_s, b_s, carry_ref):
    rev = pl.program_id(1)
    i = pl.program_id(2)
    nb = x_ref.shape[0]

    @pl.when(i == 0)
    def _():
        carry_ref[...] = jnp.zeros_like(carry_ref)

    n_tiles = w_ref.shape[0]
    neg_c_sp = (-LRU_C) * _softplus(-lam_ref[...])
    for bi in range(nb):
        x_bf = x_ref[bi]
        pre_a, pre_x = [], []
        for t in range(n_tiles):
            y = _dot(x_bf[:, t * MXU_TILE:(t + 1) * MXU_TILE], w_ref[t])
            pre_a.append(y[:, :MXU_TILE])
            pre_x.append(y[:, MXU_TILE:])
        r = _sigmoid(jnp.concatenate(pre_a, axis=1) + ba_ref[...])
        ig = _sigmoid(jnp.concatenate(pre_x, axis=1) + bx_ref[...])
        log_a = r * neg_c_sp
        a = jnp.exp(log_a)
        a_s[bi] = a
        b_s[bi] = jnp.sqrt(-jnp.tanh(log_a) * (a * a + 1.0)) * (ig * x_bf.astype(F32))

    @pl.when(rev == 0)
    def _():
        for bi in range(nb):
            _lru_scan_block(a_s.at[bi], b_s.at[bi], h_ref.at[bi], carry_ref.at[bi], reverse=False)

    @pl.when(rev == 1)
    def _():
        for bi in range(nb):
            _lru_scan_block(a_s.at[bi], b_s.at[bi], h_ref.at[bi], carry_ref.at[bi], reverse=True)


def _lru_call(p, layout, w_tiles, layer, b_a, b_x, lam, *, n_ctx_blocks):
    n_b, tt, _ = p.shape
    width = layout["lru_x"][1]
    n_blocks = tt // TOKEN_BLOCK
    n_tiles = w_tiles.shape[2]
    nb = SCAN_BATCH if n_b % SCAN_BATCH == 0 else 1

    return pl.pallas_call(
        _lru_kernel,
        out_shape=jax.ShapeDtypeStruct((2, n_b, tt, width), BF16),
        grid=(n_b // nb, 2, n_blocks),
        in_specs=[
            _scan_col_spec(layout, "lru_x", nb, TOKEN_BLOCK, n_ctx_blocks, n_blocks),
            pl.BlockSpec((None, None, n_tiles, MXU_TILE, 2 * MXU_TILE), lambda b, r, i: (layer, r, 0, 0, 0)),
            pl.BlockSpec((None, 1, width), lambda b, r, i: (r, 0, 0)),
            pl.BlockSpec((None, 1, width), lambda b, r, i: (r, 0, 0)),
            pl.BlockSpec((None, 1, width), lambda b, r, i: (r, 0, 0)),
        ],
        out_specs=_scan_out_spec(nb, TOKEN_BLOCK, width, n_ctx_blocks, n_blocks),
        scratch_shapes=[pltpu.VMEM((nb, TOKEN_BLOCK, width), F32), pltpu.VMEM((nb, TOKEN_BLOCK, width), F32),
                        pltpu.VMEM((nb, 1, width), F32)],
        compiler_params=_cparams(3),
        name="rglru_scan",
    )(p, w_tiles, b_a, b_x, lam)


def _gla_chunk(fwd, q_ref, k_ref, v_ref, sm_ref, wg_ref, bg_ref, o_ref, state_ref, alr_off, dk, dv):
    q_len = q_ref.shape[0]
    n_heads = q_ref.shape[1] // dk
    sm = sm_ref[...]
    r = GLA_GATE_RANK
    a_lr = jnp.where(fwd, sm[:, alr_off:alr_off + r], sm[:, alr_off + r:alr_off + 2 * r])
    z = _f32_dot_small_k(a_lr, wg_ref[...]) + bg_ref[...]
    log_a = _log_sigmoid(z) * (1.0 / GLA_TAU)
    row = lax.broadcasted_iota(jnp.int32, (q_len, q_len), 0)
    col = lax.broadcasted_iota(jnp.int32, (q_len, q_len), 1)
    mask = (col - row) * jnp.where(fwd, 1, -1) <= 0
    tri = jnp.where(mask, 1.0, 0.0).astype(BF16)
    bcum = _exact_lhs_dot(tri, log_a, 2)
    btot = jnp.sum(log_a, axis=0, keepdims=True)
    bmid = bcum[q_len // 2:q_len // 2 + 1, :]
    q_in = q_ref[...].astype(F32) * (dk ** -0.5) * jnp.exp(bcum - bmid)
    k_in = k_ref[...].astype(F32) * jnp.exp(bmid - bcum)
    q_off = (q_in * jnp.exp(bmid)).astype(BF16)
    k_st = (k_in * jnp.exp(btot - bmid)).astype(BF16)
    q_in = q_in.astype(BF16)
    k_in = k_in.astype(BF16)
    e_tot = jnp.exp(btot)
    for hh in range(n_heads):
        ks = slice(hh * dk, (hh + 1) * dk)
        vs = slice(hh * dv, (hh + 1) * dv)
        v_bf = v_ref[:, vs]
        att = lax.dot_general(q_in[:, ks], k_in[:, ks], (((1,), (1,)), ((), ())), preferred_element_type=F32)
        att = jnp.where(mask, att, 0.0).astype(BF16)
        s_prev = state_ref[hh]
        inter = lax.dot_general(q_off[:, ks], s_prev.astype(BF16), (((1,), (1,)), ((), ())),
                                preferred_element_type=F32)
        o_ref[:, vs] = (_dot(att, v_bf) + inter).astype(o_ref.dtype)
        upd = lax.dot_general(v_bf, k_st[:, ks], (((0,), (0,)), ((), ())), preferred_element_type=F32)
        state_ref[hh] = s_prev * e_tot[:, ks] + upd


def _chunk_scans_kernel(x_ref, b_ref, c_ref, q_ref, k_ref, v_ref, sm_ref, dtb_ref, alog_ref, exp_ref, wg_ref, bg_ref,
                        y_ref, o_ref, ssd_state, gla_state, *, n_heads, alr_off, dk, dv):
    rev = pl.program_id(1)
    i = pl.program_id(2)

    @pl.when(i == 0)
    def _():
        ssd_state[...] = jnp.zeros_like(ssd_state)
        gla_state[...] = jnp.zeros_like(gla_state)

    for bi in range(x_ref.shape[0]):
        _ssd_chunk(rev == 0, x_ref.at[bi], b_ref.at[bi], c_ref.at[bi], sm_ref.at[bi], dtb_ref, alog_ref, exp_ref,
                   y_ref.at[bi], ssd_state.at[bi], n_heads)
        _gla_chunk(rev == 0, q_ref.at[bi], k_ref.at[bi], v_ref.at[bi], sm_ref.at[bi], wg_ref, bg_ref,
                   o_ref.at[bi], gla_state.at[bi], alr_off, dk, dv)


def _chunk_scans_call(p, small, layout, dt_bias, a_log, expand, w_gate, b_gate, *, alr_off, n_ctx_chunks):
    n_b, tt, _ = p.shape
    n_heads = dt_bias.shape[1]
    ssd_w = n_heads * SSD_HEAD_DIM
    hg = n_heads // SSD_GROUPS
    key_w = layout["gla_q"][1]
    val_w = layout["gla_v"][1]
    dk = key_w // GLA_HEADS
    dv = val_w // GLA_HEADS
    n_chunks = tt // SCAN_CHUNK
    nb = SCAN_BATCH if n_b % SCAN_BATCH == 0 else 1
    col_spec = functools.partial(_scan_col_spec, layout, nb=nb, rows=SCAN_CHUNK, n_ctx=n_ctx_chunks, n_tot=n_chunks)

    kern = functools.partial(_chunk_scans_kernel, n_heads=n_heads, alr_off=alr_off, dk=dk, dv=dv)
    return pl.pallas_call(
        kern,
        out_shape=(jax.ShapeDtypeStruct((2, n_b, tt, ssd_w), BF16), jax.ShapeDtypeStruct((2, n_b, tt, val_w), BF16)),
        grid=(n_b // nb, 2, n_chunks),
        in_specs=[
            col_spec("ssd_x"), col_spec("ssd_b"), col_spec("ssd_c"),
            col_spec("gla_q"), col_spec("gla_k"), col_spec("gla_v"),
            _scan_small_spec(nb, SCAN_CHUNK, n_ctx_chunks, n_chunks),
            _resident((2, n_heads), lambda b, r, i: (0, 0)),
            _resident((2, n_heads), lambda b, r, i: (0, 0)),
            _resident(expand.shape, lambda b, r, i: (0, 0)),
            pl.BlockSpec((None, GLA_GATE_RANK, key_w), lambda b, r, i: (r, 0, 0)),
            pl.BlockSpec((None, 1, key_w), lambda b, r, i: (r, 0, 0)),
        ],
        out_specs=(_scan_out_spec(nb, SCAN_CHUNK, ssd_w, n_ctx_chunks, n_chunks),
                   _scan_out_spec(nb, SCAN_CHUNK, val_w, n_ctx_chunks, n_chunks)),
        scratch_shapes=[pltpu.VMEM((nb, SSD_GROUPS, SSD_STATE, hg * SSD_HEAD_DIM), F32),
                        pltpu.VMEM((nb, GLA_HEADS, dv, dk), F32)],
        compiler_params=_cparams(3),
        name="ssd_gla_scan",
    )(p, p, p, p, p, p, small, dt_bias, a_log, expand, w_gate, b_gate)


def _merge_kernel(h_ref, m_ref, ys0_ref, ys1_ref, xs_ref, z_ref, hl0_ref, hl1_ref, gl_ref,
                  og0_ref, og1_ref, gg_ref, gt0_ref, gt1_ref, gt2_ref, dskip_ref, sng_ref, gng_ref,
                  wbr_ref, wout_ref, lng_ref, lnb_ref, o_ref, *, alpha, dv):
    d = h_ref.shape[1]

    def f32(ref, cols=slice(None)):
        return ref[:, cols].astype(F32)

    y = (f32(ys0_ref) + f32(ys1_ref) + dskip_ref[...] * f32(xs_ref)) * f32(z_ref)
    y = y * lax.rsqrt(jnp.mean(y * y, axis=-1, keepdims=True) + NORM_EPS) * sng_ref[...]
    m = f32(gt0_ref) * _dot(y.astype(BF16), wbr_ref[0])
    y = (f32(hl0_ref) + f32(hl1_ref)) * f32(gl_ref)
    m = m + f32(gt1_ref) * _dot(y.astype(BF16), wbr_ref[1])
    parts = []
    for hh in range(d // dv):
        vs = slice(hh * dv, (hh + 1) * dv)
        o = f32(og0_ref, vs) + f32(og1_ref, vs)
        o = o * lax.rsqrt(jnp.mean(o * o, axis=-1, keepdims=True) + NORM_EPS) * gng_ref[...]
        parts.append(o * f32(gg_ref, vs))
    y = jnp.concatenate(parts, axis=1)
    m = m + f32(gt2_ref) * _dot(y.astype(BF16), wbr_ref[2])
    out = _dot(m.astype(BF16), wout_ref[...])
    z = alpha * h_ref[...] + m_ref[5:6, :] * out
    o_ref[...] = _layer_norm(z, lng_ref[...], lnb_ref[...])


def _merge_call(h, mods, p, layout, y_ssd, h_lru, o_gla, d_skip_w, ssd_norm_g, gla_norm_g, w_branch, w_out, layer,
                ln_g, ln_b, *, n_ctx_blocks, skip_blocks, alpha):
    n_b, tt, d = h.shape
    n_blocks = tt // TOKEN_BLOCK - skip_blocks
    ctx_row = mods.shape[0] - 1
    dv = gla_norm_g.shape[-1]

    def mod_map(b, t):
        return (jnp.where(t + skip_blocks < n_ctx_blocks, ctx_row, b), 0, 0)

    def tok(width, blk):
        return pl.BlockSpec((None, TOKEN_BLOCK, width), lambda b, t: (b, t + skip_blocks, blk))

    def col_spec(name):
        off, w = layout[name]
        blk = off // w
        assert blk * w == off
        return tok(w, blk)

    def gate_spec(n):
        off = layout["gates"][0] + n * d
        assert off % d == 0
        return tok(d, off // d)

    def dir_spec(rv):
        return pl.BlockSpec((None, None, TOKEN_BLOCK, d), lambda b, t: (rv, b, t + skip_blocks, 0))

    kern = functools.partial(_merge_kernel, alpha=alpha, dv=dv)
    return pl.pallas_call(
        kern,
        out_shape=jax.ShapeDtypeStruct((n_b, n_blocks * TOKEN_BLOCK, d), F32),
        grid=(n_b, n_blocks),
        in_specs=[
            tok(d, 0),
            pl.BlockSpec((None, N_MOD, d), mod_map),
            dir_spec(0), dir_spec(1), col_spec("ssd_x"), col_spec("ssd_z"),
            dir_spec(0), dir_spec(1), col_spec("lru_g"),
            dir_spec(0), dir_spec(1), col_spec("gla_g"),
            gate_spec(0), gate_spec(1), gate_spec(2),
            _resident((1, d), lambda b, t: (0, 0)),
            _resident((1, d), lambda b, t: (0, 0)),
            _resident((1, dv), lambda b, t: (0, 0)),
            _stacked(w_branch, (layer,)),
            _stacked(w_out, (layer,)),
            _resident((1, d), lambda b, t: (0, 0)),
            _resident((1, d), lambda b, t: (0, 0)),
        ],
        out_specs=pl.BlockSpec((None, TOKEN_BLOCK, d), lambda b, t: (b, t, 0)),
        compiler_params=_cparams(2),
        name="mixer_merge",
    )(h, mods, y_ssd, y_ssd, p, p, h_lru, h_lru, p, o_gla, o_gla, p, p, p, p,
      d_skip_w, ssd_norm_g.reshape(1, d), gla_norm_g.reshape(1, dv), w_branch, w_out,
      ln_g.reshape(1, d), ln_b.reshape(1, d))


def _permute_w_in(w_in, d, ssd_heads, total):
    ssd_w = d
    bc_w = SSD_GROUPS * SSD_STATE
    gla_key = d // 2
    sizes = (ssd_w, ssd_w + 2 * bc_w, 2 * ssd_heads, d, d, gla_key, gla_key, d, d, 2 * GLA_GATE_RANK, N_BRANCH * d)
    offs = [0]
    for s in sizes:
        offs.append(offs[-1] + s)
    w16 = w_in.astype(BF16)
    z, xbc, dtr, lx, lg, gq, gk, gv, gg, alr, gates = [w16[..., offs[n]:offs[n + 1]] for n in range(len(sizes))]
    pad = jnp.zeros(w16.shape[:-1] + (LANES - dtr.shape[-1] - alr.shape[-1],), BF16)
    w = jnp.concatenate([xbc, lx, z, lg, gq, gk, gv, gg, gates, dtr, alr, pad], axis=-1)
    assert w.shape[-1] == total
    return w


def _lru_gate_tiles(w_a, w_x):
    per = MXU_TILE // LRU_BLOCK
    n_l, n_dir, nb, k, _ = w_a.shape
    n_tiles = nb // per

    def tiles(w):
        w = w.astype(BF16).reshape(n_l, n_dir, n_tiles, per, k, k)
        on_diag = jnp.eye(per, dtype=bool)[:, None, :, None]
        t = jnp.where(on_diag, w[:, :, :, :, :, None, :], jnp.zeros((), BF16))
        return t.reshape(n_l, n_dir, n_tiles, MXU_TILE, MXU_TILE)

    return jnp.concatenate([tiles(w_a), tiles(w_x)], axis=-1)


def _to_colmajor(h, n_ctx, rows):
    n_b, _, d = h.shape
    lat = h[:, n_ctx:].reshape(n_b, rows, GRID_W, d).transpose(0, 2, 1, 3).reshape(n_b, rows * GRID_W, d)
    return jnp.concatenate([h[:, :n_ctx], lat], axis=1)


def _to_raster(h, n_ctx, rows):
    n_b, _, d = h.shape
    lat = h[:, n_ctx:].reshape(n_b, GRID_W, rows, d).transpose(0, 2, 1, 3).reshape(n_b, rows * GRID_W, d)
    return jnp.concatenate([h[:, :n_ctx], lat], axis=1)


def kernel(x, c, ctx, c_ctx, w_ada, b_ada, ln_g, ln_b, ffn_w_up, ffn_w_down, w_in, ssd_conv_w, ssd_conv_b,
           ssd_dt_bias, ssd_a_log, ssd_d, ssd_norm_g, lru_conv_w, lru_conv_b, lru_w_a, lru_b_a, lru_w_x, lru_b_x,
           lru_lam, gla_w_gate, gla_b_gate, gla_norm_g, w_branch, w_out):
    n_b, t_lat, d = x.shape
    n_ctx = ctx.shape[1]
    depth = w_ada.shape[0]
    rows = t_lat // GRID_W
    ssd_heads = ssd_dt_bias.shape[-1]
    assert n_ctx % TOKEN_BLOCK == 0 and t_lat % TOKEN_BLOCK == 0
    for line in (n_ctx, GRID_W, rows):
        assert line & (line - 1) == 0 and TOKEN_BLOCK % min(line, TOKEN_BLOCK) == 0
    assert n_ctx <= TOKEN_BLOCK
    alpha = (2.0 * depth) ** 0.25
    n_ctx_blocks = n_ctx // TOKEN_BLOCK
    n_ctx_chunks = n_ctx // SCAN_CHUNK

    layout, total = _proj_layout(d, ssd_heads)
    alr_off = 2 * ssd_heads

    n_rows = -(-(n_b + 1) // SUBLANES) * SUBLANES
    cond = jnp.concatenate([c, jnp.zeros((n_rows - n_b - 1, d), F32), c_ctx[None, :]], axis=0)
    mods_all = _ada_call(cond, w_ada, b_ada).reshape(depth, n_rows, N_MOD, d)

    expand = jnp.repeat(jnp.eye(ssd_heads, dtype=BF16), SSD_HEAD_DIM, axis=1)
    expand = jnp.concatenate([expand] * 3, axis=0)
    wup_all = ffn_w_up.astype(BF16)
    wdn_all = ffn_w_down.astype(BF16)
    w_perm_all = _permute_w_in(w_in, d, ssd_heads, total)
    lru_tiles_all = _lru_gate_tiles(lru_w_a, lru_w_x)
    w_branch_all = w_branch.astype(BF16)
    w_out_all = w_out.astype(BF16)

    h = jnp.concatenate([ctx, x], axis=1)
    for l in range(depth):
        last = l == depth - 1
        col_major = l % 2 == 1
        mods = mods_all[l]

        if col_major:
            h = _to_colmajor(h, n_ctx, rows)
        h = _ffn_call(h, mods, wup_all, wdn_all, (l, 0), ln_g[l, 0], ln_b[l, 0], j=0, n_ctx_blocks=n_ctx_blocks,
                      alpha=alpha)

        conv_w = jnp.concatenate([ssd_conv_w[l], lru_conv_w[l]], axis=1)
        conv_b = jnp.concatenate([ssd_conv_b[l], lru_conv_b[l]], axis=0)[None, :]
        p, small = _proj_call(h, mods, w_perm_all, l, conv_w, conv_b, layout, n_ctx_blocks=n_ctx_blocks,
                              line_ctx=n_ctx, line_lat=rows if col_major else GRID_W)

        y_ssd, o_gla = _chunk_scans_call(p, small, layout, ssd_dt_bias[l], ssd_a_log[l], expand, gla_w_gate[l],
                                         gla_b_gate[l][:, None, :], alr_off=alr_off, n_ctx_chunks=n_ctx_chunks)
        h_lru = _lru_call(p, layout, lru_tiles_all, l, lru_b_a[l][:, None, :], lru_b_x[l][:, None, :],
                          lru_lam[l][:, None, :], n_ctx_blocks=n_ctx_blocks)

        d_skip_w = jnp.repeat(ssd_d[l, 0] + ssd_d[l, 1], SSD_HEAD_DIM)[None, :]
        skip_blocks = n_ctx_blocks if last else 0
        h = _merge_call(h, mods, p, layout, y_ssd, h_lru, o_gla, d_skip_w, ssd_norm_g[l], gla_norm_g[l],
                        w_branch_all, w_out_all, l, ln_g[l, 1], ln_b[l, 1],
                        n_ctx_blocks=n_ctx_blocks, skip_blocks=skip_blocks, alpha=alpha)
        n_ctx_now = 0 if last else n_ctx
        h = _ffn_call(h, mods, wup_all, wdn_all, (l, 1), ln_g[l, 2], ln_b[l, 2], j=2,
                      n_ctx_blocks=n_ctx_now // TOKEN_BLOCK, alpha=alpha)
        if col_major:
            h = _to_raster(h, n_ctx_now, rows)
    return h
```

```python
import functools
import math

import jax
import jax.numpy as jnp
from jax import lax
from jax.experimental import pallas as pl
from jax.experimental.pallas import tpu as pltpu

F32 = jnp.float32
BF16 = jnp.bfloat16

GRID_W = 64
CONV_W = 4
CONV_LEFT = 2
SSD_HEAD_DIM = 64
SSD_GROUPS = 4
SSD_STATE = 128
LRU_BLOCK = 64
LRU_C = 8.0
GLA_HEADS = 4
GLA_GATE_RANK = 16
GLA_TAU = 16.0
N_MOD = 9
N_BRANCH = 3
FFN_RES_W = 0.5
NORM_EPS = 1e-5

TOKEN_BLOCK = 256
SCAN_CHUNK = 128
SCAN_BATCH = 4
MXU_TILE = 256
LANES = 128
SUBLANES = 8
VMEM_LIMIT = 56 * 1024 * 1024


def _cparams(n_axes):
    return pltpu.CompilerParams(dimension_semantics=("arbitrary",) * n_axes,
                                vmem_limit_bytes=VMEM_LIMIT)


def _resident(block_shape, index_map):
    return pl.BlockSpec(block_shape, index_map, pipeline_mode=pl.Buffered(1))


def _stacked(arr, lead):
    n = len(lead)
    tail = tuple(arr.shape[n:])
    index = tuple(lead) + (0,) * len(tail)
    return pl.BlockSpec((None,) * n + tail, lambda *_: index, pipeline_mode=pl.Buffered(1))


def _sigmoid(x):
    return 0.5 * jnp.tanh(0.5 * x) + 0.5


def _silu(x):
    return x * _sigmoid(x)


def _softplus(x):
    return jnp.maximum(x, 0.0) + jnp.log1p(jnp.exp(-jnp.abs(x)))


def _log_sigmoid(x):
    return jnp.minimum(x, 0.0) - jnp.log(1.0 + jnp.exp(-jnp.abs(x)))


def _split_terms(x, n):
    terms = []
    for _ in range(n - 1):
        t = x.astype(BF16)
        terms.append(t)
        x = x - t.astype(F32)
    terms.append(x.astype(BF16))
    return terms


_dot = functools.partial(jnp.dot, preferred_element_type=F32)


def _exact_lhs_dot(t_bf16, x, n_terms):
    return _dot(jnp.concatenate([t_bf16] * n_terms, axis=1), jnp.concatenate(_split_terms(x, n_terms), axis=0))


def _exact_rhs_dot(x, e3_bf16):
    return _dot(jnp.concatenate(_split_terms(x, 3), axis=1), e3_bf16)


def _f32_dot_small_k(a, w):
    at = _split_terms(a, 3)
    wt = _split_terms(w, 3)
    pairs = [(0, 0), (0, 1), (0, 2), (1, 0), (1, 1), (2, 0)]
    lhs = jnp.concatenate([at[i] for i, _ in pairs], axis=1)
    rhs = jnp.concatenate([wt[j] for _, j in pairs], axis=0)
    return _dot(lhs, rhs)


def _layer_norm(z, g, b):
    mu = jnp.mean(z, axis=-1, keepdims=True)
    zc = z - mu
    var = jnp.mean(zc * zc, axis=-1, keepdims=True)
    return zc * lax.rsqrt(var + NORM_EPS) * g + b


def _scan_order(i, rev, n_ctx, n_tot):
    back = jnp.where(i < n_ctx, n_ctx - 1 - i, n_ctx + n_tot - 1 - i)
    return jnp.where(rev == 0, i, back)


def _ada_kernel(s_ref, w_ref, b_ref, o_ref):
    s = _silu(s_ref[...])
    o_ref[...] = jnp.dot(s, w_ref[...], preferred_element_type=F32,
                         precision=lax.Precision.HIGHEST) + b_ref[...]


def _ada_call(cond, w_ada, b_ada):
    n_layers, d, nd = w_ada.shape
    rows = cond.shape[0]
    n_col = nd // d
    return pl.pallas_call(
        _ada_kernel,
        out_shape=jax.ShapeDtypeStruct((n_layers, rows, nd), F32),
        grid=(n_layers, n_col),
        in_specs=[
            pl.BlockSpec((rows, d), lambda l, j: (0, 0)),
            pl.BlockSpec((None, d, d), lambda l, j: (l, 0, j)),
            pl.BlockSpec((None, 1, d), lambda l, j: (l, 0, j)),
        ],
        out_specs=pl.BlockSpec((None, rows, d), lambda l, j: (l, 0, j)),
        compiler_params=_cparams(2),
        name="ada_mod",
    )(cond, w_ada, b_ada.reshape(n_layers, 1, nd))


def _ffn_kernel(x_ref, m_ref, wup_ref, wdn_ref, g_ref, b_ref, o_ref, *, j, alpha, d_ff, ff_chunk):
    x = x_ref[...]
    shift = m_ref[3 * j:3 * j + 1, :]
    scale = m_ref[3 * j + 1:3 * j + 2, :]
    gate = m_ref[3 * j + 2:3 * j + 3, :]
    u = (x * (1.0 + scale) + shift).astype(BF16)
    acc = None
    for c0 in range(0, d_ff, ff_chunk):
        a = _dot(u, wup_ref[:, c0:c0 + ff_chunk])
        v = _dot(u, wup_ref[:, d_ff + c0:d_ff + c0 + ff_chunk])
        hid = (_silu(a) * v).astype(BF16)
        part = _dot(hid, wdn_ref[c0:c0 + ff_chunk, :])
        acc = part if acc is None else acc + part
    z = alpha * x + (FFN_RES_W * gate) * acc
    o_ref[...] = _layer_norm(z, g_ref[...], b_ref[...])


def _ffn_call(h, mods, w_up, w_dn, lead, ln_g, ln_b, *, j, n_ctx_blocks, alpha):
    n_b, tt, d = h.shape
    d_ff = w_dn.shape[-2]
    ff_chunk = d_ff // 2 if (d_ff // 2) % LANES == 0 else d_ff
    ctx_row = mods.shape[0] - 1
    n_blocks = tt // TOKEN_BLOCK

    def mod_map(b, t):
        return (jnp.where(t < n_ctx_blocks, ctx_row, b), 0, 0)

    kern = functools.partial(_ffn_kernel, j=j, alpha=alpha, d_ff=d_ff, ff_chunk=ff_chunk)
    return pl.pallas_call(
        kern,
        out_shape=jax.ShapeDtypeStruct(h.shape, F32),
        grid=(n_b, n_blocks),
        in_specs=[
            pl.BlockSpec((None, TOKEN_BLOCK, d), lambda b, t: (b, t, 0)),
            pl.BlockSpec((None, N_MOD, d), mod_map),
            _stacked(w_up, lead),
            _stacked(w_dn, lead),
            _resident((1, d), lambda b, t: (0, 0)),
            _resident((1, d), lambda b, t: (0, 0)),
        ],
        out_specs=pl.BlockSpec((None, TOKEN_BLOCK, d), lambda b, t: (b, t, 0)),
        compiler_params=_cparams(2),
        name="ffn_sublayer",
    )(h, mods, w_up, w_dn, ln_g.reshape(1, d), ln_b.reshape(1, d))


def _proj_layout(d, ssd_heads):
    ssd_w = d
    bc_w = SSD_GROUPS * SSD_STATE
    gla_key = d // 2
    names = [("ssd_x", ssd_w), ("ssd_b", bc_w), ("ssd_c", bc_w), ("lru_x", d), ("ssd_z", ssd_w),
             ("lru_g", d), ("gla_q", gla_key), ("gla_k", gla_key), ("gla_v", d), ("gla_g", d),
             ("gates", N_BRANCH * d), ("small", LANES)]
    off, out = 0, {}
    for name, width in names:
        out[name] = (off, width)
        off += width
    return out, off


_ACTIVATIONS = {"none": lambda v: v, "silu": _silu, "sigmoid": _sigmoid, "gelu": jax.nn.gelu}


def _proj_kernel(x_ref, m_ref, w_ref, cw_ref, cb_ref, o_ref, sm_ref, ybuf, *, segments, main_cols,
                 n_ctx_blocks, line_ctx, line_lat):
    x = x_ref[...]
    u = (x * (1.0 + m_ref[4:5, :]) + m_ref[3:4, :]).astype(BF16)
    t = pl.program_id(1)
    line = jnp.where(t < n_ctx_blocks, line_ctx, line_lat)
    rows = x.shape[0]
    halo = SUBLANES
    sm_ref[...] = _dot(u, w_ref[:, main_cols:main_cols + LANES])
    pos = lax.broadcasted_iota(jnp.int32, (rows, LANES), 0) & (line - 1)
    taps = [(k, k - CONV_LEFT) for k in range(CONV_W) if k != CONV_LEFT]
    valid = {off: (pos + off >= 0) & (pos + off < line) for _, off in taps}
    zeros = jnp.zeros((halo, LANES), F32)
    for c0, cw, conv, act in segments:
        y = _dot(u, w_ref[:, c0:c0 + cw])
        if not conv:
            o_ref[:, c0:c0 + cw] = _ACTIVATIONS[act](y).astype(o_ref.dtype)
            continue
        for j in range(cw // LANES):
            ybuf[j, 0:halo, :] = zeros
            ybuf[j, halo:halo + rows, :] = y[:, j * LANES:(j + 1) * LANES]
            ybuf[j, halo + rows:2 * halo + rows, :] = zeros
        for j in range(cw // LANES):
            cols = slice(c0 + j * LANES, c0 + (j + 1) * LANES)
            acc = cb_ref[:, cols] + ybuf[j, halo:halo + rows, :] * cw_ref[CONV_LEFT:CONV_LEFT + 1, cols]
            for k, off in taps:
                shifted = ybuf[j, halo + off:halo + off + rows, :]
                acc = acc + jnp.where(valid[off], shifted, 0.0) * cw_ref[k:k + 1, cols]
            o_ref[:, cols] = _ACTIVATIONS[act](acc).astype(o_ref.dtype)


def _proj_segments(layout, col_chunk):
    kinds = {"ssd_x": (True, "silu"), "ssd_b": (True, "silu"), "ssd_c": (True, "silu"), "lru_x": (True, "none"),
             "ssd_z": (False, "silu"), "lru_g": (False, "gelu"), "gla_q": (False, "none"), "gla_k": (False, "none"),
             "gla_v": (False, "none"), "gla_g": (False, "silu"), "gates": (False, "sigmoid")}
    segs = []
    for name, (conv, act) in kinds.items():
        off, width = layout[name]
        for c0 in range(off, off + width, col_chunk):
            segs.append((c0, min(col_chunk, off + width - c0), conv, act))
    return tuple(segs)


def _proj_call(h, mods, w_perm, layer, conv_w, conv_b, layout, *, n_ctx_blocks, line_ctx, line_lat):
    n_b, tt, d = h.shape
    total = w_perm.shape[-1]
    conv_cols = conv_w.shape[1]
    ctx_row = mods.shape[0] - 1
    n_blocks = tt // TOKEN_BLOCK
    col_chunk = 1024

    def mod_map(b, t):
        return (jnp.where(t < n_ctx_blocks, ctx_row, b), 0, 0)

    main = total - LANES
    kern = functools.partial(_proj_kernel, segments=_proj_segments(layout, col_chunk), main_cols=main,
                             n_ctx_blocks=n_ctx_blocks, line_ctx=line_ctx, line_lat=line_lat)
    return pl.pallas_call(
        kern,
        out_shape=(jax.ShapeDtypeStruct((n_b, tt, main), BF16), jax.ShapeDtypeStruct((n_b, tt, LANES), F32)),
        grid=(n_b, n_blocks),
        in_specs=[
            pl.BlockSpec((None, TOKEN_BLOCK, d), lambda b, t: (b, t, 0)),
            pl.BlockSpec((None, N_MOD, d), mod_map),
            _stacked(w_perm, (layer,)),
            _resident((CONV_W, conv_cols), lambda b, t: (0, 0)),
            _resident((1, conv_cols), lambda b, t: (0, 0)),
        ],
        out_specs=(pl.BlockSpec((None, TOKEN_BLOCK, main), lambda b, t: (b, t, 0)),
                   pl.BlockSpec((None, TOKEN_BLOCK, LANES), lambda b, t: (b, t, 0))),
        scratch_shapes=[pltpu.VMEM((col_chunk // LANES, TOKEN_BLOCK + 2 * SUBLANES, LANES), F32)],
        compiler_params=_cparams(2),
        name="mixer_in_proj",
    )(h, mods, w_perm, conv_w, conv_b)


def _ssd_chunk(fwd, x_ref, b_ref, c_ref, sm_ref, dtb_ref, alog_ref, exp_ref, y_ref, state_ref, n_heads):
    q = x_ref.shape[0]
    hg = n_heads // SSD_GROUPS
    gw = hg * SSD_HEAD_DIM
    sm = sm_ref[...]
    dt_raw = jnp.where(fwd, sm[:, 0:n_heads], sm[:, n_heads:2 * n_heads])
    dt_bias = jnp.where(fwd, dtb_ref[0:1, :], dtb_ref[1:2, :])
    a_log = jnp.where(fwd, alog_ref[0:1, :], alog_ref[1:2, :])
    dt = _softplus(dt_raw + dt_bias)
    a = dt * (-jnp.exp(a_log))
    row = lax.broadcasted_iota(jnp.int32, (q, q), 0)
    col = lax.broadcasted_iota(jnp.int32, (q, q), 1)
    mask = (col - row) * jnp.where(fwd, 1, -1) <= 0
    tri = jnp.where(mask, 1.0, 0.0).astype(BF16)
    cs = _exact_lhs_dot(tri, a, 3)
    tot = jnp.sum(a, axis=0, keepdims=True)
    cs2 = cs * math.log2(math.e)
    src2 = cs2 - jnp.log2(dt)
    pad = jnp.zeros((q, LANES - n_heads), F32)
    src2_t = jnp.concatenate([src2, pad], axis=1).T[0:n_heads]

    w_state = dt * jnp.exp(tot - cs)
    from_start = jnp.exp(cs)
    e_tot = jnp.broadcast_to(jnp.exp(tot), (SUBLANES, n_heads))
    wide = _exact_rhs_dot(jnp.concatenate([w_state, from_start, e_tot], axis=0), exp_ref[...])
    w_state_w = wide[0:q]
    from_start_w = wide[q:2 * q]
    e_tot_w = wide[2 * q:2 * q + 1]

    x_bf = x_ref[...]
    xw_bf = (x_bf.astype(F32) * w_state_w).astype(BF16)
    pair_w = 2 * SSD_HEAD_DIM
    lane = lax.broadcasted_iota(jnp.int32, (1, pair_w), 1)
    keep_lo = jnp.where(lane < SSD_HEAD_DIM, 1.0, 0.0).astype(BF16)
    keep_hi = jnp.where(lane < SSD_HEAD_DIM, 0.0, 1.0).astype(BF16)
    for g in range(SSD_GROUPS):
        bt = b_ref[:, g * SSD_STATE:(g + 1) * SSD_STATE].T
        cg = c_ref[:, g * SSD_STATE:(g + 1) * SSD_STATE]
        scores = _dot(cg, bt)
        s_prev = state_ref[g]
        y_off = _dot(cg, s_prev.astype(BF16)) * from_start_w[:, g * gw:(g + 1) * gw]
        for pp in range(hg // 2):
            ms = []
            for head in (g * hg + 2 * pp, g * hg + 2 * pp + 1):
                seg2 = jnp.where(mask, cs2[:, head:head + 1] - src2_t[head:head + 1, :], -1e30)
                ms.append((scores * jnp.exp2(seg2)).astype(BF16))
            cols = slice(g * gw + pp * pair_w, g * gw + (pp + 1) * pair_w)
            xp = x_bf[:, cols]
            y_pair = _dot(jnp.concatenate(ms, axis=1), jnp.concatenate([xp * keep_lo, xp * keep_hi], axis=0))
            y_ref[:, cols] = (y_off[:, pp * pair_w:(pp + 1) * pair_w] + y_pair).astype(y_ref.dtype)
        upd = _dot(bt, xw_bf[:, g * gw:(g + 1) * gw])
        state_ref[g] = s_prev * e_tot_w[:, g * gw:(g + 1) * gw] + upd


def _scan_col_spec(layout, name, nb, rows, n_ctx, n_tot):
    off, w = layout[name]
    assert off % w == 0
    blk = off // w
    return pl.BlockSpec((nb, rows, w), lambda b, r, i: (b, _scan_order(i, r, n_ctx, n_tot), blk))


def _scan_small_spec(nb, rows, n_ctx, n_tot):
    return pl.BlockSpec((nb, rows, LANES), lambda b, r, i: (b, _scan_order(i, r, n_ctx, n_tot), 0))


def _scan_out_spec(nb, rows, width, n_ctx, n_tot):
    return pl.BlockSpec((None, nb, rows, width), lambda b, r, i: (r, b, _scan_order(i, r, n_ctx, n_tot), 0))


def _lru_scan_block(a_ref, b_ref, o_ref, carry_ref, reverse):
    rows, width = a_ref.shape
    sub = lax.broadcasted_iota(jnp.int32, (SUBLANES, width), 0)
    carry = carry_ref[...]
    n_groups = rows // SUBLANES
    order = range(n_groups - 1, -1, -1) if reverse else range(n_groups)
    for gi in order:
        sl = slice(gi * SUBLANES, (gi + 1) * SUBLANES)
        a = a_ref[sl, :]
        b = b_ref[sl, :]
        k = 1
        while k < SUBLANES:
            if reverse:
                a_sh = pltpu.roll(a, SUBLANES - k, axis=0)
                b_sh = pltpu.roll(b, SUBLANES - k, axis=0)
                ok = sub < SUBLANES - k
            else:
                a_sh = pltpu.roll(a, k, axis=0)
                b_sh = pltpu.roll(b, k, axis=0)
                ok = sub >= k
            b = b + a * jnp.where(ok, b_sh, 0.0)
            a = a * jnp.where(ok, a_sh, 1.0)
            k *= 2
        hblk = b + a * carry
        b_ref[sl, :] = hblk
        carry = hblk[0:1, :] if reverse else hblk[SUBLANES - 1:SUBLANES, :]
    carry_ref[...] = carry
    o_ref[...] = b_ref[...].astype(o_ref.dtype)


def _lru_kernel(x_ref, w_ref, ba_ref, bx_ref, lam_ref, h_ref, a_s, b_s, carry_ref):
    rev = pl.program_id(1)
    i = pl.program_id(2)
    nb = x_ref.shape[0]

    @pl.when(i == 0)
    def _():
        carry_ref[...] = jnp.zeros_like(carry_ref)

    n_tiles = w_ref.shape[0]
    neg_c_sp = (-LRU_C) * _softplus(-lam_ref[...])
    for bi in range(nb):
        x_bf = x_ref[bi]
        pre_a, pre_x = [], []
        for t in range(n_tiles):
            y = _dot(x_bf[:, t * MXU_TILE:(t + 1) * MXU_TILE], w_ref[t])
            pre_a.append(y[:, :MXU_TILE])
            pre_x.append(y[:, MXU_TILE:])
        r = _sigmoid(jnp.concatenate(pre_a, axis=1) + ba_ref[...])
        ig = _sigmoid(jnp.concatenate(pre_x, axis=1) + bx_ref[...])
        log_a = r * neg_c_sp
        a = jnp.exp(log_a)
        a_s[bi] = a
        b_s[bi] = jnp.sqrt(-jnp.tanh(log_a) * (a * a + 1.0)) * (ig * x_bf.astype(F32))

    @pl.when(rev == 0)
    def _():
        for bi in range(nb):
            _lru_scan_block(a_s.at[bi], b_s.at[bi], h_ref.at[bi], carry_ref.at[bi], reverse=False)

    @pl.when(rev == 1)
    def _():
        for bi in range(nb):
            _lru_scan_block(a_s.at[bi], b_s.at[bi], h_ref.at[bi], carry_ref.at[bi], reverse=True)


def _lru_call(p, layout, w_tiles, layer, b_a, b_x, lam, *, n_ctx_blocks):
    n_b, tt, _ = p.shape
    width = layout["lru_x"][1]
    n_blocks = tt // TOKEN_BLOCK
    n_tiles = w_tiles.shape[2]
    nb = SCAN_BATCH if n_b % SCAN_BATCH == 0 else 1

    return pl.pallas_call(
        _lru_kernel,
        out_shape=jax.ShapeDtypeStruct((2, n_b, tt, width), BF16),
        grid=(n_b // nb, 2, n_blocks),
        in_specs=[
            _scan_col_spec(layout, "lru_x", nb, TOKEN_BLOCK, n_ctx_blocks, n_blocks),
            pl.BlockSpec((None, None, n_tiles, MXU_TILE, 2 * MXU_TILE), lambda b, r, i: (layer, r, 0, 0, 0)),
            pl.BlockSpec((None, 1, width), lambda b, r, i: (r, 0, 0)),
            pl.BlockSpec((None, 1, width), lambda b, r, i: (r, 0, 0)),
            pl.BlockSpec((None, 1, width), lambda b, r, i: (r, 0, 0)),
        ],
        out_specs=_scan_out_spec(nb, TOKEN_BLOCK, width, n_ctx_blocks, n_blocks),
        scratch_shapes=[pltpu.VMEM((nb, TOKEN_BLOCK, width), F32), pltpu.VMEM((nb, TOKEN_BLOCK, width), F32),
                        pltpu.VMEM((nb, 1, width), F32)],
        compiler_params=_cparams(3),
        name="rglru_scan",
    )(p, w_tiles, b_a, b_x, lam)


def _gla_chunk(fwd, q_ref, k_ref, v_ref, sm_ref, wg_ref, bg_ref, o_ref, state_ref, alr_off, dk, dv):
    q_len = q_ref.shape[0]
    n_heads = q_ref.shape[1] // dk
    sm = sm_ref[...]
    r = GLA_GATE_RANK
    a_lr = jnp.where(fwd, sm[:, alr_off:alr_off + r], sm[:, alr_off + r:alr_off + 2 * r])
    z = _f32_dot_small_k(a_lr, wg_ref[...]) + bg_ref[...]
    log_a = _log_sigmoid(z) * (1.0 / GLA_TAU)
    row = lax.broadcasted_iota(jnp.int32, (q_len, q_len), 0)
    col = lax.broadcasted_iota(jnp.int32, (q_len, q_len), 1)
    mask = (col - row) * jnp.where(fwd, 1, -1) <= 0
    tri = jnp.where(mask, 1.0, 0.0).astype(BF16)
    bcum = _exact_lhs_dot(tri, log_a, 2)
    btot = jnp.sum(log_a, axis=0, keepdims=True)
    bmid = bcum[q_len // 2:q_len // 2 + 1, :]
    q_in = q_ref[...].astype(F32) * (dk ** -0.5) * jnp.exp(bcum - bmid)
    k_in = k_ref[...].astype(F32) * jnp.exp(bmid - bcum)
    q_off = (q_in * jnp.exp(bmid)).astype(BF16)
    k_st = (k_in * jnp.exp(btot - bmid)).astype(BF16)
    q_in = q_in.astype(BF16)
    k_in = k_in.astype(BF16)
    e_tot = jnp.exp(btot)
    for hh in range(n_heads):
        ks = slice(hh * dk, (hh + 1) * dk)
        vs = slice(hh * dv, (hh + 1) * dv)
        v_bf = v_ref[:, vs]
        att = lax.dot_general(q_in[:, ks], k_in[:, ks], (((1,), (1,)), ((), ())), preferred_element_type=F32)
        att = jnp.where(mask, att, 0.0).astype(BF16)
        s_prev = state_ref[hh]
        inter = lax.dot_general(q_off[:, ks], s_prev.astype(BF16), (((1,), (1,)), ((), ())),
                                preferred_element_type=F32)
        o_ref[:, vs] = (_dot(att, v_bf) + inter).astype(o_ref.dtype)
        upd = lax.dot_general(v_bf, k_st[:, ks], (((0,), (0,)), ((), ())), preferred_element_type=F32)
        state_ref[hh] = s_prev * e_tot[:, ks] + upd


def _chunk_scans_kernel(x_ref, b_ref, c_ref, q_ref, k_ref, v_ref, sm_ref, dtb_ref, alog_ref, exp_ref, wg_ref, bg_ref,
                        y_ref, o_ref, ssd_state, gla_state, *, n_heads, alr_off, dk, dv):
    rev = pl.program_id(1)
    i = pl.program_id(2)

    @pl.when(i == 0)
    def _():
        ssd_state[...] = jnp.zeros_like(ssd_state)
        gla_state[...] = jnp.zeros_like(gla_state)

    for bi in range(x_ref.shape[0]):
        _ssd_chunk(rev == 0, x_ref.at[bi], b_ref.at[bi], c_ref.at[bi], sm_ref.at[bi], dtb_ref, alog_ref, exp_ref,
                   y_ref.at[bi], ssd_state.at[bi], n_heads)
        _gla_chunk(rev == 0, q_ref.at[bi], k_ref.at[bi], v_ref.at[bi], sm_ref.at[bi], wg_ref, bg_ref,
                   o_ref.at[bi], gla_state.at[bi], alr_off, dk, dv)


def _chunk_scans_call(p, small, layout, dt_bias, a_log, expand, w_gate, b_gate, *, alr_off, n_ctx_chunks):
    n_b, tt, _ = p.shape
    n_heads = dt_bias.shape[1]
    ssd_w = n_heads * SSD_HEAD_DIM
    hg = n_heads // SSD_GROUPS
    key_w = layout["gla_q"][1]
    val_w = layout["gla_v"][1]
    dk = key_w // GLA_HEADS
    dv = val_w // GLA_HEADS
    n_chunks = tt // SCAN_CHUNK
    nb = SCAN_BATCH if n_b % SCAN_BATCH == 0 else 1
    col_spec = functools.partial(_scan_col_spec, layout, nb=nb, rows=SCAN_CHUNK, n_ctx=n_ctx_chunks, n_tot=n_chunks)

    kern = functools.partial(_chunk_scans_kernel, n_heads=n_heads, alr_off=alr_off, dk=dk, dv=dv)
    return pl.pallas_call(
        kern,
        out_shape=(jax.ShapeDtypeStruct((2, n_b, tt, ssd_w), BF16), jax.ShapeDtypeStruct((2, n_b, tt, val_w), BF16)),
        grid=(n_b // nb, 2, n_chunks),
        in_specs=[
            col_spec("ssd_x"), col_spec("ssd_b"), col_spec("ssd_c"),
            col_spec("gla_q"), col_spec("gla_k"), col_spec("gla_v"),
            _scan_small_spec(nb, SCAN_CHUNK, n_ctx_chunks, n_chunks),
            _resident((2, n_heads), lambda b, r, i: (0, 0)),
            _resident((2, n_heads), lambda b, r, i: (0, 0)),
            _resident(expand.shape, lambda b, r, i: (0, 0)),
            pl.BlockSpec((None, GLA_GATE_RANK, key_w), lambda b, r, i: (r, 0, 0)),
            pl.BlockSpec((None, 1, key_w), lambda b, r, i: (r, 0, 0)),
        ],
        out_specs=(_scan_out_spec(nb, SCAN_CHUNK, ssd_w, n_ctx_chunks, n_chunks),
                   _scan_out_spec(nb, SCAN_CHUNK, val_w, n_ctx_chunks, n_chunks)),
        scratch_shapes=[pltpu.VMEM((nb, SSD_GROUPS, SSD_STATE, hg * SSD_HEAD_DIM), F32),
                        pltpu.VMEM((nb, GLA_HEADS, dv, dk), F32)],
        compiler_params=_cparams(3),
        name="ssd_gla_scan",
    )(p, p, p, p, p, p, small, dt_bias, a_log, expand, w_gate, b_gate)


def _merge_kernel(h_ref, m_ref, ys0_ref, ys1_ref, xs_ref, z_ref, hl0_ref, hl1_ref, gl_ref,
                  og0_ref, og1_ref, gg_ref, gt0_ref, gt1_ref, gt2_ref, dskip_ref, sng_ref, gng_ref,
                  wbr_ref, wout_ref, lng_ref, lnb_ref, o_ref, *, alpha, dv):
    d = h_ref.shape[1]

    def f32(ref, cols=slice(None)):
        return ref[:, cols].astype(F32)

    y = (f32(ys0_ref) + f32(ys1_ref) + dskip_ref[...] * f32(xs_ref)) * f32(z_ref)
    y = y * lax.rsqrt(jnp.mean(y * y, axis=-1, keepdims=True) + NORM_EPS) * sng_ref[...]
    m = f32(gt0_ref) * _dot(y.astype(BF16), wbr_ref[0])
    y = (f32(hl0_ref) + f32(hl1_ref)) * f32(gl_ref)
    m = m + f32(gt1_ref) * _dot(y.astype(BF16), wbr_ref[1])
    parts = []
    for hh in range(d // dv):
        vs = slice(hh * dv, (hh + 1) * dv)
        o = f32(og0_ref, vs) + f32(og1_ref, vs)
        o = o * lax.rsqrt(jnp.mean(o * o, axis=-1, keepdims=True) + NORM_EPS) * gng_ref[...]
        parts.append(o * f32(gg_ref, vs))
    y = jnp.concatenate(parts, axis=1)
    m = m + f32(gt2_ref) * _dot(y.astype(BF16), wbr_ref[2])
    out = _dot(m.astype(BF16), wout_ref[...])
    z = alpha * h_ref[...] + m_ref[5:6, :] * out
    o_ref[...] = _layer_norm(z, lng_ref[...], lnb_ref[...])


def _merge_call(h, mods, p, layout, y_ssd, h_lru, o_gla, d_skip_w, ssd_norm_g, gla_norm_g, w_branch, w_out, layer,
                ln_g, ln_b, *, n_ctx_blocks, skip_blocks, alpha):
    n_b, tt, d = h.shape
    n_blocks = tt // TOKEN_BLOCK - skip_blocks
    ctx_row = mods.shape[0] - 1
    dv = gla_norm_g.shape[-1]

    def mod_map(b, t):
        return (jnp.where(t + skip_blocks < n_ctx_blocks, ctx_row, b), 0, 0)

    def tok(width, blk):
        return pl.BlockSpec((None, TOKEN_BLOCK, width), lambda b, t: (b, t + skip_blocks, blk))

    def col_spec(name):
        off, w = layout[name]
        blk = off // w
        assert blk * w == off
        return tok(w, blk)

    def gate_spec(n):
        off = layout["gates"][0] + n * d
        assert off % d == 0
        return tok(d, off // d)

    def dir_spec(rv):
        return pl.BlockSpec((None, None, TOKEN_BLOCK, d), lambda b, t: (rv, b, t + skip_blocks, 0))

    kern = functools.partial(_merge_kernel, alpha=alpha, dv=dv)
    return pl.pallas_call(
        kern,
        out_shape=jax.ShapeDtypeStruct((n_b, n_blocks * TOKEN_BLOCK, d), F32),
        grid=(n_b, n_blocks),
        in_specs=[
            tok(d, 0),
            pl.BlockSpec((None, N_MOD, d), mod_map),
            dir_spec(0), dir_spec(1), col_spec("ssd_x"), col_spec("ssd_z"),
            dir_spec(0), dir_spec(1), col_spec("lru_g"),
            dir_spec(0), dir_spec(1), col_spec("gla_g"),
            gate_spec(0), gate_spec(1), gate_spec(2),
            _resident((1, d), lambda b, t: (0, 0)),
            _resident((1, d), lambda b, t: (0, 0)),
            _resident((1, dv), lambda b, t: (0, 0)),
            _stacked(w_branch, (layer,)),
            _stacked(w_out, (layer,)),
            _resident((1, d), lambda b, t: (0, 0)),
            _resident((1, d), lambda b, t: (0, 0)),
        ],
        out_specs=pl.BlockSpec((None, TOKEN_BLOCK, d), lambda b, t: (b, t, 0)),
        compiler_params=_cparams(2),
        name="mixer_merge",
    )(h, mods, y_ssd, y_ssd, p, p, h_lru, h_lru, p, o_gla, o_gla, p, p, p, p,
      d_skip_w, ssd_norm_g.reshape(1, d), gla_norm_g.reshape(1, dv), w_branch, w_out,
      ln_g.reshape(1, d), ln_b.reshape(1, d))


def _permute_w_in(w_in, d, ssd_heads, total):
    ssd_w = d
    bc_w = SSD_GROUPS * SSD_STATE
    gla_key = d // 2
    sizes = (ssd_w, ssd_w + 2 * bc_w, 2 * ssd_heads, d, d, gla_key, gla_key, d, d, 2 * GLA_GATE_RANK, N_BRANCH * d)
    offs = [0]
    for s in sizes:
        offs.append(offs[-1] + s)
    w16 = w_in.astype(BF16)
    z, xbc, dtr, lx, lg, gq, gk, gv, gg, alr, gates = [w16[..., offs[n]:offs[n + 1]] for n in range(len(sizes))]
    pad = jnp.zeros(w16.shape[:-1] + (LANES - dtr.shape[-1] - alr.shape[-1],), BF16)
    w = jnp.concatenate([xbc, lx, z, lg, gq, gk, gv, gg, gates, dtr, alr, pad], axis=-1)
    assert w.shape[-1] == total
    return w


def _lru_gate_tiles(w_a, w_x):
    per = MXU_TILE // LRU_BLOCK
    n_l, n_dir, nb, k, _ = w_a.shape
    n_tiles = nb // per

    def tiles(w):
        w = w.astype(BF16).reshape(n_l, n_dir, n_tiles, per, k, k)
        on_diag = jnp.eye(per, dtype=bool)[:, None, :, None]
        t = jnp.where(on_diag, w[:, :, :, :, :, None, :], jnp.zeros((), BF16))
        return t.reshape(n_l, n_dir, n_tiles, MXU_TILE, MXU_TILE)

    return jnp.concatenate([tiles(w_a), tiles(w_x)], axis=-1)


def _to_colmajor(h, n_ctx, rows):
    n_b, _, d = h.shape
    lat = h[:, n_ctx:].reshape(n_b, rows, GRID_W, d).transpose(0, 2, 1, 3).reshape(n_b, rows * GRID_W, d)
    return jnp.concatenate([h[:, :n_ctx], lat], axis=1)


def _to_raster(h, n_ctx, rows):
    n_b, _, d = h.shape
    lat = h[:, n_ctx:].reshape(n_b, GRID_W, rows, d).transpose(0, 2, 1, 3).reshape(n_b, rows * GRID_W, d)
    return jnp.concatenate([h[:, :n_ctx], lat], axis=1)


def kernel(x, c, ctx, c_ctx, w_ada, b_ada, ln_g, ln_b, ffn_w_up, ffn_w_down, w_in, ssd_conv_w, ssd_conv_b,
           ssd_dt_bias, ssd_a_log, ssd_d, ssd_norm_g, lru_conv_w, lru_conv_b, lru_w_a, lru_b_a, lru_w_x, lru_b_x,
           lru_lam, gla_w_gate, gla_b_gate, gla_norm_g, w_branch, w_out):
    n_b, t_lat, d = x.shape
    n_ctx = ctx.shape[1]
    depth = w_ada.shape[0]
    rows = t_lat // GRID_W
    ssd_heads = ssd_dt_bias.shape[-1]
    assert n_ctx % TOKEN_BLOCK == 0 and t_lat % TOKEN_BLOCK == 0
    for line in (n_ctx, GRID_W, rows):
        assert line & (line - 1) == 0 and TOKEN_BLOCK % min(line, TOKEN_BLOCK) == 0
    assert n_ctx <= TOKEN_BLOCK
    alpha = (2.0 * depth) ** 0.25
    n_ctx_blocks = n_ctx // TOKEN_BLOCK
    n_ctx_chunks = n_ctx // SCAN_CHUNK

    layout, total = _proj_layout(d, ssd_heads)
    alr_off = 2 * ssd_heads

    n_rows = -(-(n_b + 1) // SUBLANES) * SUBLANES
    cond = jnp.concatenate([c, jnp.zeros((n_rows - n_b - 1, d), F32), c_ctx[None, :]], axis=0)
    mods_all = _ada_call(cond, w_ada, b_ada).reshape(depth, n_rows, N_MOD, d)

    expand = jnp.repeat(jnp.eye(ssd_heads, dtype=BF16), SSD_HEAD_DIM, axis=1)
    expand = jnp.concatenate([expand] * 3, axis=0)
    wup_all = ffn_w_up.astype(BF16)
    wdn_all = ffn_w_down.astype(BF16)
    w_perm_all = _permute_w_in(w_in, d, ssd_heads, total)
    lru_tiles_all = _lru_gate_tiles(lru_w_a, lru_w_x)
    w_branch_all = w_branch.astype(BF16)
    w_out_all = w_out.astype(BF16)

    h = jnp.concatenate([ctx, x], axis=1)
    for l in range(depth):
        last = l == depth - 1
        col_major = l % 2 == 1
        mods = mods_all[l]

        if col_major:
            h = _to_colmajor(h, n_ctx, rows)
        h = _ffn_call(h, mods, wup_all, wdn_all, (l, 0), ln_g[l, 0], ln_b[l, 0], j=0, n_ctx_blocks=n_ctx_blocks,
                      alpha=alpha)

        conv_w = jnp.concatenate([ssd_conv_w[l], lru_conv_w[l]], axis=1)
        conv_b = jnp.concatenate([ssd_conv_b[l], lru_conv_b[l]], axis=0)[None, :]
        p, small = _proj_call(h, mods, w_perm_all, l, conv_w, conv_b, layout, n_ctx_blocks=n_ctx_blocks,
                              line_ctx=n_ctx, line_lat=rows if col_major else GRID_W)

        y_ssd, o_gla = _chunk_scans_call(p, small, layout, ssd_dt_bias[l], ssd_a_log[l], expand, gla_w_gate[l],
                                         gla_b_gate[l][:, None, :], alr_off=alr_off, n_ctx_chunks=n_ctx_chunks)
        h_lru = _lru_call(p, layout, lru_tiles_all, l, lru_b_a[l][:, None, :], lru_b_x[l][:, None, :],
                          lru_lam[l][:, None, :], n_ctx_blocks=n_ctx_blocks)

        d_skip_w = jnp.repeat(ssd_d[l, 0] + ssd_d[l, 1], SSD_HEAD_DIM)[None, :]
        skip_blocks = n_ctx_blocks if last else 0
        h = _merge_call(h, mods, p, layout, y_ssd, h_lru, o_gla, d_skip_w, ssd_norm_g[l], gla_norm_g[l],
                        w_branch_all, w_out_all, l, ln_g[l, 1], ln_b[l, 1],
                        n_ctx_blocks=n_ctx_blocks, skip_blocks=skip_blocks, alpha=alpha)
        n_ctx_now = 0 if last else n_ctx
        h = _ffn_call(h, mods, wup_all, wdn_all, (l, 1), ln_g[l, 2], ln_b[l, 2], j=2,
                      n_ctx_blocks=n_ctx_now // TOKEN_BLOCK, alpha=alpha)
        if col_major:
            h = _to_raster(h, n_ctx_now, rows)
    return h
```

```python
import functools
import math

import jax
import jax.numpy as jnp
from jax import lax
from jax.experimental import pallas as pl
from jax.experimental.pallas import tpu as pltpu

F32 = jnp.float32
BF16 = jnp.bfloat16

GRID_W = 64
CONV_W = 4
CONV_LEFT = 2
SSD_HEAD_DIM = 64
SSD_GROUPS = 4
SSD_STATE = 128
LRU_BLOCK = 64
LRU_C = 8.0
GLA_HEADS = 4
GLA_GATE_RANK = 16
GLA_TAU = 16.0
N_MOD = 9
N_BRANCH = 3
FFN_RES_W = 0.5
NORM_EPS = 1e-5

TOKEN_BLOCK = 256
SCAN_CHUNK = 128
SCAN_BATCH = 4
TOKEN_BATCH = 4
MERGE_BATCH = 2
PROJ_BATCH = 1
MXU_TILE = 256
LANES = 128
SUBLANES = 8
VMEM_LIMIT = 56 * 1024 * 1024


def _cparams(n_axes):
    return pltpu.CompilerParams(dimension_semantics=("arbitrary",) * n_axes,
                                vmem_limit_bytes=VMEM_LIMIT)


def _resident(block_shape, index_map):
    return pl.BlockSpec(block_shape, index_map, pipeline_mode=pl.Buffered(1))


def _stacked(arr, lead):
    n = len(lead)
    tail = tuple(arr.shape[n:])
    index = tuple(lead) + (0,) * len(tail)
    return pl.BlockSpec((None,) * n + tail, lambda *_: index, pipeline_mode=pl.Buffered(1))


def _sigmoid(x):
    return 0.5 * jnp.tanh(0.5 * x) + 0.5


def _silu(x):
    return x * _sigmoid(x)


def _softplus(x):
    return jnp.maximum(x, 0.0) + jnp.log1p(jnp.exp(-jnp.abs(x)))


def _log_sigmoid(x):
    return jnp.minimum(x, 0.0) - jnp.log(1.0 + jnp.exp(-jnp.abs(x)))


def _split_terms(x, n):
    terms = []
    for _ in range(n - 1):
        t = x.astype(BF16)
        terms.append(t)
        x = x - t.astype(F32)
    terms.append(x.astype(BF16))
    return terms


_dot = functools.partial(jnp.dot, preferred_element_type=F32)


def _exact_lhs_dot(t_bf16, x, n_terms):
    return _dot(jnp.concatenate([t_bf16] * n_terms, axis=1), jnp.concatenate(_split_terms(x, n_terms), axis=0))


def _exact_rhs_dot(x, e3_bf16):
    return _dot(jnp.concatenate(_split_terms(x, 3), axis=1), e3_bf16)


def _f32_dot_small_k(a, w):
    at = _split_terms(a, 3)
    wt = _split_terms(w, 3)
    pairs = [(0, 0), (0, 1), (0, 2), (1, 0), (1, 1), (2, 0)]
    lhs = jnp.concatenate([at[i] for i, _ in pairs], axis=1)
    rhs = jnp.concatenate([wt[j] for _, j in pairs], axis=0)
    return _dot(lhs, rhs)


def _layer_norm(z, g, b):
    mu = jnp.mean(z, axis=-1, keepdims=True)
    zc = z - mu
    var = jnp.mean(zc * zc, axis=-1, keepdims=True)
    return zc * lax.rsqrt(var + NORM_EPS) * g + b


def _scan_order(i, rev, n_ctx, n_tot):
    back = jnp.where(i < n_ctx, n_ctx - 1 - i, n_ctx + n_tot - 1 - i)
    return jnp.where(rev == 0, i, back)


def _ada_kernel(s_ref, w_ref, b_ref, o_ref):
    s = _silu(s_ref[...])
    o_ref[...] = jnp.dot(s, w_ref[...], preferred_element_type=F32,
                         precision=lax.Precision.HIGHEST) + b_ref[...]


def _ada_call(cond, w_ada, b_ada):
    n_layers, d, nd = w_ada.shape
    rows = cond.shape[0]
    n_col = nd // d
    return pl.pallas_call(
        _ada_kernel,
        out_shape=jax.ShapeDtypeStruct((n_layers, rows, nd), F32),
        grid=(n_layers, n_col),
        in_specs=[
            pl.BlockSpec((rows, d), lambda l, j: (0, 0)),
            pl.BlockSpec((None, d, d), lambda l, j: (l, 0, j)),
            pl.BlockSpec((None, 1, d), lambda l, j: (l, 0, j)),
        ],
        out_specs=pl.BlockSpec((None, rows, d), lambda l, j: (l, 0, j)),
        compiler_params=_cparams(2),
        name="ada_mod",
    )(cond, w_ada, b_ada.reshape(n_layers, 1, nd))


def _ffn_kernel(x_ref, m_ref, mctx_ref, wup_ref, wdn_ref, g_ref, b_ref, o_ref, *, j, alpha, d_ff, ff_chunk,
                n_ctx_blocks):
    is_ctx = pl.program_id(1) < n_ctx_blocks
    for bi in range(x_ref.shape[0]):
        x = x_ref[bi]
        m = jnp.where(is_ctx, mctx_ref[3 * j:3 * j + 3, :], m_ref[bi, 3 * j:3 * j + 3, :])
        shift, scale, gate = m[0:1], m[1:2], m[2:3]
        u = (x * (1.0 + scale) + shift).astype(BF16)
        acc = None
        for c0 in range(0, d_ff, ff_chunk):
            a = _dot(u, wup_ref[:, c0:c0 + ff_chunk])
            v = _dot(u, wup_ref[:, d_ff + c0:d_ff + c0 + ff_chunk])
            hid = (_silu(a) * v).astype(BF16)
            part = _dot(hid, wdn_ref[c0:c0 + ff_chunk, :])
            acc = part if acc is None else acc + part
        z = alpha * x + (FFN_RES_W * gate) * acc
        o_ref[bi] = _layer_norm(z, g_ref[...], b_ref[...])


def _ffn_call(h, mods, w_up, w_dn, lead, ln_g, ln_b, *, j, n_ctx_blocks, alpha):
    n_b, tt, d = h.shape
    d_ff = w_dn.shape[-2]
    ff_chunk = d_ff // 2 if (d_ff // 2) % LANES == 0 else d_ff
    ctx_row = mods.shape[0] - 1
    n_blocks = tt // TOKEN_BLOCK
    nb = TOKEN_BATCH if n_b % TOKEN_BATCH == 0 else 1

    kern = functools.partial(_ffn_kernel, j=j, alpha=alpha, d_ff=d_ff, ff_chunk=ff_chunk, n_ctx_blocks=n_ctx_blocks)
    return pl.pallas_call(
        kern,
        out_shape=jax.ShapeDtypeStruct(h.shape, F32),
        grid=(n_b // nb, n_blocks),
        in_specs=[
            pl.BlockSpec((nb, TOKEN_BLOCK, d), lambda b, t: (b, t, 0)),
            pl.BlockSpec((nb, N_MOD, d), lambda b, t: (b, 0, 0)),
            pl.BlockSpec((None, N_MOD, d), lambda b, t: (ctx_row, 0, 0)),
            _stacked(w_up, lead),
            _stacked(w_dn, lead),
            _resident((1, d), lambda b, t: (0, 0)),
            _resident((1, d), lambda b, t: (0, 0)),
        ],
        out_specs=pl.BlockSpec((nb, TOKEN_BLOCK, d), lambda b, t: (b, t, 0)),
        compiler_params=_cparams(2),
        name="ffn_sublayer",
    )(h, mods, mods, w_up, w_dn, ln_g.reshape(1, d), ln_b.reshape(1, d))


def _proj_layout(d, ssd_heads):
    ssd_w = d
    bc_w = SSD_GROUPS * SSD_STATE
    gla_key = d // 2
    names = [("ssd_x", ssd_w), ("ssd_b", bc_w), ("ssd_c", bc_w), ("lru_x", d), ("ssd_z", ssd_w),
             ("lru_g", d), ("gla_q", gla_key), ("gla_k", gla_key), ("gla_v", d), ("gla_g", d),
             ("gates", N_BRANCH * d), ("small", LANES)]
    off, out = 0, {}
    for name, width in names:
        out[name] = (off, width)
        off += width
    return out, off


_ACTIVATIONS = {"none": lambda v: v, "silu": _silu, "sigmoid": _sigmoid, "gelu": jax.nn.gelu}


def _proj_kernel(x_ref, m_ref, mctx_ref, w_ref, cw_ref, cb_ref, o_ref, sm_ref, ybuf, *, segments, main_cols,
                 n_ctx_blocks, line_ctx, line_lat):
    is_ctx = pl.program_id(1) < n_ctx_blocks
    line = jnp.where(is_ctx, line_ctx, line_lat)
    n_strips = ybuf.shape[0] // x_ref.shape[0]
    for bi in range(x_ref.shape[0]):
        m = jnp.where(is_ctx, mctx_ref[3:5, :], m_ref[bi, 3:5, :])
        _proj_block(x_ref.at[bi], m[0:1], m[1:2], w_ref, cw_ref, cb_ref, o_ref.at[bi], sm_ref.at[bi],
                    ybuf.at[bi * n_strips:(bi + 1) * n_strips], line, segments, main_cols)


def _proj_block(x_ref, shift, scale, w_ref, cw_ref, cb_ref, o_ref, sm_ref, ybuf, line, segments, main_cols):
    x = x_ref[...]
    u = (x * (1.0 + scale) + shift).astype(BF16)
    rows = x.shape[0]
    halo = SUBLANES
    sm_ref[...] = _dot(u, w_ref[:, main_cols:main_cols + LANES])
    pos = lax.broadcasted_iota(jnp.int32, (rows, LANES), 0) & (line - 1)
    taps = [(k, k - CONV_LEFT) for k in range(CONV_W) if k != CONV_LEFT]
    valid = {off: (pos + off >= 0) & (pos + off < line) for _, off in taps}
    zeros = jnp.zeros((halo, LANES), F32)
    for c0, cw, conv, act in segments:
        y = _dot(u, w_ref[:, c0:c0 + cw])
        if not conv:
            o_ref[:, c0:c0 + cw] = _ACTIVATIONS[act](y).astype(o_ref.dtype)
            continue
        for j in range(cw // LANES):
            ybuf[j, 0:halo, :] = zeros
            ybuf[j, halo:halo + rows, :] = y[:, j * LANES:(j + 1) * LANES]
            ybuf[j, halo + rows:2 * halo + rows, :] = zeros
        for j in range(cw // LANES):
            cols = slice(c0 + j * LANES, c0 + (j + 1) * LANES)
            acc = cb_ref[:, cols] + ybuf[j, halo:halo + rows, :] * cw_ref[CONV_LEFT:CONV_LEFT + 1, cols]
            for k, off in taps:
                shifted = ybuf[j, halo + off:halo + off + rows, :]
                acc = acc + jnp.where(valid[off], shifted, 0.0) * cw_ref[k:k + 1, cols]
            o_ref[:, cols] = _ACTIVATIONS[act](acc).astype(o_ref.dtype)


def _proj_segments(layout, col_chunk):
    kinds = {"ssd_x": (True, "silu"), "ssd_b": (True, "silu"), "ssd_c": (True, "silu"), "lru_x": (True, "none"),
             "ssd_z": (False, "silu"), "lru_g": (False, "gelu"), "gla_q": (False, "none"), "gla_k": (False, "none"),
             "gla_v": (False, "none"), "gla_g": (False, "silu"), "gates": (False, "sigmoid")}
    segs = []
    for name, (conv, act) in kinds.items():
        off, width = layout[name]
        for c0 in range(off, off + width, col_chunk):
            segs.append((c0, min(col_chunk, off + width - c0), conv, act))
    return tuple(segs)


def _proj_call(h, mods, w_perm, layer, conv_w, conv_b, layout, *, n_ctx_blocks, line_ctx, line_lat):
    n_b, tt, d = h.shape
    total = w_perm.shape[-1]
    conv_cols = conv_w.shape[1]
    ctx_row = mods.shape[0] - 1
    n_blocks = tt // TOKEN_BLOCK
    col_chunk = 1024
    nb = PROJ_BATCH if n_b % PROJ_BATCH == 0 else 1

    main = total - LANES
    kern = functools.partial(_proj_kernel, segments=_proj_segments(layout, col_chunk), main_cols=main,
                             n_ctx_blocks=n_ctx_blocks, line_ctx=line_ctx, line_lat=line_lat)
    return pl.pallas_call(
        kern,
        out_shape=(jax.ShapeDtypeStruct((n_b, tt, main), BF16), jax.ShapeDtypeStruct((n_b, tt, LANES), F32)),
        grid=(n_b // nb, n_blocks),
        in_specs=[
            pl.BlockSpec((nb, TOKEN_BLOCK, d), lambda b, t: (b, t, 0)),
            pl.BlockSpec((nb, N_MOD, d), lambda b, t: (b, 0, 0)),
            pl.BlockSpec((None, N_MOD, d), lambda b, t: (ctx_row, 0, 0)),
            _stacked(w_perm, (layer,)),
            _resident((CONV_W, conv_cols), lambda b, t: (0, 0)),
            _resident((1, conv_cols), lambda b, t: (0, 0)),
        ],
        out_specs=(pl.BlockSpec((nb, TOKEN_BLOCK, main), lambda b, t: (b, t, 0)),
                   pl.BlockSpec((nb, TOKEN_BLOCK, LANES), lambda b, t: (b, t, 0))),
        scratch_shapes=[pltpu.VMEM((nb * (col_chunk // LANES), TOKEN_BLOCK + 2 * SUBLANES, LANES), F32)],
        compiler_params=_cparams(2),
        name="mixer_in_proj",
    )(h, mods, mods, w_perm, conv_w, conv_b)


def _ssd_chunk(fwd, x_ref, b_ref, c_ref, sm_ref, dtb_ref, alog_ref, exp_ref, y_ref, state_ref, n_heads):
    q = x_ref.shape[0]
    hg = n_heads // SSD_GROUPS
    gw = hg * SSD_HEAD_DIM
    sm = sm_ref[...]
    dt_raw = jnp.where(fwd, sm[:, 0:n_heads], sm[:, n_heads:2 * n_heads])
    dt_bias = jnp.where(fwd, dtb_ref[0:1, :], dtb_ref[1:2, :])
    a_log = jnp.where(fwd, alog_ref[0:1, :], alog_ref[1:2, :])
    dt = _softplus(dt_raw + dt_bias)
    a = dt * (-jnp.exp(a_log))
    row = lax.broadcasted_iota(jnp.int32, (q, q), 0)
    col = lax.broadcasted_iota(jnp.int32, (q, q), 1)
    mask = (col - row) * jnp.where(fwd, 1, -1) <= 0
    tri = jnp.where(mask, 1.0, 0.0).astype(BF16)
    cs = _exact_lhs_dot(tri, a, 3)
    tot = jnp.sum(a, axis=0, keepdims=True)
    cs2 = cs * math.log2(math.e)
    src2 = cs2 - jnp.log2(dt)
    pad = jnp.zeros((q, LANES - n_heads), F32)
    src2_t = jnp.concatenate([src2, pad], axis=1).T[0:n_heads]

    w_state = dt * jnp.exp(tot - cs)
    from_start = jnp.exp(cs)
    e_tot = jnp.broadcast_to(jnp.exp(tot), (SUBLANES, n_heads))
    wide = _exact_rhs_dot(jnp.concatenate([w_state, from_start, e_tot], axis=0), exp_ref[...])
    w_state_w = wide[0:q]
    from_start_w = wide[q:2 * q]
    e_tot_w = wide[2 * q:2 * q + 1]

    x_bf = x_ref[...]
    xw_bf = (x_bf.astype(F32) * w_state_w).astype(BF16)
    pair_w = 2 * SSD_HEAD_DIM
    lane = lax.broadcasted_iota(jnp.int32, (1, pair_w), 1)
    keep_lo = jnp.where(lane < SSD_HEAD_DIM, 1.0, 0.0).astype(BF16)
    keep_hi = jnp.where(lane < SSD_HEAD_DIM, 0.0, 1.0).astype(BF16)
    for g in range(SSD_GROUPS):
        bt = b_ref[:, g * SSD_STATE:(g + 1) * SSD_STATE].T
        cg = c_ref[:, g * SSD_STATE:(g + 1) * SSD_STATE]
        scores = _dot(cg, bt)
        s_prev = state_ref[g]
        y_off = _dot(cg, s_prev.astype(BF16)) * from_start_w[:, g * gw:(g + 1) * gw]
        for pp in range(hg // 2):
            ms = []
            for head in (g * hg + 2 * pp, g * hg + 2 * pp + 1):
                seg2 = jnp.where(mask, cs2[:, head:head + 1] - src2_t[head:head + 1, :], -1e30)
                ms.append((scores * jnp.exp2(seg2)).astype(BF16))
            cols = slice(g * gw + pp * pair_w, g * gw + (pp + 1) * pair_w)
            xp = x_bf[:, cols]
            y_pair = _dot(jnp.concatenate(ms, axis=1), jnp.concatenate([xp * keep_lo, xp * keep_hi], axis=0))
            y_ref[:, cols] = (y_off[:, pp * pair_w:(pp + 1) * pair_w] + y_pair).astype(y_ref.dtype)
        upd = _dot(bt, xw_bf[:, g * gw:(g + 1) * gw])
        state_ref[g] = s_prev * e_tot_w[:, g * gw:(g + 1) * gw] + upd


def _scan_col_spec(layout, name, nb, rows, n_ctx, n_tot):
    off, w = layout[name]
    assert off % w == 0
    blk = off // w
    return pl.BlockSpec((nb, rows, w), lambda b, r, i: (b, _scan_order(i, r, n_ctx, n_tot), blk))


def _scan_small_spec(nb, rows, n_ctx, n_tot):
    return pl.BlockSpec((nb, rows, LANES), lambda b, r, i: (b, _scan_order(i, r, n_ctx, n_tot), 0))


def _scan_out_spec(nb, rows, width, n_ctx, n_tot):
    return pl.BlockSpec((None, nb, rows, width), lambda b, r, i: (r, b, _scan_order(i, r, n_ctx, n_tot), 0))


def _lru_scan_block(a_ref, b_ref, o_ref, carry_ref, reverse):
    rows, width = a_ref.shape
    sub = lax.broadcasted_iota(jnp.int32, (SUBLANES, width), 0)
    carry = carry_ref[...]
    n_groups = rows // SUBLANES
    order = range(n_groups - 1, -1, -1) if reverse else range(n_groups)
    for gi in order:
        sl = slice(gi * SUBLANES, (gi + 1) * SUBLANES)
        a = a_ref[sl, :]
        b = b_ref[sl, :]
        k = 1
        while k < SUBLANES:
            if reverse:
                a_sh = pltpu.roll(a, SUBLANES - k, axis=0)
                b_sh = pltpu.roll(b, SUBLANES - k, axis=0)
                ok = sub < SUBLANES - k
            else:
                a_sh = pltpu.roll(a, k, axis=0)
                b_sh = pltpu.roll(b, k, axis=0)
                ok = sub >= k
            b = b + a * jnp.where(ok, b_sh, 0.0)
            a = a * jnp.where(ok, a_sh, 1.0)
            k *= 2
        hblk = b + a * carry
        b_ref[sl, :] = hblk
        carry = hblk[0:1, :] if reverse else hblk[SUBLANES - 1:SUBLANES, :]
    carry_ref[...] = carry
    o_ref[...] = b_ref[...].astype(o_ref.dtype)


def _lru_kernel(x_ref, w_ref, ba_ref, bx_ref, lam_ref, h_ref, a_s, b_s, carry_ref):
    rev = pl.program_id(1)
    i = pl.program_id(2)
    nb = x_ref.shape[0]

    @pl.when(i == 0)
    def _():
        carry_ref[...] = jnp.zeros_like(carry_ref)

    n_tiles = w_ref.shape[0]
    neg_c_sp = (-LRU_C) * _softplus(-lam_ref[...])
    for bi in range(nb):
        x_bf = x_ref[bi]
        pre_a, pre_x = [], []
        for t in range(n_tiles):
            y = _dot(x_bf[:, t * MXU_TILE:(t + 1) * MXU_TILE], w_ref[t])
            pre_a.append(y[:, :MXU_TILE])
            pre_x.append(y[:, MXU_TILE:])
        r = _sigmoid(jnp.concatenate(pre_a, axis=1) + ba_ref[...])
        ig = _sigmoid(jnp.concatenate(pre_x, axis=1) + bx_ref[...])
        log_a = r * neg_c_sp
        a = jnp.exp(log_a)
        a_s[bi] = a
        b_s[bi] = jnp.sqrt(-jnp.tanh(log_a) * (a * a + 1.0)) * (ig * x_bf.astype(F32))

    @pl.when(rev == 0)
    def _():
        for bi in range(nb):
            _lru_scan_block(a_s.at[bi], b_s.at[bi], h_ref.at[bi], carry_ref.at[bi], reverse=False)

    @pl.when(rev == 1)
    def _():
        for bi in range(nb):
            _lru_scan_block(a_s.at[bi], b_s.at[bi], h_ref.at[bi], carry_ref.at[bi], reverse=True)


def _lru_call(p, layout, w_tiles, layer, b_a, b_x, lam, *, n_ctx_blocks):
    n_b, tt, _ = p.shape
    width = layout["lru_x"][1]
    n_blocks = tt // TOKEN_BLOCK
    n_tiles = w_tiles.shape[2]
    nb = SCAN_BATCH if n_b % SCAN_BATCH == 0 else 1

    return pl.pallas_call(
        _lru_kernel,
        out_shape=jax.ShapeDtypeStruct((2, n_b, tt, width), BF16),
        grid=(n_b // nb, 2, n_blocks),
        in_specs=[
            _scan_col_spec(layout, "lru_x", nb, TOKEN_BLOCK, n_ctx_blocks, n_blocks),
            pl.BlockSpec((None, None, n_tiles, MXU_TILE, 2 * MXU_TILE), lambda b, r, i: (layer, r, 0, 0, 0)),
            pl.BlockSpec((None, 1, width), lambda b, r, i: (r, 0, 0)),
            pl.BlockSpec((None, 1, width), lambda b, r, i: (r, 0, 0)),
            pl.BlockSpec((None, 1, width), lambda b, r, i: (r, 0, 0)),
        ],
        out_specs=_scan_out_spec(nb, TOKEN_BLOCK, width, n_ctx_blocks, n_blocks),
        scratch_shapes=[pltpu.VMEM((nb, TOKEN_BLOCK, width), F32), pltpu.VMEM((nb, TOKEN_BLOCK, width), F32),
                        pltpu.VMEM((nb, 1, width), F32)],
        compiler_params=_cparams(3),
        name="rglru_scan",
    )(p, w_tiles, b_a, b_x, lam)


def _gla_chunk(fwd, q_ref, k_ref, v_ref, sm_ref, wg_ref, bg_ref, o_ref, state_ref, alr_off, dk, dv):
    q_len = q_ref.shape[0]
    n_heads = q_ref.shape[1] // dk
    sm = sm_ref[...]
    r = GLA_GATE_RANK
    a_lr = jnp.where(fwd, sm[:, alr_off:alr_off + r], sm[:, alr_off + r:alr_off + 2 * r])
    z = _f32_dot_small_k(a_lr, wg_ref[...]) + bg_ref[...]
    log_a = _log_sigmoid(z) * (1.0 / GLA_TAU)
    row = lax.broadcasted_iota(jnp.int32, (q_len, q_len), 0)
    col = lax.broadcasted_iota(jnp.int32, (q_len, q_len), 1)
    mask = (col - row) * jnp.where(fwd, 1, -1) <= 0
    tri = jnp.where(mask, 1.0, 0.0).astype(BF16)
    bcum = _exact_lhs_dot(tri, log_a, 2)
    btot = jnp.sum(log_a, axis=0, keepdims=True)
    bmid = bcum[q_len // 2:q_len // 2 + 1, :]
    q_in = q_ref[...].astype(F32) * (dk ** -0.5) * jnp.exp(bcum - bmid)
    k_in = k_ref[...].astype(F32) * jnp.exp(bmid - bcum)
    q_off = (q_in * jnp.exp(bmid)).astype(BF16)
    k_st = (k_in * jnp.exp(btot - bmid)).astype(BF16)
    q_in = q_in.astype(BF16)
    k_in = k_in.astype(BF16)
    e_tot = jnp.exp(btot)
    for hh in range(n_heads):
        ks = slice(hh * dk, (hh + 1) * dk)
        vs = slice(hh * dv, (hh + 1) * dv)
        v_bf = v_ref[:, vs]
        att = lax.dot_general(q_in[:, ks], k_in[:, ks], (((1,), (1,)), ((), ())), preferred_element_type=F32)
        att = jnp.where(mask, att, 0.0).astype(BF16)
        s_prev = state_ref[hh]
        inter = lax.dot_general(q_off[:, ks], s_prev.astype(BF16), (((1,), (1,)), ((), ())),
                                preferred_element_type=F32)
        o_ref[:, vs] = (_dot(att, v_bf) + inter).astype(o_ref.dtype)
        upd = lax.dot_general(v_bf, k_st[:, ks], (((0,), (0,)), ((), ())), preferred_element_type=F32)
        state_ref[hh] = s_prev * e_tot[:, ks] + upd


def _chunk_scans_kernel(x_ref, b_ref, c_ref, q_ref, k_ref, v_ref, sm_ref, dtb_ref, alog_ref, exp_ref, wg_ref, bg_ref,
                        y_ref, o_ref, ssd_state, gla_state, *, n_heads, alr_off, dk, dv):
    rev = pl.program_id(1)
    i = pl.program_id(2)

    @pl.when(i == 0)
    def _():
        ssd_state[...] = jnp.zeros_like(ssd_state)
        gla_state[...] = jnp.zeros_like(gla_state)

    for bi in range(x_ref.shape[0]):
        _ssd_chunk(rev == 0, x_ref.at[bi], b_ref.at[bi], c_ref.at[bi], sm_ref.at[bi], dtb_ref, alog_ref, exp_ref,
                   y_ref.at[bi], ssd_state.at[bi], n_heads)
        _gla_chunk(rev == 0, q_ref.at[bi], k_ref.at[bi], v_ref.at[bi], sm_ref.at[bi], wg_ref, bg_ref,
                   o_ref.at[bi], gla_state.at[bi], alr_off, dk, dv)


def _chunk_scans_call(p, small, layout, dt_bias, a_log, expand, w_gate, b_gate, *, alr_off, n_ctx_chunks):
    n_b, tt, _ = p.shape
    n_heads = dt_bias.shape[1]
    ssd_w = n_heads * SSD_HEAD_DIM
    hg = n_heads // SSD_GROUPS
    key_w = layout["gla_q"][1]
    val_w = layout["gla_v"][1]
    dk = key_w // GLA_HEADS
    dv = val_w // GLA_HEADS
    n_chunks = tt // SCAN_CHUNK
    nb = SCAN_BATCH if n_b % SCAN_BATCH == 0 else 1
    col_spec = functools.partial(_scan_col_spec, layout, nb=nb, rows=SCAN_CHUNK, n_ctx=n_ctx_chunks, n_tot=n_chunks)

    kern = functools.partial(_chunk_scans_kernel, n_heads=n_heads, alr_off=alr_off, dk=dk, dv=dv)
    return pl.pallas_call(
        kern,
        out_shape=(jax.ShapeDtypeStruct((2, n_b, tt, ssd_w), BF16), jax.ShapeDtypeStruct((2, n_b, tt, val_w), BF16)),
        grid=(n_b // nb, 2, n_chunks),
        in_specs=[
            col_spec("ssd_x"), col_spec("ssd_b"), col_spec("ssd_c"),
            col_spec("gla_q"), col_spec("gla_k"), col_spec("gla_v"),
            _scan_small_spec(nb, SCAN_CHUNK, n_ctx_chunks, n_chunks),
            _resident((2, n_heads), lambda b, r, i: (0, 0)),
            _resident((2, n_heads), lambda b, r, i: (0, 0)),
            _resident(expand.shape, lambda b, r, i: (0, 0)),
            pl.BlockSpec((None, GLA_GATE_RANK, key_w), lambda b, r, i: (r, 0, 0)),
            pl.BlockSpec((None, 1, key_w), lambda b, r, i: (r, 0, 0)),
        ],
        out_specs=(_scan_out_spec(nb, SCAN_CHUNK, ssd_w, n_ctx_chunks, n_chunks),
                   _scan_out_spec(nb, SCAN_CHUNK, val_w, n_ctx_chunks, n_chunks)),
        scratch_shapes=[pltpu.VMEM((nb, SSD_GROUPS, SSD_STATE, hg * SSD_HEAD_DIM), F32),
                        pltpu.VMEM((nb, GLA_HEADS, dv, dk), F32)],
        compiler_params=_cparams(3),
        name="ssd_gla_scan",
    )(p, p, p, p, p, p, small, dt_bias, a_log, expand, w_gate, b_gate)


def _merge_kernel(h_ref, m_ref, mctx_ref, ys0_ref, ys1_ref, xs_ref, z_ref, hl0_ref, hl1_ref, gl_ref,
                  og0_ref, og1_ref, gg_ref, gt0_ref, gt1_ref, gt2_ref, dskip_ref, sng_ref, gng_ref,
                  wbr_ref, wout_ref, lng_ref, lnb_ref, o_ref, *, alpha, dv, n_ctx_blocks, skip_blocks):
    is_ctx = pl.program_id(1) + skip_blocks < n_ctx_blocks
    for bi in range(h_ref.shape[0]):
        gate = jnp.where(is_ctx, mctx_ref[5:6, :], m_ref[bi, 5:6, :])
        _merge_block(h_ref.at[bi], gate, ys0_ref.at[bi], ys1_ref.at[bi], xs_ref.at[bi], z_ref.at[bi],
                     hl0_ref.at[bi], hl1_ref.at[bi], gl_ref.at[bi], og0_ref.at[bi], og1_ref.at[bi], gg_ref.at[bi],
                     gt0_ref.at[bi], gt1_ref.at[bi], gt2_ref.at[bi], dskip_ref, sng_ref, gng_ref, wbr_ref, wout_ref,
                     lng_ref, lnb_ref, o_ref.at[bi], alpha, dv)


def _merge_block(h_ref, gate, ys0_ref, ys1_ref, xs_ref, z_ref, hl0_ref, hl1_ref, gl_ref, og0_ref, og1_ref, gg_ref,
                 gt0_ref, gt1_ref, gt2_ref, dskip_ref, sng_ref, gng_ref, wbr_ref, wout_ref, lng_ref, lnb_ref, o_ref,
                 alpha, dv):
    d = h_ref.shape[1]

    def f32(ref, cols=slice(None)):
        return ref[:, cols].astype(F32)

    y = (f32(ys0_ref) + f32(ys1_ref) + dskip_ref[...] * f32(xs_ref)) * f32(z_ref)
    y = y * lax.rsqrt(jnp.mean(y * y, axis=-1, keepdims=True) + NORM_EPS) * sng_ref[...]
    m = f32(gt0_ref) * _dot(y.astype(BF16), wbr_ref[0])
    y = (f32(hl0_ref) + f32(hl1_ref)) * f32(gl_ref)
    m = m + f32(gt1_ref) * _dot(y.astype(BF16), wbr_ref[1])
    parts = []
    for hh in range(d // dv):
        vs = slice(hh * dv, (hh + 1) * dv)
        o = f32(og0_ref, vs) + f32(og1_ref, vs)
        o = o * lax.rsqrt(jnp.mean(o * o, axis=-1, keepdims=True) + NORM_EPS) * gng_ref[...]
        parts.append(o * f32(gg_ref, vs))
    y = jnp.concatenate(parts, axis=1)
    m = m + f32(gt2_ref) * _dot(y.astype(BF16), wbr_ref[2])
    out = _dot(m.astype(BF16), wout_ref[...])
    z = alpha * h_ref[...] + gate * out
    o_ref[...] = _layer_norm(z, lng_ref[...], lnb_ref[...])


def _merge_call(h, mods, p, layout, y_ssd, h_lru, o_gla, d_skip_w, ssd_norm_g, gla_norm_g, w_branch, w_out, layer,
                ln_g, ln_b, *, n_ctx_blocks, skip_blocks, alpha):
    n_b, tt, d = h.shape
    n_blocks = tt // TOKEN_BLOCK - skip_blocks
    ctx_row = mods.shape[0] - 1
    dv = gla_norm_g.shape[-1]
    nb = MERGE_BATCH if n_b % MERGE_BATCH == 0 else 1

    def tok(width, blk):
        return pl.BlockSpec((nb, TOKEN_BLOCK, width), lambda b, t: (b, t + skip_blocks, blk))

    def col_spec(name):
        off, w = layout[name]
        blk = off // w
        assert blk * w == off
        return tok(w, blk)

    def gate_spec(n):
        off = layout["gates"][0] + n * d
        assert off % d == 0
        return tok(d, off // d)

    def dir_spec(rv):
        return pl.BlockSpec((None, nb, TOKEN_BLOCK, d), lambda b, t: (rv, b, t + skip_blocks, 0))

    kern = functools.partial(_merge_kernel, alpha=alpha, dv=dv, n_ctx_blocks=n_ctx_blocks, skip_blocks=skip_blocks)
    return pl.pallas_call(
        kern,
        out_shape=jax.ShapeDtypeStruct((n_b, n_blocks * TOKEN_BLOCK, d), F32),
        grid=(n_b // nb, n_blocks),
        in_specs=[
            tok(d, 0),
            pl.BlockSpec((nb, N_MOD, d), lambda b, t: (b, 0, 0)),
            pl.BlockSpec((None, N_MOD, d), lambda b, t: (ctx_row, 0, 0)),
            dir_spec(0), dir_spec(1), col_spec("ssd_x"), col_spec("ssd_z"),
            dir_spec(0), dir_spec(1), col_spec("lru_g"),
            dir_spec(0), dir_spec(1), col_spec("gla_g"),
            gate_spec(0), gate_spec(1), gate_spec(2),
            _resident((1, d), lambda b, t: (0, 0)),
            _resident((1, d), lambda b, t: (0, 0)),
            _resident((1, dv), lambda b, t: (0, 0)),
            _stacked(w_branch, (layer,)),
            _stacked(w_out, (layer,)),
            _resident((1, d), lambda b, t: (0, 0)),
            _resident((1, d), lambda b, t: (0, 0)),
        ],
        out_specs=pl.BlockSpec((nb, TOKEN_BLOCK, d), lambda b, t: (b, t, 0)),
        compiler_params=_cparams(2),
        name="mixer_merge",
    )(h, mods, mods, y_ssd, y_ssd, p, p, h_lru, h_lru, p, o_gla, o_gla, p, p, p, p,
      d_skip_w, ssd_norm_g.reshape(1, d), gla_norm_g.reshape(1, dv), w_branch, w_out,
      ln_g.reshape(1, d), ln_b.reshape(1, d))


def _permute_w_in(w_in, d, ssd_heads, total):
    ssd_w = d
    bc_w = SSD_GROUPS * SSD_STATE
    gla_key = d // 2
    sizes = (ssd_w, ssd_w + 2 * bc_w, 2 * ssd_heads, d, d, gla_key, gla_key, d, d, 2 * GLA_GATE_RANK, N_BRANCH * d)
    offs = [0]
    for s in sizes:
        offs.append(offs[-1] + s)
    w16 = w_in.astype(BF16)
    z, xbc, dtr, lx, lg, gq, gk, gv, gg, alr, gates = [w16[..., offs[n]:offs[n + 1]] for n in range(len(sizes))]
    pad = jnp.zeros(w16.shape[:-1] + (LANES - dtr.shape[-1] - alr.shape[-1],), BF16)
    w = jnp.concatenate([xbc, lx, z, lg, gq, gk, gv, gg, gates, dtr, alr, pad], axis=-1)
    assert w.shape[-1] == total
    return w


def _lru_gate_tiles(w_a, w_x):
    per = MXU_TILE // LRU_BLOCK
    n_l, n_dir, nb, k, _ = w_a.shape
    n_tiles = nb // per

    def tiles(w):
        w = w.astype(BF16).reshape(n_l, n_dir, n_tiles, per, k, k)
        on_diag = jnp.eye(per, dtype=bool)[:, None, :, None]
        t = jnp.where(on_diag, w[:, :, :, :, :, None, :], jnp.zeros((), BF16))
        return t.reshape(n_l, n_dir, n_tiles, MXU_TILE, MXU_TILE)

    return jnp.concatenate([tiles(w_a), tiles(w_x)], axis=-1)


def _to_colmajor(h, n_ctx, rows):
    n_b, _, d = h.shape
    lat = h[:, n_ctx:].reshape(n_b, rows, GRID_W, d).transpose(0, 2, 1, 3).reshape(n_b, rows * GRID_W, d)
    return jnp.concatenate([h[:, :n_ctx], lat], axis=1)


def _to_raster(h, n_ctx, rows):
    n_b, _, d = h.shape
    lat = h[:, n_ctx:].reshape(n_b, GRID_W, rows, d).transpose(0, 2, 1, 3).reshape(n_b, rows * GRID_W, d)
    return jnp.concatenate([h[:, :n_ctx], lat], axis=1)


def kernel(x, c, ctx, c_ctx, w_ada, b_ada, ln_g, ln_b, ffn_w_up, ffn_w_down, w_in, ssd_conv_w, ssd_conv_b,
           ssd_dt_bias, ssd_a_log, ssd_d, ssd_norm_g, lru_conv_w, lru_conv_b, lru_w_a, lru_b_a, lru_w_x, lru_b_x,
           lru_lam, gla_w_gate, gla_b_gate, gla_norm_g, w_branch, w_out):
    n_b, t_lat, d = x.shape
    n_ctx = ctx.shape[1]
    depth = w_ada.shape[0]
    rows = t_lat // GRID_W
    ssd_heads = ssd_dt_bias.shape[-1]
    assert n_ctx % TOKEN_BLOCK == 0 and t_lat % TOKEN_BLOCK == 0
    for line in (n_ctx, GRID_W, rows):
        assert line & (line - 1) == 0 and TOKEN_BLOCK % min(line, TOKEN_BLOCK) == 0
    assert n_ctx <= TOKEN_BLOCK
    alpha = (2.0 * depth) ** 0.25
    n_ctx_blocks = n_ctx // TOKEN_BLOCK
    n_ctx_chunks = n_ctx // SCAN_CHUNK

    layout, total = _proj_layout(d, ssd_heads)
    alr_off = 2 * ssd_heads

    n_rows = -(-(n_b + 1) // SUBLANES) * SUBLANES
    cond = jnp.concatenate([c, jnp.zeros((n_rows - n_b - 1, d), F32), c_ctx[None, :]], axis=0)
    mods_all = _ada_call(cond, w_ada, b_ada).reshape(depth, n_rows, N_MOD, d)

    expand = jnp.repeat(jnp.eye(ssd_heads, dtype=BF16), SSD_HEAD_DIM, axis=1)
    expand = jnp.concatenate([expand] * 3, axis=0)
    wup_all = ffn_w_up.astype(BF16)
    wdn_all = ffn_w_down.astype(BF16)
    w_perm_all = _permute_w_in(w_in, d, ssd_heads, total)
    lru_tiles_all = _lru_gate_tiles(lru_w_a, lru_w_x)
    w_branch_all = w_branch.astype(BF16)
    w_out_all = w_out.astype(BF16)

    h = jnp.concatenate([ctx, x], axis=1)
    for l in range(depth):
        last = l == depth - 1
        col_major = l % 2 == 1
        mods = mods_all[l]

        if col_major:
            h = _to_colmajor(h, n_ctx, rows)
        h = _ffn_call(h, mods, wup_all, wdn_all, (l, 0), ln_g[l, 0], ln_b[l, 0], j=0, n_ctx_blocks=n_ctx_blocks,
                      alpha=alpha)

        conv_w = jnp.concatenate([ssd_conv_w[l], lru_conv_w[l]], axis=1)
        conv_b = jnp.concatenate([ssd_conv_b[l], lru_conv_b[l]], axis=0)[None, :]
        p, small = _proj_call(h, mods, w_perm_all, l, conv_w, conv_b, layout, n_ctx_blocks=n_ctx_blocks,
                              line_ctx=n_ctx, line_lat=rows if col_major else GRID_W)

        y_ssd, o_gla = _chunk_scans_call(p, small, layout, ssd_dt_bias[l], ssd_a_log[l], expand, gla_w_gate[l],
                                         gla_b_gate[l][:, None, :], alr_off=alr_off, n_ctx_chunks=n_ctx_chunks)
        h_lru = _lru_call(p, layout, lru_tiles_all, l, lru_b_a[l][:, None, :], lru_b_x[l][:, None, :],
                          lru_lam[l][:, None, :], n_ctx_blocks=n_ctx_blocks)

        d_skip_w = jnp.repeat(ssd_d[l, 0] + ssd_d[l, 1], SSD_HEAD_DIM)[None, :]
        skip_blocks = n_ctx_blocks if last else 0
        h = _merge_call(h, mods, p, layout, y_ssd, h_lru, o_gla, d_skip_w, ssd_norm_g[l], gla_norm_g[l],
                        w_branch_all, w_out_all, l, ln_g[l, 1], ln_b[l, 1],
                        n_ctx_blocks=n_ctx_blocks, skip_blocks=skip_blocks, alpha=alpha)
        n_ctx_now = 0 if last else n_ctx
        h = _ffn_call(h, mods, wup_all, wdn_all, (l, 1), ln_g[l, 2], ln_b[l, 2], j=2,
                      n_ctx_blocks=n_ctx_now // TOKEN_BLOCK, alpha=alpha)
        if col_major:
            h = _to_raster(h, n_ctx_now, rows)
    return h
```

```python
import functools
import math

import jax
import jax.numpy as jnp
from jax import lax
from jax.experimental import pallas as pl
from jax.experimental.pallas import tpu as pltpu

F32 = jnp.float32
BF16 = jnp.bfloat16

GRID_W = 64
CONV_W = 4
CONV_LEFT = 2
SSD_HEAD_DIM = 64
SSD_GROUPS = 4
SSD_STATE = 128
LRU_BLOCK = 64
LRU_C = 8.0
GLA_HEADS = 4
GLA_GATE_RANK = 16
GLA_TAU = 16.0
N_MOD = 9
N_BRANCH = 3
FFN_RES_W = 0.5
NORM_EPS = 1e-5

TOKEN_BLOCK = 256
SCAN_CHUNK = 128
SCAN_BATCH = 4
TOKEN_BATCH = 4
MERGE_BATCH = 2
PROJ_BATCH = 1
MXU_TILE = 256
LANES = 128
SUBLANES = 8
VMEM_LIMIT = 56 * 1024 * 1024


def _cparams(n_axes):
    return pltpu.CompilerParams(dimension_semantics=("arbitrary",) * n_axes,
                                vmem_limit_bytes=VMEM_LIMIT)


def _resident(block_shape, index_map):
    return pl.BlockSpec(block_shape, index_map, pipeline_mode=pl.Buffered(1))


def _stacked(arr, lead):
    n = len(lead)
    tail = tuple(arr.shape[n:])
    index = tuple(lead) + (0,) * len(tail)
    return pl.BlockSpec((None,) * n + tail, lambda *_: index, pipeline_mode=pl.Buffered(1))


def _sigmoid(x):
    return 0.5 * jnp.tanh(0.5 * x) + 0.5


def _silu(x):
    return x * _sigmoid(x)


def _softplus(x):
    return jnp.maximum(x, 0.0) + jnp.log1p(jnp.exp(-jnp.abs(x)))


def _log_sigmoid(x):
    return jnp.minimum(x, 0.0) - jnp.log(1.0 + jnp.exp(-jnp.abs(x)))


def _split_terms(x, n):
    terms = []
    for _ in range(n - 1):
        t = x.astype(BF16)
        terms.append(t)
        x = x - t.astype(F32)
    terms.append(x.astype(BF16))
    return terms


_dot = functools.partial(jnp.dot, preferred_element_type=F32)


def _exact_lhs_dot(t_bf16, x, n_terms):
    return _dot(jnp.concatenate([t_bf16] * n_terms, axis=1), jnp.concatenate(_split_terms(x, n_terms), axis=0))


def _exact_rhs_dot(x, e3_bf16):
    return _dot(jnp.concatenate(_split_terms(x, 3), axis=1), e3_bf16)


def _f32_dot_small_k(a, w):
    at = _split_terms(a, 3)
    wt = _split_terms(w, 3)
    pairs = [(0, 0), (0, 1), (0, 2), (1, 0), (1, 1), (2, 0)]
    lhs = jnp.concatenate([at[i] for i, _ in pairs], axis=1)
    rhs = jnp.concatenate([wt[j] for _, j in pairs], axis=0)
    return _dot(lhs, rhs)


def _layer_norm(z, g, b):
    mu = jnp.mean(z, axis=-1, keepdims=True)
    zc = z - mu
    var = jnp.mean(zc * zc, axis=-1, keepdims=True)
    return zc * lax.rsqrt(var + NORM_EPS) * g + b


def _scan_order(i, rev, n_ctx, n_tot):
    back = jnp.where(i < n_ctx, n_ctx - 1 - i, n_ctx + n_tot - 1 - i)
    return jnp.where(rev == 0, i, back)


def _ada_kernel(s_ref, w_ref, b_ref, o_ref):
    s = _silu(s_ref[...])
    o_ref[...] = jnp.dot(s, w_ref[...], preferred_element_type=F32,
                         precision=lax.Precision.HIGHEST) + b_ref[...]


def _ada_call(cond, w_ada, b_ada):
    n_layers, d, nd = w_ada.shape
    rows = cond.shape[0]
    n_col = nd // d
    return pl.pallas_call(
        _ada_kernel,
        out_shape=jax.ShapeDtypeStruct((n_layers, rows, nd), F32),
        grid=(n_layers, n_col),
        in_specs=[
            pl.BlockSpec((rows, d), lambda l, j: (0, 0)),
            pl.BlockSpec((None, d, d), lambda l, j: (l, 0, j)),
            pl.BlockSpec((None, 1, d), lambda l, j: (l, 0, j)),
        ],
        out_specs=pl.BlockSpec((None, rows, d), lambda l, j: (l, 0, j)),
        compiler_params=_cparams(2),
        name="ada_mod",
    )(cond, w_ada, b_ada.reshape(n_layers, 1, nd))


def _grid_block_copies(grid_ref, buf_ref, sem, elem, bi, t, n_ctx_blocks, to_grid):
    n_lines = grid_ref.shape[1]
    lines_per_block = TOKEN_BLOCK // GRID_W
    ctx_lines = n_ctx_blocks * lines_per_block
    rows = n_lines - ctx_lines
    cols_per_block = TOKEN_BLOCK // rows

    def copy(grid_view, buf_view):
        src, dst = (buf_view, grid_view) if to_grid else (grid_view, buf_view)
        return pltpu.make_async_copy(src, dst, sem.at[bi])

    def ctx():
        return [copy(grid_ref.at[elem, t * lines_per_block + r], buf_ref.at[bi, pl.ds(r * GRID_W, GRID_W)])
                for r in range(lines_per_block)]

    def lat():
        col0 = (t - n_ctx_blocks) * cols_per_block
        return [copy(grid_ref.at[elem, pl.ds(ctx_lines, rows), col0 + c], buf_ref.at[bi, pl.ds(c * rows, rows)])
                for c in range(cols_per_block)]

    return ctx, lat


def _ffn_kernel(x_ref, m_ref, mctx_ref, wup_ref, wdn_ref, g_ref, b_ref, o_ref, *scratch, j, alpha, d_ff, ff_chunk,
                n_ctx_blocks, gather_in, scatter_out):
    b = pl.program_id(0)
    t = pl.program_id(1)
    is_ctx = t < n_ctx_blocks
    nb = m_ref.shape[0]
    buf, sem = scratch if (gather_in or scatter_out) else (None, None)

    def copies(bi):
        grid_ref = x_ref if gather_in else o_ref
        return _grid_block_copies(grid_ref, buf, sem, b * nb + bi, bi, t, n_ctx_blocks, to_grid=scatter_out)

    def for_block_kind(bi, action):
        ctx, lat = copies(bi)
        if n_ctx_blocks:
            @pl.when(is_ctx)
            def _():
                for cp in ctx():
                    action(cp)

        @pl.when(jnp.logical_not(is_ctx))
        def _():
            for cp in lat():
                action(cp)

    if gather_in:
        for bi in range(nb):
            for_block_kind(bi, lambda cp: cp.start())
    for bi in range(nb):
        if gather_in:
            for_block_kind(bi, lambda cp: cp.wait())
            x = buf[bi]
        else:
            x = x_ref[bi]
        m = jnp.where(is_ctx, mctx_ref[3 * j:3 * j + 3, :], m_ref[bi, 3 * j:3 * j + 3, :])
        shift, scale, gate = m[0:1], m[1:2], m[2:3]
        u = (x * (1.0 + scale) + shift).astype(BF16)
        acc = None
        for c0 in range(0, d_ff, ff_chunk):
            a = _dot(u, wup_ref[:, c0:c0 + ff_chunk])
            v = _dot(u, wup_ref[:, d_ff + c0:d_ff + c0 + ff_chunk])
            hid = (_silu(a) * v).astype(BF16)
            part = _dot(hid, wdn_ref[c0:c0 + ff_chunk, :])
            acc = part if acc is None else acc + part
        z = alpha * x + (FFN_RES_W * gate) * acc
        out = _layer_norm(z, g_ref[...], b_ref[...])
        if scatter_out:
            buf[bi] = out
            for_block_kind(bi, lambda cp: cp.start())
        else:
            o_ref[bi] = out
    if scatter_out:
        for bi in range(nb):
            for_block_kind(bi, lambda cp: cp.wait())


def _ffn_call(h, mods, w_up, w_dn, lead, ln_g, ln_b, *, j, n_ctx_blocks, alpha, gather_in=False, scatter_out=False):
    assert not (gather_in and scatter_out)
    n_b, tt, d = h.shape
    d_ff = w_dn.shape[-2]
    ff_chunk = d_ff // 2 if (d_ff // 2) % LANES == 0 else d_ff
    ctx_row = mods.shape[0] - 1
    n_blocks = tt // TOKEN_BLOCK
    nb = TOKEN_BATCH if n_b % TOKEN_BATCH == 0 else 1
    grid_view = (n_b, tt // GRID_W, GRID_W, d)
    block_spec = pl.BlockSpec((nb, TOKEN_BLOCK, d), lambda b, t: (b, t, 0))
    any_spec = pl.BlockSpec(memory_space=pl.ANY)
    scratch = []
    if gather_in or scatter_out:
        scratch = [pltpu.VMEM((nb, TOKEN_BLOCK, d), F32), pltpu.SemaphoreType.DMA((nb,))]

    kern = functools.partial(_ffn_kernel, j=j, alpha=alpha, d_ff=d_ff, ff_chunk=ff_chunk, n_ctx_blocks=n_ctx_blocks,
                             gather_in=gather_in, scatter_out=scatter_out)
    out = pl.pallas_call(
        kern,
        out_shape=jax.ShapeDtypeStruct(grid_view if scatter_out else h.shape, F32),
        grid=(n_b // nb, n_blocks),
        in_specs=[
            any_spec if gather_in else block_spec,
            pl.BlockSpec((nb, N_MOD, d), lambda b, t: (b, 0, 0)),
            pl.BlockSpec((None, N_MOD, d), lambda b, t: (ctx_row, 0, 0)),
            _stacked(w_up, lead),
            _stacked(w_dn, lead),
            _resident((1, d), lambda b, t: (0, 0)),
            _resident((1, d), lambda b, t: (0, 0)),
        ],
        out_specs=any_spec if scatter_out else block_spec,
        scratch_shapes=scratch,
        compiler_params=_cparams(2),
        name="ffn_sublayer",
    )(h.reshape(grid_view) if gather_in else h, mods, mods, w_up, w_dn, ln_g.reshape(1, d), ln_b.reshape(1, d))
    return out.reshape(h.shape) if scatter_out else out


def _proj_layout(d, ssd_heads):
    ssd_w = d
    bc_w = SSD_GROUPS * SSD_STATE
    gla_key = d // 2
    names = [("ssd_x", ssd_w), ("ssd_b", bc_w), ("ssd_c", bc_w), ("lru_x", d), ("ssd_z", ssd_w),
             ("lru_g", d), ("gla_q", gla_key), ("gla_k", gla_key), ("gla_v", d), ("gla_g", d),
             ("gates", N_BRANCH * d), ("small", LANES)]
    off, out = 0, {}
    for name, width in names:
        out[name] = (off, width)
        off += width
    return out, off


_ACTIVATIONS = {"none": lambda v: v, "silu": _silu, "sigmoid": _sigmoid, "gelu": jax.nn.gelu}


def _proj_kernel(x_ref, m_ref, mctx_ref, w_ref, cw_ref, cb_ref, o_ref, sm_ref, ybuf, *, segments, main_cols,
                 n_ctx_blocks, line_ctx, line_lat):
    is_ctx = pl.program_id(1) < n_ctx_blocks
    line = jnp.where(is_ctx, line_ctx, line_lat)
    n_strips = ybuf.shape[0] // x_ref.shape[0]
    for bi in range(x_ref.shape[0]):
        m = jnp.where(is_ctx, mctx_ref[3:5, :], m_ref[bi, 3:5, :])
        _proj_block(x_ref.at[bi], m[0:1], m[1:2], w_ref, cw_ref, cb_ref, o_ref.at[bi], sm_ref.at[bi],
                    ybuf.at[bi * n_strips:(bi + 1) * n_strips], line, segments, main_cols)


def _proj_block(x_ref, shift, scale, w_ref, cw_ref, cb_ref, o_ref, sm_ref, ybuf, line, segments, main_cols):
    x = x_ref[...]
    u = (x * (1.0 + scale) + shift).astype(BF16)
    rows = x.shape[0]
    halo = SUBLANES
    sm_ref[...] = _dot(u, w_ref[:, main_cols:main_cols + LANES])
    pos = lax.broadcasted_iota(jnp.int32, (rows, LANES), 0) & (line - 1)
    taps = [(k, k - CONV_LEFT) for k in range(CONV_W) if k != CONV_LEFT]
    valid = {off: (pos + off >= 0) & (pos + off < line) for _, off in taps}
    zeros = jnp.zeros((halo, LANES), F32)
    for c0, cw, conv, act in segments:
        y = _dot(u, w_ref[:, c0:c0 + cw])
        if not conv:
            o_ref[:, c0:c0 + cw] = _ACTIVATIONS[act](y).astype(o_ref.dtype)
            continue
        for j in range(cw // LANES):
            ybuf[j, 0:halo, :] = zeros
            ybuf[j, halo:halo + rows, :] = y[:, j * LANES:(j + 1) * LANES]
            ybuf[j, halo + rows:2 * halo + rows, :] = zeros
        for j in range(cw // LANES):
            cols = slice(c0 + j * LANES, c0 + (j + 1) * LANES)
            acc = cb_ref[:, cols] + ybuf[j, halo:halo + rows, :] * cw_ref[CONV_LEFT:CONV_LEFT + 1, cols]
            for k, off in taps:
                shifted = ybuf[j, halo + off:halo + off + rows, :]
                acc = acc + jnp.where(valid[off], shifted, 0.0) * cw_ref[k:k + 1, cols]
            o_ref[:, cols] = _ACTIVATIONS[act](acc).astype(o_ref.dtype)


def _proj_segments(layout, col_chunk):
    kinds = {"ssd_x": (True, "silu"), "ssd_b": (True, "silu"), "ssd_c": (True, "silu"), "lru_x": (True, "none"),
             "ssd_z": (False, "silu"), "lru_g": (False, "gelu"), "gla_q": (False, "none"), "gla_k": (False, "none"),
             "gla_v": (False, "none"), "gla_g": (False, "silu"), "gates": (False, "sigmoid")}
    segs = []
    for name, (conv, act) in kinds.items():
        off, width = layout[name]
        for c0 in range(off, off + width, col_chunk):
            segs.append((c0, min(col_chunk, off + width - c0), conv, act))
    return tuple(segs)


def _proj_call(h, mods, w_perm, layer, conv_w, conv_b, layout, *, n_ctx_blocks, line_ctx, line_lat):
    n_b, tt, d = h.shape
    total = w_perm.shape[-1]
    conv_cols = conv_w.shape[1]
    ctx_row = mods.shape[0] - 1
    n_blocks = tt // TOKEN_BLOCK
    col_chunk = 1024
    nb = PROJ_BATCH if n_b % PROJ_BATCH == 0 else 1

    main = total - LANES
    kern = functools.partial(_proj_kernel, segments=_proj_segments(layout, col_chunk), main_cols=main,
                             n_ctx_blocks=n_ctx_blocks, line_ctx=line_ctx, line_lat=line_lat)
    return pl.pallas_call(
        kern,
        out_shape=(jax.ShapeDtypeStruct((n_b, tt, main), BF16), jax.ShapeDtypeStruct((n_b, tt, LANES), F32)),
        grid=(n_b // nb, n_blocks),
        in_specs=[
            pl.BlockSpec((nb, TOKEN_BLOCK, d), lambda b, t: (b, t, 0)),
            pl.BlockSpec((nb, N_MOD, d), lambda b, t: (b, 0, 0)),
            pl.BlockSpec((None, N_MOD, d), lambda b, t: (ctx_row, 0, 0)),
            _stacked(w_perm, (layer,)),
            _resident((CONV_W, conv_cols), lambda b, t: (0, 0)),
            _resident((1, conv_cols), lambda b, t: (0, 0)),
        ],
        out_specs=(pl.BlockSpec((nb, TOKEN_BLOCK, main), lambda b, t: (b, t, 0)),
                   pl.BlockSpec((nb, TOKEN_BLOCK, LANES), lambda b, t: (b, t, 0))),
        scratch_shapes=[pltpu.VMEM((nb * (col_chunk // LANES), TOKEN_BLOCK + 2 * SUBLANES, LANES), F32)],
        compiler_params=_cparams(2),
        name="mixer_in_proj",
    )(h, mods, mods, w_perm, conv_w, conv_b)


def _ssd_chunk(fwd, x_ref, b_ref, c_ref, sm_ref, dtb_ref, alog_ref, exp_ref, y_ref, state_ref, n_heads):
    q = x_ref.shape[0]
    hg = n_heads // SSD_GROUPS
    gw = hg * SSD_HEAD_DIM
    sm = sm_ref[...]
    dt_raw = jnp.where(fwd, sm[:, 0:n_heads], sm[:, n_heads:2 * n_heads])
    dt_bias = jnp.where(fwd, dtb_ref[0:1, :], dtb_ref[1:2, :])
    a_log = jnp.where(fwd, alog_ref[0:1, :], alog_ref[1:2, :])
    dt = _softplus(dt_raw + dt_bias)
    a = dt * (-jnp.exp(a_log))
    row = lax.broadcasted_iota(jnp.int32, (q, q), 0)
    col = lax.broadcasted_iota(jnp.int32, (q, q), 1)
    mask = (col - row) * jnp.where(fwd, 1, -1) <= 0
    tri = jnp.where(mask, 1.0, 0.0).astype(BF16)
    cs = _exact_lhs_dot(tri, a, 3)
    tot = jnp.sum(a, axis=0, keepdims=True)
    cs2 = cs * math.log2(math.e)
    src2 = cs2 - jnp.log2(dt)
    pad = jnp.zeros((q, LANES - n_heads), F32)
    src2_t = jnp.concatenate([src2, pad], axis=1).T[0:n_heads]

    w_state = dt * jnp.exp(tot - cs)
    from_start = jnp.exp(cs)
    e_tot = jnp.broadcast_to(jnp.exp(tot), (SUBLANES, n_heads))
    wide = _exact_rhs_dot(jnp.concatenate([w_state, from_start, e_tot], axis=0), exp_ref[...])
    w_state_w = wide[0:q]
    from_start_w = wide[q:2 * q]
    e_tot_w = wide[2 * q:2 * q + 1]

    x_bf = x_ref[...]
    xw_bf = (x_bf.astype(F32) * w_state_w).astype(BF16)
    pair_w = 2 * SSD_HEAD_DIM
    lane = lax.broadcasted_iota(jnp.int32, (1, pair_w), 1)
    keep_lo = jnp.where(lane < SSD_HEAD_DIM, 1.0, 0.0).astype(BF16)
    keep_hi = jnp.where(lane < SSD_HEAD_DIM, 0.0, 1.0).astype(BF16)
    for g in range(SSD_GROUPS):
        bt = b_ref[:, g * SSD_STATE:(g + 1) * SSD_STATE].T
        cg = c_ref[:, g * SSD_STATE:(g + 1) * SSD_STATE]
        scores = _dot(cg, bt)
        s_prev = state_ref[g]
        y_off = _dot(cg, s_prev.astype(BF16)) * from_start_w[:, g * gw:(g + 1) * gw]
        for pp in range(hg // 2):
            ms = []
            for head in (g * hg + 2 * pp, g * hg + 2 * pp + 1):
                seg2 = jnp.where(mask, cs2[:, head:head + 1] - src2_t[head:head + 1, :], -1e30)
                ms.append((scores * jnp.exp2(seg2)).astype(BF16))
            cols = slice(g * gw + pp * pair_w, g * gw + (pp + 1) * pair_w)
            xp = x_bf[:, cols]
            y_pair = _dot(jnp.concatenate(ms, axis=1), jnp.concatenate([xp * keep_lo, xp * keep_hi], axis=0))
            y_ref[:, cols] = (y_off[:, pp * pair_w:(pp + 1) * pair_w] + y_pair).astype(y_ref.dtype)
        upd = _dot(bt, xw_bf[:, g * gw:(g + 1) * gw])
        state_ref[g] = s_prev * e_tot_w[:, g * gw:(g + 1) * gw] + upd


def _scan_col_spec(layout, name, nb, rows, n_ctx, n_tot):
    off, w = layout[name]
    assert off % w == 0
    blk = off // w
    return pl.BlockSpec((nb, rows, w), lambda b, r, i: (b, _scan_order(i, r, n_ctx, n_tot), blk))


def _scan_small_spec(nb, rows, n_ctx, n_tot):
    return pl.BlockSpec((nb, rows, LANES), lambda b, r, i: (b, _scan_order(i, r, n_ctx, n_tot), 0))


def _scan_out_spec(nb, rows, width, n_ctx, n_tot):
    return pl.BlockSpec((None, nb, rows, width), lambda b, r, i: (r, b, _scan_order(i, r, n_ctx, n_tot), 0))


def _lru_scan_block(a_ref, b_ref, o_ref, carry_ref, reverse):
    rows, width = a_ref.shape
    sub = lax.broadcasted_iota(jnp.int32, (SUBLANES, width), 0)
    carry = carry_ref[...]
    n_groups = rows // SUBLANES
    order = range(n_groups - 1, -1, -1) if reverse else range(n_groups)
    for gi in order:
        sl = slice(gi * SUBLANES, (gi + 1) * SUBLANES)
        a = a_ref[sl, :]
        b = b_ref[sl, :]
        k = 1
        while k < SUBLANES:
            if reverse:
                a_sh = pltpu.roll(a, SUBLANES - k, axis=0)
                b_sh = pltpu.roll(b, SUBLANES - k, axis=0)
                ok = sub < SUBLANES - k
            else:
                a_sh = pltpu.roll(a, k, axis=0)
                b_sh = pltpu.roll(b, k, axis=0)
                ok = sub >= k
            b = b + a * jnp.where(ok, b_sh, 0.0)
            a = a * jnp.where(ok, a_sh, 1.0)
            k *= 2
        hblk = b + a * carry
        b_ref[sl, :] = hblk
        carry = hblk[0:1, :] if reverse else hblk[SUBLANES - 1:SUBLANES, :]
    carry_ref[...] = carry
    o_ref[...] = b_ref[...].astype(o_ref.dtype)


def _lru_kernel(x_ref, w_ref, ba_ref, bx_ref, lam_ref, h_ref, a_s, b_s, carry_ref):
    rev = pl.program_id(1)
    i = pl.program_id(2)
    nb = x_ref.shape[0]

    @pl.when(i == 0)
    def _():
        carry_ref[...] = jnp.zeros_like(carry_ref)

    n_tiles = w_ref.shape[0]
    neg_c_sp = (-LRU_C) * _softplus(-lam_ref[...])
    for bi in range(nb):
        x_bf = x_ref[bi]
        pre_a, pre_x = [], []
        for t in range(n_tiles):
            y = _dot(x_bf[:, t * MXU_TILE:(t + 1) * MXU_TILE], w_ref[t])
            pre_a.append(y[:, :MXU_TILE])
            pre_x.append(y[:, MXU_TILE:])
        r = _sigmoid(jnp.concatenate(pre_a, axis=1) + ba_ref[...])
        ig = _sigmoid(jnp.concatenate(pre_x, axis=1) + bx_ref[...])
        log_a = r * neg_c_sp
        a = jnp.exp(log_a)
        a_s[bi] = a
        b_s[bi] = jnp.sqrt(-jnp.tanh(log_a) * (a * a + 1.0)) * (ig * x_bf.astype(F32))

    @pl.when(rev == 0)
    def _():
        for bi in range(nb):
            _lru_scan_block(a_s.at[bi], b_s.at[bi], h_ref.at[bi], carry_ref.at[bi], reverse=False)

    @pl.when(rev == 1)
    def _():
        for bi in range(nb):
            _lru_scan_block(a_s.at[bi], b_s.at[bi], h_ref.at[bi], carry_ref.at[bi], reverse=True)


def _lru_call(p, layout, w_tiles, layer, b_a, b_x, lam, *, n_ctx_blocks):
    n_b, tt, _ = p.shape
    width = layout["lru_x"][1]
    n_blocks = tt // TOKEN_BLOCK
    n_tiles = w_tiles.shape[2]
    nb = SCAN_BATCH if n_b % SCAN_BATCH == 0 else 1

    return pl.pallas_call(
        _lru_kernel,
        out_shape=jax.ShapeDtypeStruct((2, n_b, tt, width), BF16),
        grid=(n_b // nb, 2, n_blocks),
        in_specs=[
            _scan_col_spec(layout, "lru_x", nb, TOKEN_BLOCK, n_ctx_blocks, n_blocks),
            pl.BlockSpec((None, None, n_tiles, MXU_TILE, 2 * MXU_TILE), lambda b, r, i: (layer, r, 0, 0, 0)),
            pl.BlockSpec((None, 1, width), lambda b, r, i: (r, 0, 0)),
            pl.BlockSpec((None, 1, width), lambda b, r, i: (r, 0, 0)),
            pl.BlockSpec((None, 1, width), lambda b, r, i: (r, 0, 0)),
        ],
        out_specs=_scan_out_spec(nb, TOKEN_BLOCK, width, n_ctx_blocks, n_blocks),
        scratch_shapes=[pltpu.VMEM((nb, TOKEN_BLOCK, width), F32), pltpu.VMEM((nb, TOKEN_BLOCK, width), F32),
                        pltpu.VMEM((nb, 1, width), F32)],
        compiler_params=_cparams(3),
        name="rglru_scan",
    )(p, w_tiles, b_a, b_x, lam)


def _gla_chunk(fwd, q_ref, k_ref, v_ref, sm_ref, wg_ref, bg_ref, o_ref, state_ref, alr_off, dk, dv):
    q_len = q_ref.shape[0]
    n_heads = q_ref.shape[1] // dk
    sm = sm_ref[...]
    r = GLA_GATE_RANK
    a_lr = jnp.where(fwd, sm[:, alr_off:alr_off + r], sm[:, alr_off + r:alr_off + 2 * r])
    z = _f32_dot_small_k(a_lr, wg_ref[...]) + bg_ref[...]
    log_a = _log_sigmoid(z) * (1.0 / GLA_TAU)
    row = lax.broadcasted_iota(jnp.int32, (q_len, q_len), 0)
    col = lax.broadcasted_iota(jnp.int32, (q_len, q_len), 1)
    mask = (col - row) * jnp.where(fwd, 1, -1) <= 0
    tri = jnp.where(mask, 1.0, 0.0).astype(BF16)
    bcum = _exact_lhs_dot(tri, log_a, 2)
    btot = jnp.sum(log_a, axis=0, keepdims=True)
    bmid = bcum[q_len // 2:q_len // 2 + 1, :]
    q_in = q_ref[...].astype(F32) * (dk ** -0.5) * jnp.exp(bcum - bmid)
    k_in = k_ref[...].astype(F32) * jnp.exp(bmid - bcum)
    q_off = (q_in * jnp.exp(bmid)).astype(BF16)
    k_st = (k_in * jnp.exp(btot - bmid)).astype(BF16)
    q_in = q_in.astype(BF16)
    k_in = k_in.astype(BF16)
    e_tot = jnp.exp(btot)
    for hh in range(n_heads):
        ks = slice(hh * dk, (hh + 1) * dk)
        vs = slice(hh * dv, (hh + 1) * dv)
        v_bf = v_ref[:, vs]
        att = lax.dot_general(q_in[:, ks], k_in[:, ks], (((1,), (1,)), ((), ())), preferred_element_type=F32)
        att = jnp.where(mask, att, 0.0).astype(BF16)
        s_prev = state_ref[hh]
        inter = lax.dot_general(q_off[:, ks], s_prev.astype(BF16), (((1,), (1,)), ((), ())),
                                preferred_element_type=F32)
        o_ref[:, vs] = (_dot(att, v_bf) + inter).astype(o_ref.dtype)
        upd = lax.dot_general(v_bf, k_st[:, ks], (((0,), (0,)), ((), ())), preferred_element_type=F32)
        state_ref[hh] = s_prev * e_tot[:, ks] + upd


def _chunk_scans_kernel(x_ref, b_ref, c_ref, q_ref, k_ref, v_ref, sm_ref, dtb_ref, alog_ref, exp_ref, wg_ref, bg_ref,
                        y_ref, o_ref, ssd_state, gla_state, *, n_heads, alr_off, dk, dv):
    rev = pl.program_id(1)
    i = pl.program_id(2)

    @pl.when(i == 0)
    def _():
        ssd_state[...] = jnp.zeros_like(ssd_state)
        gla_state[...] = jnp.zeros_like(gla_state)

    for bi in range(x_ref.shape[0]):
        _ssd_chunk(rev == 0, x_ref.at[bi], b_ref.at[bi], c_ref.at[bi], sm_ref.at[bi], dtb_ref, alog_ref, exp_ref,
                   y_ref.at[bi], ssd_state.at[bi], n_heads)
        _gla_chunk(rev == 0, q_ref.at[bi], k_ref.at[bi], v_ref.at[bi], sm_ref.at[bi], wg_ref, bg_ref,
                   o_ref.at[bi], gla_state.at[bi], alr_off, dk, dv)


def _chunk_scans_call(p, small, layout, dt_bias, a_log, expand, w_gate, b_gate, *, alr_off, n_ctx_chunks):
    n_b, tt, _ = p.shape
    n_heads = dt_bias.shape[1]
    ssd_w = n_heads * SSD_HEAD_DIM
    hg = n_heads // SSD_GROUPS
    key_w = layout["gla_q"][1]
    val_w = layout["gla_v"][1]
    dk = key_w // GLA_HEADS
    dv = val_w // GLA_HEADS
    n_chunks = tt // SCAN_CHUNK
    nb = SCAN_BATCH if n_b % SCAN_BATCH == 0 else 1
    col_spec = functools.partial(_scan_col_spec, layout, nb=nb, rows=SCAN_CHUNK, n_ctx=n_ctx_chunks, n_tot=n_chunks)

    kern = functools.partial(_chunk_scans_kernel, n_heads=n_heads, alr_off=alr_off, dk=dk, dv=dv)
    return pl.pallas_call(
        kern,
        out_shape=(jax.ShapeDtypeStruct((2, n_b, tt, ssd_w), BF16), jax.ShapeDtypeStruct((2, n_b, tt, val_w), BF16)),
        grid=(n_b // nb, 2, n_chunks),
        in_specs=[
            col_spec("ssd_x"), col_spec("ssd_b"), col_spec("ssd_c"),
            col_spec("gla_q"), col_spec("gla_k"), col_spec("gla_v"),
            _scan_small_spec(nb, SCAN_CHUNK, n_ctx_chunks, n_chunks),
            _resident((2, n_heads), lambda b, r, i: (0, 0)),
            _resident((2, n_heads), lambda b, r, i: (0, 0)),
            _resident(expand.shape, lambda b, r, i: (0, 0)),
            pl.BlockSpec((None, GLA_GATE_RANK, key_w), lambda b, r, i: (r, 0, 0)),
            pl.BlockSpec((None, 1, key_w), lambda b, r, i: (r, 0, 0)),
        ],
        out_specs=(_scan_out_spec(nb, SCAN_CHUNK, ssd_w, n_ctx_chunks, n_chunks),
                   _scan_out_spec(nb, SCAN_CHUNK, val_w, n_ctx_chunks, n_chunks)),
        scratch_shapes=[pltpu.VMEM((nb, SSD_GROUPS, SSD_STATE, hg * SSD_HEAD_DIM), F32),
                        pltpu.VMEM((nb, GLA_HEADS, dv, dk), F32)],
        compiler_params=_cparams(3),
        name="ssd_gla_scan",
    )(p, p, p, p, p, p, small, dt_bias, a_log, expand, w_gate, b_gate)


def _merge_kernel(h_ref, m_ref, mctx_ref, ys0_ref, ys1_ref, xs_ref, z_ref, hl0_ref, hl1_ref, gl_ref,
                  og0_ref, og1_ref, gg_ref, gt0_ref, gt1_ref, gt2_ref, dskip_ref, sng_ref, gng_ref,
                  wbr_ref, wout_ref, lng_ref, lnb_ref, o_ref, *, alpha, dv, n_ctx_blocks, skip_blocks):
    is_ctx = pl.program_id(1) + skip_blocks < n_ctx_blocks
    for bi in range(h_ref.shape[0]):
        gate = jnp.where(is_ctx, mctx_ref[5:6, :], m_ref[bi, 5:6, :])
        _merge_block(h_ref.at[bi], gate, ys0_ref.at[bi], ys1_ref.at[bi], xs_ref.at[bi], z_ref.at[bi],
                     hl0_ref.at[bi], hl1_ref.at[bi], gl_ref.at[bi], og0_ref.at[bi], og1_ref.at[bi], gg_ref.at[bi],
                     gt0_ref.at[bi], gt1_ref.at[bi], gt2_ref.at[bi], dskip_ref, sng_ref, gng_ref, wbr_ref, wout_ref,
                     lng_ref, lnb_ref, o_ref.at[bi], alpha, dv)


def _merge_block(h_ref, gate, ys0_ref, ys1_ref, xs_ref, z_ref, hl0_ref, hl1_ref, gl_ref, og0_ref, og1_ref, gg_ref,
                 gt0_ref, gt1_ref, gt2_ref, dskip_ref, sng_ref, gng_ref, wbr_ref, wout_ref, lng_ref, lnb_ref, o_ref,
                 alpha, dv):
    d = h_ref.shape[1]

    def f32(ref, cols=slice(None)):
        return ref[:, cols].astype(F32)

    y = (f32(ys0_ref) + f32(ys1_ref) + dskip_ref[...] * f32(xs_ref)) * f32(z_ref)
    y = y * lax.rsqrt(jnp.mean(y * y, axis=-1, keepdims=True) + NORM_EPS) * sng_ref[...]
    m = f32(gt0_ref) * _dot(y.astype(BF16), wbr_ref[0])
    y = (f32(hl0_ref) + f32(hl1_ref)) * f32(gl_ref)
    m = m + f32(gt1_ref) * _dot(y.astype(BF16), wbr_ref[1])
    parts = []
    for hh in range(d // dv):
        vs = slice(hh * dv, (hh + 1) * dv)
        o = f32(og0_ref, vs) + f32(og1_ref, vs)
        o = o * lax.rsqrt(jnp.mean(o * o, axis=-1, keepdims=True) + NORM_EPS) * gng_ref[...]
        parts.append(o * f32(gg_ref, vs))
    y = jnp.concatenate(parts, axis=1)
    m = m + f32(gt2_ref) * _dot(y.astype(BF16), wbr_ref[2])
    out = _dot(m.astype(BF16), wout_ref[...])
    z = alpha * h_ref[...] + gate * out
    o_ref[...] = _layer_norm(z, lng_ref[...], lnb_ref[...])


def _merge_call(h, mods, p, layout, y_ssd, h_lru, o_gla, d_skip_w, ssd_norm_g, gla_norm_g, w_branch, w_out, layer,
                ln_g, ln_b, *, n_ctx_blocks, skip_blocks, alpha):
    n_b, tt, d = h.shape
    n_blocks = tt // TOKEN_BLOCK - skip_blocks
    ctx_row = mods.shape[0] - 1
    dv = gla_norm_g.shape[-1]
    nb = MERGE_BATCH if n_b % MERGE_BATCH == 0 else 1

    def tok(width, blk):
        return pl.BlockSpec((nb, TOKEN_BLOCK, width), lambda b, t: (b, t + skip_blocks, blk))

    def col_spec(name):
        off, w = layout[name]
        blk = off // w
        assert blk * w == off
        return tok(w, blk)

    def gate_spec(n):
        off = layout["gates"][0] + n * d
        assert off % d == 0
        return tok(d, off // d)

    def dir_spec(rv):
        return pl.BlockSpec((None, nb, TOKEN_BLOCK, d), lambda b, t: (rv, b, t + skip_blocks, 0))

    kern = functools.partial(_merge_kernel, alpha=alpha, dv=dv, n_ctx_blocks=n_ctx_blocks, skip_blocks=skip_blocks)
    return pl.pallas_call(
        kern,
        out_shape=jax.ShapeDtypeStruct((n_b, n_blocks * TOKEN_BLOCK, d), F32),
        grid=(n_b // nb, n_blocks),
        in_specs=[
            tok(d, 0),
            pl.BlockSpec((nb, N_MOD, d), lambda b, t: (b, 0, 0)),
            pl.BlockSpec((None, N_MOD, d), lambda b, t: (ctx_row, 0, 0)),
            dir_spec(0), dir_spec(1), col_spec("ssd_x"), col_spec("ssd_z"),
            dir_spec(0), dir_spec(1), col_spec("lru_g"),
            dir_spec(0), dir_spec(1), col_spec("gla_g"),
            gate_spec(0), gate_spec(1), gate_spec(2),
            _resident((1, d), lambda b, t: (0, 0)),
            _resident((1, d), lambda b, t: (0, 0)),
            _resident((1, dv), lambda b, t: (0, 0)),
            _stacked(w_branch, (layer,)),
            _stacked(w_out, (layer,)),
            _resident((1, d), lambda b, t: (0, 0)),
            _resident((1, d), lambda b, t: (0, 0)),
        ],
        out_specs=pl.BlockSpec((nb, TOKEN_BLOCK, d), lambda b, t: (b, t, 0)),
        compiler_params=_cparams(2),
        name="mixer_merge",
    )(h, mods, mods, y_ssd, y_ssd, p, p, h_lru, h_lru, p, o_gla, o_gla, p, p, p, p,
      d_skip_w, ssd_norm_g.reshape(1, d), gla_norm_g.reshape(1, dv), w_branch, w_out,
      ln_g.reshape(1, d), ln_b.reshape(1, d))


def _permute_w_kernel(w_ref, o_ref, *, moves, tail):
    in_total = w_ref.shape[1]

    def piece(src, width):
        a0 = src // LANES * LANES
        a1 = min(-(-(src + width) // LANES) * LANES, in_total)
        return w_ref[:, a0:a1][:, src - a0:src - a0 + width]

    for src, dst, width in moves:
        o_ref[:, dst:dst + width] = piece(src, width).astype(o_ref.dtype)
    parts = [piece(src, width) for src, width in tail]
    used = sum(width for _, width in tail)
    parts.append(jnp.zeros((w_ref.shape[0], LANES - used), w_ref.dtype))
    o_ref[:, o_ref.shape[1] - LANES:] = jnp.concatenate(parts, axis=1).astype(o_ref.dtype)


def _permute_w_in(w_in, d, ssd_heads, total):
    ssd_w = d
    bc_w = SSD_GROUPS * SSD_STATE
    gla_key = d // 2
    sizes = (ssd_w, ssd_w + 2 * bc_w, 2 * ssd_heads, d, d, gla_key, gla_key, d, d, 2 * GLA_GATE_RANK, N_BRANCH * d)
    src = [0]
    for s in sizes:
        src.append(src[-1] + s)
    z, xbc, dtr, lx, lg, gq, gk, gv, gg, alr, gates = range(len(sizes))
    order = (xbc, lx, z, lg, gq, gk, gv, gg, gates)
    moves, dst = [], 0
    for n in order:
        assert dst % LANES == 0
        moves.append((src[n], dst, sizes[n]))
        dst += sizes[n]
    assert dst == total - LANES
    tail = ((src[dtr], sizes[dtr]), (src[alr], sizes[alr]))
    n_layers, rows, in_total = w_in.shape
    row_block = 256 if rows % 256 == 0 else rows
    kern = functools.partial(_permute_w_kernel, moves=tuple(moves), tail=tail)
    return pl.pallas_call(
        kern,
        out_shape=jax.ShapeDtypeStruct((n_layers, rows, total), BF16),
        grid=(n_layers, rows // row_block),
        in_specs=[pl.BlockSpec((None, row_block, in_total), lambda l, r: (l, r, 0))],
        out_specs=pl.BlockSpec((None, row_block, total), lambda l, r: (l, r, 0)),
        compiler_params=_cparams(2),
        name="permute_w_in",
    )(w_in)


def _lru_gate_tiles(w_a, w_x):
    per = MXU_TILE // LRU_BLOCK
    n_l, n_dir, nb, k, _ = w_a.shape
    n_tiles = nb // per

    def tiles(w):
        w = w.astype(BF16).reshape(n_l, n_dir, n_tiles, per, k, k)
        on_diag = jnp.eye(per, dtype=bool)[:, None, :, None]
        t = jnp.where(on_diag, w[:, :, :, :, :, None, :], jnp.zeros((), BF16))
        return t.reshape(n_l, n_dir, n_tiles, MXU_TILE, MXU_TILE)

    return jnp.concatenate([tiles(w_a), tiles(w_x)], axis=-1)


def kernel(x, c, ctx, c_ctx, w_ada, b_ada, ln_g, ln_b, ffn_w_up, ffn_w_down, w_in, ssd_conv_w, ssd_conv_b,
           ssd_dt_bias, ssd_a_log, ssd_d, ssd_norm_g, lru_conv_w, lru_conv_b, lru_w_a, lru_b_a, lru_w_x, lru_b_x,
           lru_lam, gla_w_gate, gla_b_gate, gla_norm_g, w_branch, w_out):
    n_b, t_lat, d = x.shape
    n_ctx = ctx.shape[1]
    depth = w_ada.shape[0]
    rows = t_lat // GRID_W
    ssd_heads = ssd_dt_bias.shape[-1]
    assert n_ctx % TOKEN_BLOCK == 0 and t_lat % TOKEN_BLOCK == 0
    for line in (n_ctx, GRID_W, rows):
        assert line & (line - 1) == 0 and TOKEN_BLOCK % min(line, TOKEN_BLOCK) == 0
    assert n_ctx <= TOKEN_BLOCK
    alpha = (2.0 * depth) ** 0.25
    n_ctx_blocks = n_ctx // TOKEN_BLOCK
    n_ctx_chunks = n_ctx // SCAN_CHUNK

    layout, total = _proj_layout(d, ssd_heads)
    alr_off = 2 * ssd_heads

    n_rows = -(-(n_b + 1) // SUBLANES) * SUBLANES
    cond = jnp.concatenate([c, jnp.zeros((n_rows - n_b - 1, d), F32), c_ctx[None, :]], axis=0)
    mods_all = _ada_call(cond, w_ada, b_ada).reshape(depth, n_rows, N_MOD, d)

    expand = jnp.repeat(jnp.eye(ssd_heads, dtype=BF16), SSD_HEAD_DIM, axis=1)
    expand = jnp.concatenate([expand] * 3, axis=0)
    wup_all = ffn_w_up.astype(BF16)
    wdn_all = ffn_w_down.astype(BF16)
    w_perm_all = _permute_w_in(w_in, d, ssd_heads, total)
    lru_tiles_all = _lru_gate_tiles(lru_w_a, lru_w_x)
    w_branch_all = w_branch.astype(BF16)
    w_out_all = w_out.astype(BF16)

    h = jnp.concatenate([ctx, x], axis=1)
    for l in range(depth):
        last = l == depth - 1
        col_major = l % 2 == 1
        mods = mods_all[l]

        h = _ffn_call(h, mods, wup_all, wdn_all, (l, 0), ln_g[l, 0], ln_b[l, 0], j=0, n_ctx_blocks=n_ctx_blocks,
                      alpha=alpha, gather_in=col_major)

        conv_w = jnp.concatenate([ssd_conv_w[l], lru_conv_w[l]], axis=1)
        conv_b = jnp.concatenate([ssd_conv_b[l], lru_conv_b[l]], axis=0)[None, :]
        p, small = _proj_call(h, mods, w_perm_all, l, conv_w, conv_b, layout, n_ctx_blocks=n_ctx_blocks,
                              line_ctx=n_ctx, line_lat=rows if col_major else GRID_W)

        y_ssd, o_gla = _chunk_scans_call(p, small, layout, ssd_dt_bias[l], ssd_a_log[l], expand, gla_w_gate[l],
                                         gla_b_gate[l][:, None, :], alr_off=alr_off, n_ctx_chunks=n_ctx_chunks)
        h_lru = _lru_call(p, layout, lru_tiles_all, l, lru_b_a[l][:, None, :], lru_b_x[l][:, None, :],
                          lru_lam[l][:, None, :], n_ctx_blocks=n_ctx_blocks)

        d_skip_w = jnp.repeat(ssd_d[l, 0] + ssd_d[l, 1], SSD_HEAD_DIM)[None, :]
        skip_blocks = n_ctx_blocks if last else 0
        h = _merge_call(h, mods, p, layout, y_ssd, h_lru, o_gla, d_skip_w, ssd_norm_g[l], gla_norm_g[l],
                        w_branch_all, w_out_all, l, ln_g[l, 1], ln_b[l, 1],
                        n_ctx_blocks=n_ctx_blocks, skip_blocks=skip_blocks, alpha=alpha)
        n_ctx_now = 0 if last else n_ctx
        h = _ffn_call(h, mods, wup_all, wdn_all, (l, 1), ln_g[l, 2], ln_b[l, 2], j=2,
                      n_ctx_blocks=n_ctx_now // TOKEN_BLOCK, alpha=alpha, scatter_out=col_major)
    return h
```

```python
import functools
import math

import jax
import jax.numpy as jnp
from jax import lax
from jax.experimental import pallas as pl
from jax.experimental.pallas import tpu as pltpu

F32 = jnp.float32
BF16 = jnp.bfloat16

GRID_W = 64
CONV_W = 4
CONV_LEFT = 2
SSD_HEAD_DIM = 64
SSD_GROUPS = 4
SSD_STATE = 128
LRU_BLOCK = 64
LRU_C = 8.0
GLA_HEADS = 4
GLA_GATE_RANK = 16
GLA_TAU = 16.0
N_MOD = 9
N_BRANCH = 3
FFN_RES_W = 0.5
NORM_EPS = 1e-5

TOKEN_BLOCK = 256
SCAN_CHUNK = 128
SCAN_BATCH = 8
TOKEN_BATCH = 4
MERGE_BATCH = 2
PROJ_BATCH = 1
MXU_TILE = 256
LANES = 128
SUBLANES = 8
VMEM_LIMIT = 56 * 1024 * 1024


def _cparams(n_axes):
    return pltpu.CompilerParams(dimension_semantics=("arbitrary",) * n_axes,
                                vmem_limit_bytes=VMEM_LIMIT)


def _resident(block_shape, index_map):
    return pl.BlockSpec(block_shape, index_map, pipeline_mode=pl.Buffered(1))


def _stacked(arr, lead):
    n = len(lead)
    tail = tuple(arr.shape[n:])
    index = tuple(lead) + (0,) * len(tail)
    return pl.BlockSpec((None,) * n + tail, lambda *_: index, pipeline_mode=pl.Buffered(1))


def _sigmoid(x):
    return 0.5 * jnp.tanh(0.5 * x) + 0.5


def _silu(x):
    hx = 0.5 * x
    return hx * jnp.tanh(hx) + hx


def _softplus(x):
    return jnp.maximum(x, 0.0) + jnp.log1p(jnp.exp(-jnp.abs(x)))


def _log_sigmoid(x):
    return jnp.minimum(x, 0.0) - jnp.log(1.0 + jnp.exp(-jnp.abs(x)))


def _split_terms(x, n):
    terms = []
    for _ in range(n - 1):
        t = x.astype(BF16)
        terms.append(t)
        x = x - t.astype(F32)
    terms.append(x.astype(BF16))
    return terms


_dot = functools.partial(jnp.dot, preferred_element_type=F32)


def _exact_lhs_dot(t_bf16, x, n_terms):
    return _dot(jnp.concatenate([t_bf16] * n_terms, axis=1), jnp.concatenate(_split_terms(x, n_terms), axis=0))


def _exact_rhs_dot(x, e3_bf16):
    return _dot(jnp.concatenate(_split_terms(x, 3), axis=1), e3_bf16)


def _f32_dot_small_k(a, w):
    at = _split_terms(a, 3)
    wt = _split_terms(w, 3)
    pairs = [(0, 0), (0, 1), (0, 2), (1, 0), (1, 1), (2, 0)]
    lhs = jnp.concatenate([at[i] for i, _ in pairs], axis=1)
    rhs = jnp.concatenate([wt[j] for _, j in pairs], axis=0)
    return _dot(lhs, rhs)


def _layer_norm(z, g, b):
    mu = jnp.mean(z, axis=-1, keepdims=True)
    zc = z - mu
    var = jnp.mean(zc * zc, axis=-1, keepdims=True)
    return zc * lax.rsqrt(var + NORM_EPS) * g + b


def _scan_order(i, rev, n_ctx, n_tot):
    back = jnp.where(i < n_ctx, n_ctx - 1 - i, n_ctx + n_tot - 1 - i)
    return jnp.where(rev == 0, i, back)


def _ada_kernel(s_ref, w_ref, b_ref, o_ref):
    s = _silu(s_ref[...])
    o_ref[...] = jnp.dot(s, w_ref[...], preferred_element_type=F32,
                         precision=lax.Precision.HIGHEST) + b_ref[...]


def _ada_call(cond, w_ada, b_ada):
    n_layers, d, nd = w_ada.shape
    rows = cond.shape[0]
    n_col = nd // d
    return pl.pallas_call(
        _ada_kernel,
        out_shape=jax.ShapeDtypeStruct((n_layers, rows, nd), F32),
        grid=(n_layers, n_col),
        in_specs=[
            pl.BlockSpec((rows, d), lambda l, j: (0, 0)),
            pl.BlockSpec((None, d, d), lambda l, j: (l, 0, j)),
            pl.BlockSpec((None, 1, d), lambda l, j: (l, 0, j)),
        ],
        out_specs=pl.BlockSpec((None, rows, d), lambda l, j: (l, 0, j)),
        compiler_params=_cparams(2),
        name="ada_mod",
    )(cond, w_ada, b_ada.reshape(n_layers, 1, nd))


def _grid_block_copies(grid_ref, buf_ref, sem, elem, bi, t, n_ctx_blocks, to_grid):
    n_lines = grid_ref.shape[1]
    lines_per_block = TOKEN_BLOCK // GRID_W
    ctx_lines = n_ctx_blocks * lines_per_block
    rows = n_lines - ctx_lines
    cols_per_block = TOKEN_BLOCK // rows

    def copy(grid_view, buf_view):
        src, dst = (buf_view, grid_view) if to_grid else (grid_view, buf_view)
        return pltpu.make_async_copy(src, dst, sem.at[bi])

    def ctx():
        return [copy(grid_ref.at[elem, t * lines_per_block + r], buf_ref.at[bi, pl.ds(r * GRID_W, GRID_W)])
                for r in range(lines_per_block)]

    def lat():
        col0 = (t - n_ctx_blocks) * cols_per_block
        return [copy(grid_ref.at[elem, pl.ds(ctx_lines, rows), col0 + c], buf_ref.at[bi, pl.ds(c * rows, rows)])
                for c in range(cols_per_block)]

    return ctx, lat


def _ffn_kernel(x_ref, m_ref, mctx_ref, wup_ref, wdn_ref, g_ref, b_ref, o_ref, *scratch, j, alpha, d_ff, ff_chunk,
                n_ctx_blocks, gather_in, scatter_out):
    b = pl.program_id(0)
    t = pl.program_id(1)
    is_ctx = t < n_ctx_blocks
    nb = m_ref.shape[0]
    buf, sem = scratch if (gather_in or scatter_out) else (None, None)

    def copies(bi):
        grid_ref = x_ref if gather_in else o_ref
        return _grid_block_copies(grid_ref, buf, sem, b * nb + bi, bi, t, n_ctx_blocks, to_grid=scatter_out)

    def for_block_kind(bi, action):
        ctx, lat = copies(bi)
        if n_ctx_blocks:
            @pl.when(is_ctx)
            def _():
                for cp in ctx():
                    action(cp)

        @pl.when(jnp.logical_not(is_ctx))
        def _():
            for cp in lat():
                action(cp)

    if gather_in:
        for bi in range(nb):
            for_block_kind(bi, lambda cp: cp.start())
    for bi in range(nb):
        if gather_in:
            for_block_kind(bi, lambda cp: cp.wait())
            x = buf[bi]
        else:
            x = x_ref[bi]
        m = jnp.where(is_ctx, mctx_ref[3 * j:3 * j + 3, :], m_ref[bi, 3 * j:3 * j + 3, :])
        shift, scale, gate = m[0:1], m[1:2], m[2:3]
        u = (x * (1.0 + scale) + shift).astype(BF16)
        acc = None
        for c0 in range(0, d_ff, ff_chunk):
            a = _dot(u, wup_ref[:, c0:c0 + ff_chunk])
            v = _dot(u, wup_ref[:, d_ff + c0:d_ff + c0 + ff_chunk])
            hid = (_silu(a) * v).astype(BF16)
            part = _dot(hid, wdn_ref[c0:c0 + ff_chunk, :])
            acc = part if acc is None else acc + part
        z = alpha * x + (FFN_RES_W * gate) * acc
        out = _layer_norm(z, g_ref[...], b_ref[...])
        if scatter_out:
            buf[bi] = out
            for_block_kind(bi, lambda cp: cp.start())
        else:
            o_ref[bi] = out
    if scatter_out:
        for bi in range(nb):
            for_block_kind(bi, lambda cp: cp.wait())


def _ffn_call(h, mods, w_up, w_dn, lead, ln_g, ln_b, *, j, n_ctx_blocks, alpha, gather_in=False, scatter_out=False):
    assert not (gather_in and scatter_out)
    n_b, tt, d = h.shape
    d_ff = w_dn.shape[-2]
    ff_chunk = d_ff // 2 if (d_ff // 2) % LANES == 0 else d_ff
    ctx_row = mods.shape[0] - 1
    n_blocks = tt // TOKEN_BLOCK
    nb = TOKEN_BATCH if n_b % TOKEN_BATCH == 0 else 1
    grid_view = (n_b, tt // GRID_W, GRID_W, d)
    block_spec = pl.BlockSpec((nb, TOKEN_BLOCK, d), lambda b, t: (b, t, 0))
    any_spec = pl.BlockSpec(memory_space=pl.ANY)
    scratch = []
    if gather_in or scatter_out:
        scratch = [pltpu.VMEM((nb, TOKEN_BLOCK, d), F32), pltpu.SemaphoreType.DMA((nb,))]

    kern = functools.partial(_ffn_kernel, j=j, alpha=alpha, d_ff=d_ff, ff_chunk=ff_chunk, n_ctx_blocks=n_ctx_blocks,
                             gather_in=gather_in, scatter_out=scatter_out)
    out = pl.pallas_call(
        kern,
        out_shape=jax.ShapeDtypeStruct(grid_view if scatter_out else h.shape, F32),
        grid=(n_b // nb, n_blocks),
        in_specs=[
            any_spec if gather_in else block_spec,
            pl.BlockSpec((nb, N_MOD, d), lambda b, t: (b, 0, 0)),
            pl.BlockSpec((None, N_MOD, d), lambda b, t: (ctx_row, 0, 0)),
            _stacked(w_up, lead),
            _stacked(w_dn, lead),
            _resident((1, d), lambda b, t: (0, 0)),
            _resident((1, d), lambda b, t: (0, 0)),
        ],
        out_specs=any_spec if scatter_out else block_spec,
        scratch_shapes=scratch,
        compiler_params=_cparams(2),
        name="ffn_sublayer",
    )(h.reshape(grid_view) if gather_in else h, mods, mods, w_up, w_dn, ln_g.reshape(1, d), ln_b.reshape(1, d))
    return out.reshape(h.shape) if scatter_out else out


def _proj_layout(d, ssd_heads):
    ssd_w = d
    bc_w = SSD_GROUPS * SSD_STATE
    gla_key = d // 2
    names = [("ssd_x", ssd_w), ("ssd_b", bc_w), ("ssd_c", bc_w), ("lru_x", d), ("ssd_z", ssd_w),
             ("lru_g", d), ("gla_q", gla_key), ("gla_k", gla_key), ("gla_v", d), ("gla_g", d),
             ("gates", N_BRANCH * d), ("small", LANES)]
    off, out = 0, {}
    for name, width in names:
        out[name] = (off, width)
        off += width
    return out, off


_ACTIVATIONS = {"none": lambda v: v, "silu": _silu, "sigmoid": _sigmoid, "gelu": jax.nn.gelu}


def _proj_kernel(x_ref, m_ref, mctx_ref, w_ref, cw_ref, cb_ref, o_ref, sm_ref, ybuf, *, segments, main_cols,
                 n_ctx_blocks, line_ctx, line_lat):
    is_ctx = pl.program_id(1) < n_ctx_blocks
    line = jnp.where(is_ctx, line_ctx, line_lat)
    n_strips = ybuf.shape[0] // x_ref.shape[0]
    for bi in range(x_ref.shape[0]):
        m = jnp.where(is_ctx, mctx_ref[3:5, :], m_ref[bi, 3:5, :])
        _proj_block(x_ref.at[bi], m[0:1], m[1:2], w_ref, cw_ref, cb_ref, o_ref.at[bi], sm_ref.at[bi],
                    ybuf.at[bi * n_strips:(bi + 1) * n_strips], line, segments, main_cols)


def _proj_block(x_ref, shift, scale, w_ref, cw_ref, cb_ref, o_ref, sm_ref, ybuf, line, segments, main_cols):
    x = x_ref[...]
    u = (x * (1.0 + scale) + shift).astype(BF16)
    rows = x.shape[0]
    halo = SUBLANES
    sm_ref[...] = _dot(u, w_ref[:, main_cols:main_cols + LANES])
    pos = lax.broadcasted_iota(jnp.int32, (rows, LANES), 0) & (line - 1)
    taps = [(k, k - CONV_LEFT) for k in range(CONV_W) if k != CONV_LEFT]
    valid = {off: (pos + off >= 0) & (pos + off < line) for _, off in taps}
    zeros = jnp.zeros((halo, LANES), F32)
    for c0, cw, conv, act in segments:
        y = _dot(u, w_ref[:, c0:c0 + cw])
        if not conv:
            o_ref[:, c0:c0 + cw] = _ACTIVATIONS[act](y).astype(o_ref.dtype)
            continue
        for j in range(cw // LANES):
            ybuf[j, 0:halo, :] = zeros
            ybuf[j, halo:halo + rows, :] = y[:, j * LANES:(j + 1) * LANES]
            ybuf[j, halo + rows:2 * halo + rows, :] = zeros
        for j in range(cw // LANES):
            cols = slice(c0 + j * LANES, c0 + (j + 1) * LANES)
            acc = cb_ref[:, cols] + ybuf[j, halo:halo + rows, :] * cw_ref[CONV_LEFT:CONV_LEFT + 1, cols]
            for k, off in taps:
                shifted = ybuf[j, halo + off:halo + off + rows, :]
                acc = acc + jnp.where(valid[off], shifted, 0.0) * cw_ref[k:k + 1, cols]
            o_ref[:, cols] = _ACTIVATIONS[act](acc).astype(o_ref.dtype)


def _proj_segments(layout, col_chunk):
    kinds = {"ssd_x": (True, "silu"), "ssd_b": (True, "silu"), "ssd_c": (True, "silu"), "lru_x": (True, "none"),
             "lru_g": (False, "gelu"), "gates": (False, "sigmoid"), "ssd_z": (False, "silu"),
             "gla_g": (False, "silu"), "gla_v": (False, "none"), "gla_q": (False, "none"), "gla_k": (False, "none")}
    conv_segs, plain_segs = [], []
    for name, (conv, act) in kinds.items():
        off, width = layout[name]
        for c0 in range(off, off + width, col_chunk):
            (conv_segs if conv else plain_segs).append((c0, min(col_chunk, off + width - c0), conv, act))
    segs = []
    while conv_segs or plain_segs:
        if conv_segs:
            segs.append(conv_segs.pop(0))
        if plain_segs:
            segs.append(plain_segs.pop(0))
    return tuple(segs)


def _proj_call(h, mods, w_perm, layer, conv_w, conv_b, layout, *, n_ctx_blocks, line_ctx, line_lat):
    n_b, tt, d = h.shape
    total = w_perm.shape[-1]
    conv_cols = conv_w.shape[1]
    ctx_row = mods.shape[0] - 1
    n_blocks = tt // TOKEN_BLOCK
    col_chunk = 1024
    nb = PROJ_BATCH if n_b % PROJ_BATCH == 0 else 1

    main = total - LANES
    kern = functools.partial(_proj_kernel, segments=_proj_segments(layout, col_chunk), main_cols=main,
                             n_ctx_blocks=n_ctx_blocks, line_ctx=line_ctx, line_lat=line_lat)
    return pl.pallas_call(
        kern,
        out_shape=(jax.ShapeDtypeStruct((n_b, tt, main), BF16), jax.ShapeDtypeStruct((n_b, tt, LANES), F32)),
        grid=(n_b // nb, n_blocks),
        in_specs=[
            pl.BlockSpec((nb, TOKEN_BLOCK, d), lambda b, t: (b, t, 0)),
            pl.BlockSpec((nb, N_MOD, d), lambda b, t: (b, 0, 0)),
            pl.BlockSpec((None, N_MOD, d), lambda b, t: (ctx_row, 0, 0)),
            _stacked(w_perm, (layer,)),
            _resident((CONV_W, conv_cols), lambda b, t: (0, 0)),
            _resident((1, conv_cols), lambda b, t: (0, 0)),
        ],
        out_specs=(pl.BlockSpec((nb, TOKEN_BLOCK, main), lambda b, t: (b, t, 0)),
                   pl.BlockSpec((nb, TOKEN_BLOCK, LANES), lambda b, t: (b, t, 0))),
        scratch_shapes=[pltpu.VMEM((nb * (col_chunk // LANES), TOKEN_BLOCK + 2 * SUBLANES, LANES), F32)],
        compiler_params=_cparams(2),
        name="mixer_in_proj",
    )(h, mods, mods, w_perm, conv_w, conv_b)


def _ssd_chunk(fwd, x_ref, b_ref, c_ref, sm_ref, dtb_ref, alog_ref, exp_ref, y_ref, state_ref, n_heads):
    q = x_ref.shape[0]
    hg = n_heads // SSD_GROUPS
    gw = hg * SSD_HEAD_DIM
    sm = sm_ref[...]
    dt_raw = jnp.where(fwd, sm[:, 0:n_heads], sm[:, n_heads:2 * n_heads])
    dt_bias = jnp.where(fwd, dtb_ref[0:1, :], dtb_ref[1:2, :])
    a_log = jnp.where(fwd, alog_ref[0:1, :], alog_ref[1:2, :])
    dt = _softplus(dt_raw + dt_bias)
    a = dt * (-jnp.exp(a_log))
    row = lax.broadcasted_iota(jnp.int32, (q, q), 0)
    col = lax.broadcasted_iota(jnp.int32, (q, q), 1)
    mask = (col - row) * jnp.where(fwd, 1, -1) <= 0
    tri = jnp.where(mask, 1.0, 0.0).astype(BF16)
    cs = _exact_lhs_dot(tri, a, 3)
    tot = jnp.sum(a, axis=0, keepdims=True)
    cs2 = cs * math.log2(math.e)
    src2 = cs2 - jnp.log2(dt)
    pad = jnp.zeros((q, LANES - n_heads), F32)
    src2_t = jnp.concatenate([src2, pad], axis=1).T[0:n_heads]

    w_state = dt * jnp.exp(tot - cs)
    from_start = jnp.exp(cs)
    e_tot = jnp.broadcast_to(jnp.exp(tot), (SUBLANES, n_heads))
    wide = _exact_rhs_dot(jnp.concatenate([w_state, from_start, e_tot], axis=0), exp_ref[...])
    w_state_w = wide[0:q]
    from_start_w = wide[q:2 * q]
    e_tot_w = wide[2 * q:2 * q + 1]

    x_bf = x_ref[...]
    xw_bf = (x_bf.astype(F32) * w_state_w).astype(BF16)
    pair_w = 2 * SSD_HEAD_DIM
    lane = lax.broadcasted_iota(jnp.int32, (1, pair_w), 1)
    keep_lo = jnp.where(lane < SSD_HEAD_DIM, 1.0, 0.0).astype(BF16)
    keep_hi = jnp.where(lane < SSD_HEAD_DIM, 0.0, 1.0).astype(BF16)
    for g in range(SSD_GROUPS):
        bt = b_ref[:, g * SSD_STATE:(g + 1) * SSD_STATE].T
        cg = c_ref[:, g * SSD_STATE:(g + 1) * SSD_STATE]
        scores = _dot(cg, bt)
        s_prev = state_ref[g]
        y_off = _dot(cg, s_prev.astype(BF16)) * from_start_w[:, g * gw:(g + 1) * gw]
        for pp in range(hg // 2):
            ms = []
            for head in (g * hg + 2 * pp, g * hg + 2 * pp + 1):
                seg2 = jnp.where(mask, cs2[:, head:head + 1] - src2_t[head:head + 1, :], -1e30)
                ms.append((scores * jnp.exp2(seg2)).astype(BF16))
            cols = slice(g * gw + pp * pair_w, g * gw + (pp + 1) * pair_w)
            xp = x_bf[:, cols]
            y_pair = _dot(jnp.concatenate(ms, axis=1), jnp.concatenate([xp * keep_lo, xp * keep_hi], axis=0))
            y_ref[:, cols] = (y_off[:, pp * pair_w:(pp + 1) * pair_w] + y_pair).astype(y_ref.dtype)
        upd = _dot(bt, xw_bf[:, g * gw:(g + 1) * gw])
        state_ref[g] = s_prev * e_tot_w[:, g * gw:(g + 1) * gw] + upd


def _scan_col_spec(layout, name, nb, rows, n_ctx, n_tot):
    off, w = layout[name]
    assert off % w == 0
    blk = off // w
    return pl.BlockSpec((nb, rows, w), lambda b, r, i: (b, _scan_order(i, r, n_ctx, n_tot), blk))


def _scan_small_spec(nb, rows, n_ctx, n_tot):
    return pl.BlockSpec((nb, rows, LANES), lambda b, r, i: (b, _scan_order(i, r, n_ctx, n_tot), 0))


def _scan_out_spec(nb, rows, width, n_ctx, n_tot):
    return pl.BlockSpec((None, nb, rows, width), lambda b, r, i: (r, b, _scan_order(i, r, n_ctx, n_tot), 0))


def _lru_scan_block(a_ref, b_ref, o_ref, carry_ref, reverse):
    rows, width = a_ref.shape
    sub = lax.broadcasted_iota(jnp.int32, (SUBLANES, width), 0)
    carry = carry_ref[...]
    n_groups = rows // SUBLANES
    order = range(n_groups - 1, -1, -1) if reverse else range(n_groups)
    for gi in order:
        sl = slice(gi * SUBLANES, (gi + 1) * SUBLANES)
        a = a_ref[sl, :]
        b = b_ref[sl, :]
        k = 1
        while k < SUBLANES:
            if reverse:
                a_sh = pltpu.roll(a, SUBLANES - k, axis=0)
                b_sh = pltpu.roll(b, SUBLANES - k, axis=0)
                ok = sub < SUBLANES - k
            else:
                a_sh = pltpu.roll(a, k, axis=0)
                b_sh = pltpu.roll(b, k, axis=0)
                ok = sub >= k
            b = b + a * jnp.where(ok, b_sh, 0.0)
            a = a * jnp.where(ok, a_sh, 1.0)
            k *= 2
        hblk = b + a * carry
        b_ref[sl, :] = hblk
        carry = hblk[0:1, :] if reverse else hblk[SUBLANES - 1:SUBLANES, :]
    carry_ref[...] = carry
    o_ref[...] = b_ref[...].astype(o_ref.dtype)


def _lru_kernel(x_ref, w_ref, ba_ref, bx_ref, lam_ref, h_ref, a_s, b_s, carry_ref):
    rev = pl.program_id(1)
    i = pl.program_id(2)
    nb = x_ref.shape[0]

    @pl.when(i == 0)
    def _():
        carry_ref[...] = jnp.zeros_like(carry_ref)

    n_tiles = w_ref.shape[0]
    neg_c_sp = (-LRU_C) * _softplus(-lam_ref[...])
    for bi in range(nb):
        x_bf = x_ref[bi]
        pre_a, pre_x = [], []
        for t in range(n_tiles):
            y = _dot(x_bf[:, t * MXU_TILE:(t + 1) * MXU_TILE], w_ref[t])
            pre_a.append(y[:, :MXU_TILE])
            pre_x.append(y[:, MXU_TILE:])
        r = _sigmoid(jnp.concatenate(pre_a, axis=1) + ba_ref[...])
        ig = _sigmoid(jnp.concatenate(pre_x, axis=1) + bx_ref[...])
        log_a = r * neg_c_sp
        a = jnp.exp(log_a)
        a_s[bi] = a
        b_s[bi] = jnp.sqrt(-jnp.tanh(log_a) * (a * a + 1.0)) * (ig * x_bf.astype(F32))

    @pl.when(rev == 0)
    def _():
        for bi in range(nb):
            _lru_scan_block(a_s.at[bi], b_s.at[bi], h_ref.at[bi], carry_ref.at[bi], reverse=False)

    @pl.when(rev == 1)
    def _():
        for bi in range(nb):
            _lru_scan_block(a_s.at[bi], b_s.at[bi], h_ref.at[bi], carry_ref.at[bi], reverse=True)


def _lru_call(p, layout, w_tiles, layer, b_a, b_x, lam, *, n_ctx_blocks):
    n_b, tt, _ = p.shape
    width = layout["lru_x"][1]
    n_blocks = tt // TOKEN_BLOCK
    n_tiles = w_tiles.shape[2]
    nb = SCAN_BATCH if n_b % SCAN_BATCH == 0 else 1

    return pl.pallas_call(
        _lru_kernel,
        out_shape=jax.ShapeDtypeStruct((2, n_b, tt, width), BF16),
        grid=(n_b // nb, 2, n_blocks),
        in_specs=[
            _scan_col_spec(layout, "lru_x", nb, TOKEN_BLOCK, n_ctx_blocks, n_blocks),
            pl.BlockSpec((None, None, n_tiles, MXU_TILE, 2 * MXU_TILE), lambda b, r, i: (layer, r, 0, 0, 0)),
            pl.BlockSpec((None, 1, width), lambda b, r, i: (r, 0, 0)),
            pl.BlockSpec((None, 1, width), lambda b, r, i: (r, 0, 0)),
            pl.BlockSpec((None, 1, width), lambda b, r, i: (r, 0, 0)),
        ],
        out_specs=_scan_out_spec(nb, TOKEN_BLOCK, width, n_ctx_blocks, n_blocks),
        scratch_shapes=[pltpu.VMEM((nb, TOKEN_BLOCK, width), F32), pltpu.VMEM((nb, TOKEN_BLOCK, width), F32),
                        pltpu.VMEM((nb, 1, width), F32)],
        compiler_params=_cparams(3),
        name="rglru_scan",
    )(p, w_tiles, b_a, b_x, lam)


def _gla_chunk(fwd, q_ref, k_ref, v_ref, sm_ref, wg_ref, bg_ref, o_ref, state_ref, alr_off, dk, dv):
    q_len = q_ref.shape[0]
    n_heads = q_ref.shape[1] // dk
    sm = sm_ref[...]
    r = GLA_GATE_RANK
    a_lr = jnp.where(fwd, sm[:, alr_off:alr_off + r], sm[:, alr_off + r:alr_off + 2 * r])
    z = _f32_dot_small_k(a_lr, wg_ref[...]) + bg_ref[...]
    log_a = _log_sigmoid(z) * (1.0 / GLA_TAU)
    row = lax.broadcasted_iota(jnp.int32, (q_len, q_len), 0)
    col = lax.broadcasted_iota(jnp.int32, (q_len, q_len), 1)
    mask = (col - row) * jnp.where(fwd, 1, -1) <= 0
    tri = jnp.where(mask, 1.0, 0.0).astype(BF16)
    bcum = _exact_lhs_dot(tri, log_a, 2)
    btot = jnp.sum(log_a, axis=0, keepdims=True)
    bmid = bcum[q_len // 2:q_len // 2 + 1, :]
    q_in = q_ref[...].astype(F32) * (dk ** -0.5) * jnp.exp(bcum - bmid)
    k_in = k_ref[...].astype(F32) * jnp.exp(bmid - bcum)
    q_off = (q_in * jnp.exp(bmid)).astype(BF16)
    k_st = (k_in * jnp.exp(btot - bmid)).astype(BF16)
    q_in = q_in.astype(BF16)
    k_in = k_in.astype(BF16)
    e_tot = jnp.exp(btot)
    for hh in range(n_heads):
        ks = slice(hh * dk, (hh + 1) * dk)
        vs = slice(hh * dv, (hh + 1) * dv)
        v_bf = v_ref[:, vs]
        att = lax.dot_general(q_in[:, ks], k_in[:, ks], (((1,), (1,)), ((), ())), preferred_element_type=F32)
        att = jnp.where(mask, att, 0.0).astype(BF16)
        s_prev = state_ref[hh]
        inter = lax.dot_general(q_off[:, ks], s_prev.astype(BF16), (((1,), (1,)), ((), ())),
                                preferred_element_type=F32)
        o_ref[:, vs] = (_dot(att, v_bf) + inter).astype(o_ref.dtype)
        upd = lax.dot_general(v_bf, k_st[:, ks], (((0,), (0,)), ((), ())), preferred_element_type=F32)
        state_ref[hh] = s_prev * e_tot[:, ks] + upd


def _chunk_scans_kernel(x_ref, b_ref, c_ref, q_ref, k_ref, v_ref, sm_ref, dtb_ref, alog_ref, exp_ref, wg_ref, bg_ref,
                        y_ref, o_ref, ssd_state, gla_state, *, n_heads, alr_off, dk, dv):
    rev = pl.program_id(1)
    i = pl.program_id(2)

    @pl.when(i == 0)
    def _():
        ssd_state[...] = jnp.zeros_like(ssd_state)
        gla_state[...] = jnp.zeros_like(gla_state)

    for bi in range(x_ref.shape[0]):
        _ssd_chunk(rev == 0, x_ref.at[bi], b_ref.at[bi], c_ref.at[bi], sm_ref.at[bi], dtb_ref, alog_ref, exp_ref,
                   y_ref.at[bi], ssd_state.at[bi], n_heads)
        _gla_chunk(rev == 0, q_ref.at[bi], k_ref.at[bi], v_ref.at[bi], sm_ref.at[bi], wg_ref, bg_ref,
                   o_ref.at[bi], gla_state.at[bi], alr_off, dk, dv)


def _chunk_scans_call(p, small, layout, dt_bias, a_log, expand, w_gate, b_gate, *, alr_off, n_ctx_chunks):
    n_b, tt, _ = p.shape
    n_heads = dt_bias.shape[1]
    ssd_w = n_heads * SSD_HEAD_DIM
    hg = n_heads // SSD_GROUPS
    key_w = layout["gla_q"][1]
    val_w = layout["gla_v"][1]
    dk = key_w // GLA_HEADS
    dv = val_w // GLA_HEADS
    n_chunks = tt // SCAN_CHUNK
    nb = SCAN_BATCH if n_b % SCAN_BATCH == 0 else 1
    col_spec = functools.partial(_scan_col_spec, layout, nb=nb, rows=SCAN_CHUNK, n_ctx=n_ctx_chunks, n_tot=n_chunks)

    kern = functools.partial(_chunk_scans_kernel, n_heads=n_heads, alr_off=alr_off, dk=dk, dv=dv)
    return pl.pallas_call(
        kern,
        out_shape=(jax.ShapeDtypeStruct((2, n_b, tt, ssd_w), BF16), jax.ShapeDtypeStruct((2, n_b, tt, val_w), BF16)),
        grid=(n_b // nb, 2, n_chunks),
        in_specs=[
            col_spec("ssd_x"), col_spec("ssd_b"), col_spec("ssd_c"),
            col_spec("gla_q"), col_spec("gla_k"), col_spec("gla_v"),
            _scan_small_spec(nb, SCAN_CHUNK, n_ctx_chunks, n_chunks),
            _resident((2, n_heads), lambda b, r, i: (0, 0)),
            _resident((2, n_heads), lambda b, r, i: (0, 0)),
            _resident(expand.shape, lambda b, r, i: (0, 0)),
            pl.BlockSpec((None, GLA_GATE_RANK, key_w), lambda b, r, i: (r, 0, 0)),
            pl.BlockSpec((None, 1, key_w), lambda b, r, i: (r, 0, 0)),
        ],
        out_specs=(_scan_out_spec(nb, SCAN_CHUNK, ssd_w, n_ctx_chunks, n_chunks),
                   _scan_out_spec(nb, SCAN_CHUNK, val_w, n_ctx_chunks, n_chunks)),
        scratch_shapes=[pltpu.VMEM((nb, SSD_GROUPS, SSD_STATE, hg * SSD_HEAD_DIM), F32),
                        pltpu.VMEM((nb, GLA_HEADS, dv, dk), F32)],
        compiler_params=_cparams(3),
        name="ssd_gla_scan",
    )(p, p, p, p, p, p, small, dt_bias, a_log, expand, w_gate, b_gate)


def _merge_kernel(h_ref, m_ref, mctx_ref, ys0_ref, ys1_ref, xs_ref, z_ref, hl0_ref, hl1_ref, gl_ref,
                  og0_ref, og1_ref, gg_ref, gt0_ref, gt1_ref, gt2_ref, dskip_ref, sng_ref, gng_ref,
                  wbr_ref, wout_ref, lng_ref, lnb_ref, o_ref, *, alpha, dv, n_ctx_blocks, skip_blocks):
    is_ctx = pl.program_id(1) + skip_blocks < n_ctx_blocks
    for bi in range(h_ref.shape[0]):
        gate = jnp.where(is_ctx, mctx_ref[5:6, :], m_ref[bi, 5:6, :])
        _merge_block(h_ref.at[bi], gate, ys0_ref.at[bi], ys1_ref.at[bi], xs_ref.at[bi], z_ref.at[bi],
                     hl0_ref.at[bi], hl1_ref.at[bi], gl_ref.at[bi], og0_ref.at[bi], og1_ref.at[bi], gg_ref.at[bi],
                     gt0_ref.at[bi], gt1_ref.at[bi], gt2_ref.at[bi], dskip_ref, sng_ref, gng_ref, wbr_ref, wout_ref,
                     lng_ref, lnb_ref, o_ref.at[bi], alpha, dv)


def _merge_block(h_ref, gate, ys0_ref, ys1_ref, xs_ref, z_ref, hl0_ref, hl1_ref, gl_ref, og0_ref, og1_ref, gg_ref,
                 gt0_ref, gt1_ref, gt2_ref, dskip_ref, sng_ref, gng_ref, wbr_ref, wout_ref, lng_ref, lnb_ref, o_ref,
                 alpha, dv):
    d = h_ref.shape[1]

    def f32(ref, cols=slice(None)):
        return ref[:, cols].astype(F32)

    y = (f32(ys0_ref) + f32(ys1_ref) + dskip_ref[...] * f32(xs_ref)) * f32(z_ref)
    y = y * lax.rsqrt(jnp.mean(y * y, axis=-1, keepdims=True) + NORM_EPS) * sng_ref[...]
    m = f32(gt0_ref) * _dot(y.astype(BF16), wbr_ref[0])
    y = (f32(hl0_ref) + f32(hl1_ref)) * f32(gl_ref)
    m = m + f32(gt1_ref) * _dot(y.astype(BF16), wbr_ref[1])
    parts = []
    for hh in range(d // dv):
        vs = slice(hh * dv, (hh + 1) * dv)
        o = f32(og0_ref, vs) + f32(og1_ref, vs)
        o = o * lax.rsqrt(jnp.mean(o * o, axis=-1, keepdims=True) + NORM_EPS) * gng_ref[...]
        parts.append(o * f32(gg_ref, vs))
    y = jnp.concatenate(parts, axis=1)
    m = m + f32(gt2_ref) * _dot(y.astype(BF16), wbr_ref[2])
    out = _dot(m.astype(BF16), wout_ref[...])
    z = alpha * h_ref[...] + gate * out
    o_ref[...] = _layer_norm(z, lng_ref[...], lnb_ref[...])


def _merge_call(h, mods, p, layout, y_ssd, h_lru, o_gla, d_skip_w, ssd_norm_g, gla_norm_g, w_branch, w_out, layer,
                ln_g, ln_b, *, n_ctx_blocks, skip_blocks, alpha):
    n_b, tt, d = h.shape
    n_blocks = tt // TOKEN_BLOCK - skip_blocks
    ctx_row = mods.shape[0] - 1
    dv = gla_norm_g.shape[-1]
    nb = MERGE_BATCH if n_b % MERGE_BATCH == 0 else 1

    def tok(width, blk):
        return pl.BlockSpec((nb, TOKEN_BLOCK, width), lambda b, t: (b, t + skip_blocks, blk))

    def col_spec(name):
        off, w = layout[name]
        blk = off // w
        assert blk * w == off
        return tok(w, blk)

    def gate_spec(n):
        off = layout["gates"][0] + n * d
        assert off % d == 0
        return tok(d, off // d)

    def dir_spec(rv):
        return pl.BlockSpec((None, nb, TOKEN_BLOCK, d), lambda b, t: (rv, b, t + skip_blocks, 0))

    kern = functools.partial(_merge_kernel, alpha=alpha, dv=dv, n_ctx_blocks=n_ctx_blocks, skip_blocks=skip_blocks)
    return pl.pallas_call(
        kern,
        out_shape=jax.ShapeDtypeStruct((n_b, n_blocks * TOKEN_BLOCK, d), F32),
        grid=(n_b // nb, n_blocks),
        in_specs=[
            tok(d, 0),
            pl.BlockSpec((nb, N_MOD, d), lambda b, t: (b, 0, 0)),
            pl.BlockSpec((None, N_MOD, d), lambda b, t: (ctx_row, 0, 0)),
            dir_spec(0), dir_spec(1), col_spec("ssd_x"), col_spec("ssd_z"),
            dir_spec(0), dir_spec(1), col_spec("lru_g"),
            dir_spec(0), dir_spec(1), col_spec("gla_g"),
            gate_spec(0), gate_spec(1), gate_spec(2),
            _resident((1, d), lambda b, t: (0, 0)),
            _resident((1, d), lambda b, t: (0, 0)),
            _resident((1, dv), lambda b, t: (0, 0)),
            _stacked(w_branch, (layer,)),
            _stacked(w_out, (layer,)),
            _resident((1, d), lambda b, t: (0, 0)),
            _resident((1, d), lambda b, t: (0, 0)),
        ],
        out_specs=pl.BlockSpec((nb, TOKEN_BLOCK, d), lambda b, t: (b, t, 0)),
        compiler_params=_cparams(2),
        name="mixer_merge",
    )(h, mods, mods, y_ssd, y_ssd, p, p, h_lru, h_lru, p, o_gla, o_gla, p, p, p, p,
      d_skip_w, ssd_norm_g.reshape(1, d), gla_norm_g.reshape(1, dv), w_branch, w_out,
      ln_g.reshape(1, d), ln_b.reshape(1, d))


def _permute_w_kernel(w_ref, o_ref, *, moves, tail):
    in_total = w_ref.shape[1]

    def piece(src, width):
        a0 = src // LANES * LANES
        a1 = min(-(-(src + width) // LANES) * LANES, in_total)
        return w_ref[:, a0:a1][:, src - a0:src - a0 + width]

    for src, dst, width in moves:
        o_ref[:, dst:dst + width] = piece(src, width).astype(o_ref.dtype)
    parts = [piece(src, width) for src, width in tail]
    used = sum(width for _, width in tail)
    parts.append(jnp.zeros((w_ref.shape[0], LANES - used), w_ref.dtype))
    o_ref[:, o_ref.shape[1] - LANES:] = jnp.concatenate(parts, axis=1).astype(o_ref.dtype)


def _permute_w_in(w_in, d, ssd_heads, total):
    ssd_w = d
    bc_w = SSD_GROUPS * SSD_STATE
    gla_key = d // 2
    sizes = (ssd_w, ssd_w + 2 * bc_w, 2 * ssd_heads, d, d, gla_key, gla_key, d, d, 2 * GLA_GATE_RANK, N_BRANCH * d)
    src = [0]
    for s in sizes:
        src.append(src[-1] + s)
    z, xbc, dtr, lx, lg, gq, gk, gv, gg, alr, gates = range(len(sizes))
    order = (xbc, lx, z, lg, gq, gk, gv, gg, gates)
    moves, dst = [], 0
    for n in order:
        assert dst % LANES == 0
        moves.append((src[n], dst, sizes[n]))
        dst += sizes[n]
    assert dst == total - LANES
    tail = ((src[dtr], sizes[dtr]), (src[alr], sizes[alr]))
    n_layers, rows, in_total = w_in.shape
    row_block = 256 if rows % 256 == 0 else rows
    kern = functools.partial(_permute_w_kernel, moves=tuple(moves), tail=tail)
    return pl.pallas_call(
        kern,
        out_shape=jax.ShapeDtypeStruct((n_layers, rows, total), BF16),
        grid=(n_layers, rows // row_block),
        in_specs=[pl.BlockSpec((None, row_block, in_total), lambda l, r: (l, r, 0))],
        out_specs=pl.BlockSpec((None, row_block, total), lambda l, r: (l, r, 0)),
        compiler_params=_cparams(2),
        name="permute_w_in",
    )(w_in)


def _lru_gate_tiles(w_a, w_x):
    per = MXU_TILE // LRU_BLOCK
    n_l, n_dir, nb, k, _ = w_a.shape
    n_tiles = nb // per

    def tiles(w):
        w = w.astype(BF16).reshape(n_l, n_dir, n_tiles, per, k, k)
        on_diag = jnp.eye(per, dtype=bool)[:, None, :, None]
        t = jnp.where(on_diag, w[:, :, :, :, :, None, :], jnp.zeros((), BF16))
        return t.reshape(n_l, n_dir, n_tiles, MXU_TILE, MXU_TILE)

    return jnp.concatenate([tiles(w_a), tiles(w_x)], axis=-1)


def kernel(x, c, ctx, c_ctx, w_ada, b_ada, ln_g, ln_b, ffn_w_up, ffn_w_down, w_in, ssd_conv_w, ssd_conv_b,
           ssd_dt_bias, ssd_a_log, ssd_d, ssd_norm_g, lru_conv_w, lru_conv_b, lru_w_a, lru_b_a, lru_w_x, lru_b_x,
           lru_lam, gla_w_gate, gla_b_gate, gla_norm_g, w_branch, w_out):
    n_b, t_lat, d = x.shape
    n_ctx = ctx.shape[1]
    depth = w_ada.shape[0]
    rows = t_lat // GRID_W
    ssd_heads = ssd_dt_bias.shape[-1]
    assert n_ctx % TOKEN_BLOCK == 0 and t_lat % TOKEN_BLOCK == 0
    for line in (n_ctx, GRID_W, rows):
        assert line & (line - 1) == 0 and TOKEN_BLOCK % min(line, TOKEN_BLOCK) == 0
    assert n_ctx <= TOKEN_BLOCK
    alpha = (2.0 * depth) ** 0.25
    n_ctx_blocks = n_ctx // TOKEN_BLOCK
    n_ctx_chunks = n_ctx // SCAN_CHUNK

    layout, total = _proj_layout(d, ssd_heads)
    alr_off = 2 * ssd_heads

    n_rows = -(-(n_b + 1) // SUBLANES) * SUBLANES
    cond = jnp.concatenate([c, jnp.zeros((n_rows - n_b - 1, d), F32), c_ctx[None, :]], axis=0)
    mods_all = _ada_call(cond, w_ada, b_ada).reshape(depth, n_rows, N_MOD, d)

    expand = jnp.repeat(jnp.eye(ssd_heads, dtype=BF16), SSD_HEAD_DIM, axis=1)
    expand = jnp.concatenate([expand] * 3, axis=0)
    wup_all = ffn_w_up.astype(BF16)
    wdn_all = ffn_w_down.astype(BF16)
    w_perm_all = _permute_w_in(w_in, d, ssd_heads, total)
    lru_tiles_all = _lru_gate_tiles(lru_w_a, lru_w_x)
    w_branch_all = w_branch.astype(BF16)
    w_out_all = w_out.astype(BF16)

    h = jnp.concatenate([ctx, x], axis=1)
    for l in range(depth):
        last = l == depth - 1
        col_major = l % 2 == 1
        mods = mods_all[l]

        h = _ffn_call(h, mods, wup_all, wdn_all, (l, 0), ln_g[l, 0], ln_b[l, 0], j=0, n_ctx_blocks=n_ctx_blocks,
                      alpha=alpha, gather_in=col_major)

        conv_w = jnp.concatenate([ssd_conv_w[l], lru_conv_w[l]], axis=1)
        conv_b = jnp.concatenate([ssd_conv_b[l], lru_conv_b[l]], axis=0)[None, :]
        p, small = _proj_call(h, mods, w_perm_all, l, conv_w, conv_b, layout, n_ctx_blocks=n_ctx_blocks,
                              line_ctx=n_ctx, line_lat=rows if col_major else GRID_W)

        y_ssd, o_gla = _chunk_scans_call(p, small, layout, ssd_dt_bias[l], ssd_a_log[l], expand, gla_w_gate[l],
                                         gla_b_gate[l][:, None, :], alr_off=alr_off, n_ctx_chunks=n_ctx_chunks)
        h_lru = _lru_call(p, layout, lru_tiles_all, l, lru_b_a[l][:, None, :], lru_b_x[l][:, None, :],
                          lru_lam[l][:, None, :], n_ctx_blocks=n_ctx_blocks)

        d_skip_w = jnp.repeat(ssd_d[l, 0] + ssd_d[l, 1], SSD_HEAD_DIM)[None, :]
        skip_blocks = n_ctx_blocks if last else 0
        h = _merge_call(h, mods, p, layout, y_ssd, h_lru, o_gla, d_skip_w, ssd_norm_g[l], gla_norm_g[l],
                        w_branch_all, w_out_all, l, ln_g[l, 1], ln_b[l, 1],
                        n_ctx_blocks=n_ctx_blocks, skip_blocks=skip_blocks, alpha=alpha)
        n_ctx_now = 0 if last else n_ctx
        h = _ffn_call(h, mods, wup_all, wdn_all, (l, 1), ln_g[l, 2], ln_b[l, 2], j=2,
                      n_ctx_blocks=n_ctx_now // TOKEN_BLOCK, alpha=alpha, scatter_out=col_major)
    return h
```

```python
import functools
import math

import jax
import jax.numpy as jnp
from jax import lax
from jax.experimental import pallas as pl
from jax.experimental.pallas import tpu as pltpu

F32 = jnp.float32
BF16 = jnp.bfloat16

GRID_W = 64
CONV_W = 4
CONV_LEFT = 2
SSD_HEAD_DIM = 64
SSD_GROUPS = 4
SSD_STATE = 128
LRU_BLOCK = 64
LRU_C = 8.0
GLA_HEADS = 4
GLA_GATE_RANK = 16
GLA_TAU = 16.0
N_MOD = 9
N_BRANCH = 3
FFN_RES_W = 0.5
NORM_EPS = 1e-5

TOKEN_BLOCK = 256
SCAN_CHUNK = 128
SCAN_BATCH = 8
TOKEN_BATCH = 4
MERGE_BATCH = 2
PROJ_BATCH = 1
MXU_TILE = 256
LANES = 128
SUBLANES = 8
VMEM_LIMIT = 56 * 1024 * 1024


def _cparams(n_axes):
    return pltpu.CompilerParams(dimension_semantics=("arbitrary",) * n_axes,
                                vmem_limit_bytes=VMEM_LIMIT)


def _resident(block_shape, index_map):
    return pl.BlockSpec(block_shape, index_map, pipeline_mode=pl.Buffered(1))


def _stacked(arr, lead):
    n = len(lead)
    tail = tuple(arr.shape[n:])
    index = tuple(lead) + (0,) * len(tail)
    return pl.BlockSpec((None,) * n + tail, lambda *_: index, pipeline_mode=pl.Buffered(1))


def _sigmoid(x):
    return 0.5 * jnp.tanh(0.5 * x) + 0.5


def _silu(x):
    hx = 0.5 * x
    return hx * jnp.tanh(hx) + hx


def _softplus(x):
    return jnp.maximum(x, 0.0) + jnp.log1p(jnp.exp(-jnp.abs(x)))


def _log_sigmoid(x):
    return jnp.minimum(x, 0.0) - jnp.log(1.0 + jnp.exp(-jnp.abs(x)))


def _split_terms(x, n):
    terms = []
    for _ in range(n - 1):
        t = x.astype(BF16)
        terms.append(t)
        x = x - t.astype(F32)
    terms.append(x.astype(BF16))
    return terms


_dot = functools.partial(jnp.dot, preferred_element_type=F32)


def _exact_lhs_dot(t_bf16, x, n_terms):
    return _dot(jnp.concatenate([t_bf16] * n_terms, axis=1), jnp.concatenate(_split_terms(x, n_terms), axis=0))


def _exact_rhs_dot(x, e3_bf16):
    return _dot(jnp.concatenate(_split_terms(x, 3), axis=1), e3_bf16)


def _f32_dot_small_k(a, w):
    at = _split_terms(a, 3)
    wt = _split_terms(w, 3)
    pairs = [(0, 0), (0, 1), (0, 2), (1, 0), (1, 1), (2, 0)]
    lhs = jnp.concatenate([at[i] for i, _ in pairs], axis=1)
    rhs = jnp.concatenate([wt[j] for _, j in pairs], axis=0)
    return _dot(lhs, rhs)


def _layer_norm(z, g, b):
    mu = jnp.mean(z, axis=-1, keepdims=True)
    zc = z - mu
    var = jnp.mean(zc * zc, axis=-1, keepdims=True)
    return zc * lax.rsqrt(var + NORM_EPS) * g + b


def _scan_order(i, rev, n_ctx, n_tot):
    back = jnp.where(i < n_ctx, n_ctx - 1 - i, n_ctx + n_tot - 1 - i)
    return jnp.where(rev == 0, i, back)


def _ada_kernel(s_ref, w_ref, b_ref, o_ref):
    s = _silu(s_ref[...])
    o_ref[...] = jnp.dot(s, w_ref[...], preferred_element_type=F32,
                         precision=lax.Precision.HIGHEST) + b_ref[...]


def _ada_call(cond, w_ada, b_ada):
    n_layers, d, nd = w_ada.shape
    rows = cond.shape[0]
    n_col = nd // d
    return pl.pallas_call(
        _ada_kernel,
        out_shape=jax.ShapeDtypeStruct((n_layers, rows, nd), F32),
        grid=(n_layers, n_col),
        in_specs=[
            pl.BlockSpec((rows, d), lambda l, j: (0, 0)),
            pl.BlockSpec((None, d, d), lambda l, j: (l, 0, j)),
            pl.BlockSpec((None, 1, d), lambda l, j: (l, 0, j)),
        ],
        out_specs=pl.BlockSpec((None, rows, d), lambda l, j: (l, 0, j)),
        compiler_params=_cparams(2),
        name="ada_mod",
    )(cond, w_ada, b_ada.reshape(n_layers, 1, nd))


def _grid_block_copies(grid_ref, buf_ref, sem, elem, bi, t, n_ctx_blocks, to_grid):
    n_lines = grid_ref.shape[1]
    lines_per_block = TOKEN_BLOCK // GRID_W
    ctx_lines = n_ctx_blocks * lines_per_block
    rows = n_lines - ctx_lines
    cols_per_block = TOKEN_BLOCK // rows

    def copy(grid_view, buf_view):
        src, dst = (buf_view, grid_view) if to_grid else (grid_view, buf_view)
        return pltpu.make_async_copy(src, dst, sem.at[bi])

    def ctx():
        return [copy(grid_ref.at[elem, t * lines_per_block + r], buf_ref.at[bi, pl.ds(r * GRID_W, GRID_W)])
                for r in range(lines_per_block)]

    def lat():
        col0 = (t - n_ctx_blocks) * cols_per_block
        return [copy(grid_ref.at[elem, pl.ds(ctx_lines, rows), col0 + c], buf_ref.at[bi, pl.ds(c * rows, rows)])
                for c in range(cols_per_block)]

    return ctx, lat


def _ffn_kernel(x_ref, m_ref, mctx_ref, wup_ref, wdn_ref, g_ref, b_ref, o_ref, *scratch, j, alpha, d_ff, ff_chunk,
                n_ctx_blocks, gather_in, scatter_out):
    b = pl.program_id(0)
    t = pl.program_id(1)
    is_ctx = t < n_ctx_blocks
    nb = m_ref.shape[0]
    buf, sem = scratch if (gather_in or scatter_out) else (None, None)

    def copies(bi):
        grid_ref = x_ref if gather_in else o_ref
        return _grid_block_copies(grid_ref, buf, sem, b * nb + bi, bi, t, n_ctx_blocks, to_grid=scatter_out)

    def for_block_kind(bi, action):
        ctx, lat = copies(bi)
        if n_ctx_blocks:
            @pl.when(is_ctx)
            def _():
                for cp in ctx():
                    action(cp)

        @pl.when(jnp.logical_not(is_ctx))
        def _():
            for cp in lat():
                action(cp)

    if gather_in:
        for bi in range(nb):
            for_block_kind(bi, lambda cp: cp.start())
    for bi in range(nb):
        if gather_in:
            for_block_kind(bi, lambda cp: cp.wait())
            x = buf[bi]
        else:
            x = x_ref[bi]
        m = jnp.where(is_ctx, mctx_ref[3 * j:3 * j + 3, :], m_ref[bi, 3 * j:3 * j + 3, :])
        shift, scale, gate = m[0:1], m[1:2], m[2:3]
        u = (x * (1.0 + scale) + shift).astype(BF16)
        acc = None
        for c0 in range(0, d_ff, ff_chunk):
            a = _dot(u, wup_ref[:, c0:c0 + ff_chunk])
            v = _dot(u, wup_ref[:, d_ff + c0:d_ff + c0 + ff_chunk])
            hid = (_silu(a) * v).astype(BF16)
            part = _dot(hid, wdn_ref[c0:c0 + ff_chunk, :])
            acc = part if acc is None else acc + part
        z = alpha * x + (FFN_RES_W * gate) * acc
        out = _layer_norm(z, g_ref[...], b_ref[...])
        if scatter_out:
            buf[bi] = out
            for_block_kind(bi, lambda cp: cp.start())
        else:
            o_ref[bi] = out
    if scatter_out:
        for bi in range(nb):
            for_block_kind(bi, lambda cp: cp.wait())


def _ffn_call(h, mods, w_up, w_dn, lead, ln_g, ln_b, *, j, n_ctx_blocks, alpha, gather_in=False, scatter_out=False):
    assert not (gather_in and scatter_out)
    n_b, tt, d = h.shape
    d_ff = w_dn.shape[-2]
    ff_chunk = d_ff // 2 if (d_ff // 2) % LANES == 0 else d_ff
    ctx_row = mods.shape[0] - 1
    n_blocks = tt // TOKEN_BLOCK
    nb = TOKEN_BATCH if n_b % TOKEN_BATCH == 0 else 1
    grid_view = (n_b, tt // GRID_W, GRID_W, d)
    block_spec = pl.BlockSpec((nb, TOKEN_BLOCK, d), lambda b, t: (b, t, 0))
    any_spec = pl.BlockSpec(memory_space=pl.ANY)
    scratch = []
    if gather_in or scatter_out:
        scratch = [pltpu.VMEM((nb, TOKEN_BLOCK, d), F32), pltpu.SemaphoreType.DMA((nb,))]

    kern = functools.partial(_ffn_kernel, j=j, alpha=alpha, d_ff=d_ff, ff_chunk=ff_chunk, n_ctx_blocks=n_ctx_blocks,
                             gather_in=gather_in, scatter_out=scatter_out)
    out = pl.pallas_call(
        kern,
        out_shape=jax.ShapeDtypeStruct(grid_view if scatter_out else h.shape, F32),
        grid=(n_b // nb, n_blocks),
        in_specs=[
            any_spec if gather_in else block_spec,
            pl.BlockSpec((nb, N_MOD, d), lambda b, t: (b, 0, 0)),
            pl.BlockSpec((None, N_MOD, d), lambda b, t: (ctx_row, 0, 0)),
            _stacked(w_up, lead),
            _stacked(w_dn, lead),
            _resident((1, d), lambda b, t: (0, 0)),
            _resident((1, d), lambda b, t: (0, 0)),
        ],
        out_specs=any_spec if scatter_out else block_spec,
        scratch_shapes=scratch,
        compiler_params=_cparams(2),
        name="ffn_sublayer",
    )(h.reshape(grid_view) if gather_in else h, mods, mods, w_up, w_dn, ln_g.reshape(1, d), ln_b.reshape(1, d))
    return out.reshape(h.shape) if scatter_out else out


def _proj_layout(d, ssd_heads):
    ssd_w = d
    bc_w = SSD_GROUPS * SSD_STATE
    gla_key = d // 2
    names = [("ssd_x", ssd_w), ("ssd_b", bc_w), ("ssd_c", bc_w), ("lru_x", d), ("ssd_z", ssd_w),
             ("lru_g", d), ("gla_q", gla_key), ("gla_k", gla_key), ("gla_v", d), ("gla_g", d),
             ("gates", N_BRANCH * d), ("small", LANES)]
    off, out = 0, {}
    for name, width in names:
        out[name] = (off, width)
        off += width
    return out, off


_ACTIVATIONS = {"none": lambda v: v, "silu": _silu, "sigmoid": _sigmoid, "gelu": jax.nn.gelu}


def _proj_kernel(x_ref, m_ref, mctx_ref, w_ref, cw_ref, cb_ref, o_ref, sm_ref, ybuf, *, segments, main_cols,
                 n_ctx_blocks, line_ctx, line_lat):
    is_ctx = pl.program_id(1) < n_ctx_blocks
    unit = min(line_ctx, line_lat, x_ref.shape[1])
    line_shift = jnp.where(is_ctx, int(math.log2(min(line_ctx, x_ref.shape[1]) // unit)),
                           int(math.log2(line_lat // unit)))
    n_strips = ybuf.shape[0] // x_ref.shape[0]
    for bi in range(x_ref.shape[0]):
        m = jnp.where(is_ctx, mctx_ref[3:5, :], m_ref[bi, 3:5, :])
        _proj_block(x_ref.at[bi], m[0:1], m[1:2], w_ref, cw_ref, cb_ref, o_ref.at[bi], sm_ref.at[bi],
                    ybuf.at[bi * n_strips:(bi + 1) * n_strips], line_shift, unit, segments, main_cols)


def _proj_block(x_ref, shift, scale, w_ref, cw_ref, cb_ref, o_ref, sm_ref, ybuf, line_shift, unit, segments,
                main_cols):
    x = x_ref[...]
    u = (x * (1.0 + scale) + shift).astype(BF16)
    rows = x.shape[0]
    halo = SUBLANES
    sm_ref[...] = _dot(u, w_ref[:, main_cols:main_cols + LANES])
    zeros = jnp.zeros((halo, LANES), F32)
    bases = [pl.multiple_of(halo + c * unit + lax.shift_right_logical(jnp.int32(c), line_shift) * halo, halo)
             for c in range(rows // unit)]
    for c0, cw, conv, act in segments:
        y = _dot(u, w_ref[:, c0:c0 + cw])
        if not conv:
            o_ref[:, c0:c0 + cw] = _ACTIVATIONS[act](y).astype(o_ref.dtype)
            continue
        for j in range(cw // LANES):
            ybuf[j, 0:halo, :] = zeros
            for c, base in enumerate(bases):
                ybuf[j, pl.ds(base, unit), :] = y[c * unit:(c + 1) * unit, j * LANES:(j + 1) * LANES]
                ybuf[j, pl.ds(base + unit, halo), :] = zeros
        for j in range(cw // LANES):
            cols = slice(c0 + j * LANES, c0 + (j + 1) * LANES)
            for c, base in enumerate(bases):
                acc = cb_ref[:, cols]
                for k in range(CONV_W):
                    acc = acc + ybuf[j, pl.ds(base + (k - CONV_LEFT), unit), :] * cw_ref[k:k + 1, cols]
                o_ref[c * unit:(c + 1) * unit, cols] = _ACTIVATIONS[act](acc).astype(o_ref.dtype)


def _proj_segments(layout, col_chunk):
    kinds = {"ssd_x": (True, "silu"), "ssd_b": (True, "silu"), "ssd_c": (True, "silu"), "lru_x": (True, "none"),
             "lru_g": (False, "gelu"), "gates": (False, "sigmoid"), "ssd_z": (False, "silu"),
             "gla_g": (False, "silu"), "gla_v": (False, "none"), "gla_q": (False, "none"), "gla_k": (False, "none")}
    conv_segs, plain_segs = [], []
    for name, (conv, act) in kinds.items():
        off, width = layout[name]
        for c0 in range(off, off + width, col_chunk):
            (conv_segs if conv else plain_segs).append((c0, min(col_chunk, off + width - c0), conv, act))
    segs = []
    while conv_segs or plain_segs:
        if conv_segs:
            segs.append(conv_segs.pop(0))
        if plain_segs:
            segs.append(plain_segs.pop(0))
    return tuple(segs)


def _proj_call(h, mods, w_perm, layer, conv_w, conv_b, layout, *, n_ctx_blocks, line_ctx, line_lat):
    n_b, tt, d = h.shape
    total = w_perm.shape[-1]
    conv_cols = conv_w.shape[1]
    ctx_row = mods.shape[0] - 1
    n_blocks = tt // TOKEN_BLOCK
    col_chunk = 1024
    nb = PROJ_BATCH if n_b % PROJ_BATCH == 0 else 1
    unit = min(line_ctx, line_lat, TOKEN_BLOCK)
    slab_rows = TOKEN_BLOCK + (TOKEN_BLOCK // unit + 2) * SUBLANES

    main = total - LANES
    kern = functools.partial(_proj_kernel, segments=_proj_segments(layout, col_chunk), main_cols=main,
                             n_ctx_blocks=n_ctx_blocks, line_ctx=line_ctx, line_lat=line_lat)
    return pl.pallas_call(
        kern,
        out_shape=(jax.ShapeDtypeStruct((n_b, tt, main), BF16), jax.ShapeDtypeStruct((n_b, tt, LANES), F32)),
        grid=(n_b // nb, n_blocks),
        in_specs=[
            pl.BlockSpec((nb, TOKEN_BLOCK, d), lambda b, t: (b, t, 0)),
            pl.BlockSpec((nb, N_MOD, d), lambda b, t: (b, 0, 0)),
            pl.BlockSpec((None, N_MOD, d), lambda b, t: (ctx_row, 0, 0)),
            _stacked(w_perm, (layer,)),
            _resident((CONV_W, conv_cols), lambda b, t: (0, 0)),
            _resident((1, conv_cols), lambda b, t: (0, 0)),
        ],
        out_specs=(pl.BlockSpec((nb, TOKEN_BLOCK, main), lambda b, t: (b, t, 0)),
                   pl.BlockSpec((nb, TOKEN_BLOCK, LANES), lambda b, t: (b, t, 0))),
        scratch_shapes=[pltpu.VMEM((nb * (col_chunk // LANES), slab_rows, LANES), F32)],
        compiler_params=_cparams(2),
        name="mixer_in_proj",
    )(h, mods, mods, w_perm, conv_w, conv_b)


def _ssd_chunk(fwd, x_ref, b_ref, c_ref, sm_ref, dtb_ref, alog_ref, exp_ref, y_ref, state_ref, n_heads):
    q = x_ref.shape[0]
    hg = n_heads // SSD_GROUPS
    gw = hg * SSD_HEAD_DIM
    sm = sm_ref[...]
    dt_raw = jnp.where(fwd, sm[:, 0:n_heads], sm[:, n_heads:2 * n_heads])
    dt_bias = jnp.where(fwd, dtb_ref[0:1, :], dtb_ref[1:2, :])
    a_log = jnp.where(fwd, alog_ref[0:1, :], alog_ref[1:2, :])
    dt = _softplus(dt_raw + dt_bias)
    a = dt * (-jnp.exp(a_log))
    row = lax.broadcasted_iota(jnp.int32, (q, q), 0)
    col = lax.broadcasted_iota(jnp.int32, (q, q), 1)
    mask = (col - row) * jnp.where(fwd, 1, -1) <= 0
    tri = jnp.where(mask, 1.0, 0.0).astype(BF16)
    cs = _exact_lhs_dot(tri, a, 3)
    tot = jnp.sum(a, axis=0, keepdims=True)
    cs2 = cs * math.log2(math.e)
    src2 = cs2 - jnp.log2(dt)
    pad = jnp.zeros((q, LANES - n_heads), F32)
    src2_t = jnp.concatenate([src2, pad], axis=1).T[0:n_heads]

    w_state = dt * jnp.exp(tot - cs)
    from_start = jnp.exp(cs)
    e_tot = jnp.broadcast_to(jnp.exp(tot), (SUBLANES, n_heads))
    wide = _exact_rhs_dot(jnp.concatenate([w_state, from_start, e_tot], axis=0), exp_ref[...])
    w_state_w = wide[0:q]
    from_start_w = wide[q:2 * q]
    e_tot_w = wide[2 * q:2 * q + 1]

    x_bf = x_ref[...]
    xw_bf = (x_bf.astype(F32) * w_state_w).astype(BF16)
    pair_w = 2 * SSD_HEAD_DIM
    lane = lax.broadcasted_iota(jnp.int32, (1, pair_w), 1)
    keep_lo = jnp.where(lane < SSD_HEAD_DIM, 1.0, 0.0).astype(BF16)
    keep_hi = jnp.where(lane < SSD_HEAD_DIM, 0.0, 1.0).astype(BF16)
    for g in range(SSD_GROUPS):
        bt = b_ref[:, g * SSD_STATE:(g + 1) * SSD_STATE].T
        cg = c_ref[:, g * SSD_STATE:(g + 1) * SSD_STATE]
        scores = _dot(cg, bt)
        s_prev = state_ref[g]
        y_off = _dot(cg, s_prev.astype(BF16)) * from_start_w[:, g * gw:(g + 1) * gw]
        for pp in range(hg // 2):
            ms = []
            for head in (g * hg + 2 * pp, g * hg + 2 * pp + 1):
                seg2 = jnp.where(mask, cs2[:, head:head + 1] - src2_t[head:head + 1, :], -1e30)
                ms.append((scores * jnp.exp2(seg2)).astype(BF16))
            cols = slice(g * gw + pp * pair_w, g * gw + (pp + 1) * pair_w)
            xp = x_bf[:, cols]
            y_pair = _dot(jnp.concatenate(ms, axis=1), jnp.concatenate([xp * keep_lo, xp * keep_hi], axis=0))
            y_ref[:, cols] = (y_off[:, pp * pair_w:(pp + 1) * pair_w] + y_pair).astype(y_ref.dtype)
        upd = _dot(bt, xw_bf[:, g * gw:(g + 1) * gw])
        state_ref[g] = s_prev * e_tot_w[:, g * gw:(g + 1) * gw] + upd


def _scan_col_spec(layout, name, nb, rows, n_ctx, n_tot):
    off, w = layout[name]
    assert off % w == 0
    blk = off // w
    return pl.BlockSpec((nb, rows, w), lambda b, r, i: (b, _scan_order(i, r, n_ctx, n_tot), blk))


def _scan_small_spec(nb, rows, n_ctx, n_tot):
    return pl.BlockSpec((nb, rows, LANES), lambda b, r, i: (b, _scan_order(i, r, n_ctx, n_tot), 0))


def _scan_out_spec(nb, rows, width, n_ctx, n_tot):
    return pl.BlockSpec((None, nb, rows, width), lambda b, r, i: (r, b, _scan_order(i, r, n_ctx, n_tot), 0))


def _lru_scan_block(a_ref, b_ref, o_ref, carry_ref, reverse):
    rows, width = a_ref.shape
    sub = lax.broadcasted_iota(jnp.int32, (SUBLANES, width), 0)
    carry = carry_ref[...]
    n_groups = rows // SUBLANES
    order = range(n_groups - 1, -1, -1) if reverse else range(n_groups)
    for gi in order:
        sl = slice(gi * SUBLANES, (gi + 1) * SUBLANES)
        a = a_ref[sl, :]
        b = b_ref[sl, :]
        k = 1
        while k < SUBLANES:
            if reverse:
                a_sh = pltpu.roll(a, SUBLANES - k, axis=0)
                b_sh = pltpu.roll(b, SUBLANES - k, axis=0)
                ok = sub < SUBLANES - k
            else:
                a_sh = pltpu.roll(a, k, axis=0)
                b_sh = pltpu.roll(b, k, axis=0)
                ok = sub >= k
            b = b + a * jnp.where(ok, b_sh, 0.0)
            a = a * jnp.where(ok, a_sh, 1.0)
            k *= 2
        hblk = b + a * carry
        b_ref[sl, :] = hblk
        carry = hblk[0:1, :] if reverse else hblk[SUBLANES - 1:SUBLANES, :]
    carry_ref[...] = carry
    o_ref[...] = b_ref[...].astype(o_ref.dtype)


def _lru_kernel(x_ref, w_ref, ba_ref, bx_ref, lam_ref, h_ref, a_s, b_s, carry_ref):
    rev = pl.program_id(1)
    i = pl.program_id(2)
    nb = x_ref.shape[0]

    @pl.when(i == 0)
    def _():
        carry_ref[...] = jnp.zeros_like(carry_ref)

    n_tiles = w_ref.shape[0]
    neg_c_sp = (-LRU_C) * _softplus(-lam_ref[...])
    for bi in range(nb):
        x_bf = x_ref[bi]
        pre_a, pre_x = [], []
        for t in range(n_tiles):
            y = _dot(x_bf[:, t * MXU_TILE:(t + 1) * MXU_TILE], w_ref[t])
            pre_a.append(y[:, :MXU_TILE])
            pre_x.append(y[:, MXU_TILE:])
        r = _sigmoid(jnp.concatenate(pre_a, axis=1) + ba_ref[...])
        ig = _sigmoid(jnp.concatenate(pre_x, axis=1) + bx_ref[...])
        log_a = r * neg_c_sp
        a = jnp.exp(log_a)
        a_s[bi] = a
        b_s[bi] = jnp.sqrt(-jnp.tanh(log_a) * (a * a + 1.0)) * (ig * x_bf.astype(F32))

    @pl.when(rev == 0)
    def _():
        for bi in range(nb):
            _lru_scan_block(a_s.at[bi], b_s.at[bi], h_ref.at[bi], carry_ref.at[bi], reverse=False)

    @pl.when(rev == 1)
    def _():
        for bi in range(nb):
            _lru_scan_block(a_s.at[bi], b_s.at[bi], h_ref.at[bi], carry_ref.at[bi], reverse=True)


def _lru_call(p, layout, w_tiles, layer, b_a, b_x, lam, *, n_ctx_blocks):
    n_b, tt, _ = p.shape
    width = layout["lru_x"][1]
    n_blocks = tt // TOKEN_BLOCK
    n_tiles = w_tiles.shape[2]
    nb = SCAN_BATCH if n_b % SCAN_BATCH == 0 else 1

    return pl.pallas_call(
        _lru_kernel,
        out_shape=jax.ShapeDtypeStruct((2, n_b, tt, width), BF16),
        grid=(n_b // nb, 2, n_blocks),
        in_specs=[
            _scan_col_spec(layout, "lru_x", nb, TOKEN_BLOCK, n_ctx_blocks, n_blocks),
            pl.BlockSpec((None, None, n_tiles, MXU_TILE, 2 * MXU_TILE), lambda b, r, i: (layer, r, 0, 0, 0)),
            pl.BlockSpec((None, 1, width), lambda b, r, i: (r, 0, 0)),
            pl.BlockSpec((None, 1, width), lambda b, r, i: (r, 0, 0)),
            pl.BlockSpec((None, 1, width), lambda b, r, i: (r, 0, 0)),
        ],
        out_specs=_scan_out_spec(nb, TOKEN_BLOCK, width, n_ctx_blocks, n_blocks),
        scratch_shapes=[pltpu.VMEM((nb, TOKEN_BLOCK, width), F32), pltpu.VMEM((nb, TOKEN_BLOCK, width), F32),
                        pltpu.VMEM((nb, 1, width), F32)],
        compiler_params=_cparams(3),
        name="rglru_scan",
    )(p, w_tiles, b_a, b_x, lam)


def _gla_chunk(fwd, q_ref, k_ref, v_ref, sm_ref, wg_ref, bg_ref, o_ref, state_ref, alr_off, dk, dv):
    q_len = q_ref.shape[0]
    n_heads = q_ref.shape[1] // dk
    sm = sm_ref[...]
    r = GLA_GATE_RANK
    a_lr = jnp.where(fwd, sm[:, alr_off:alr_off + r], sm[:, alr_off + r:alr_off + 2 * r])
    z = _f32_dot_small_k(a_lr, wg_ref[...]) + bg_ref[...]
    log_a = _log_sigmoid(z) * (1.0 / GLA_TAU)
    row = lax.broadcasted_iota(jnp.int32, (q_len, q_len), 0)
    col = lax.broadcasted_iota(jnp.int32, (q_len, q_len), 1)
    mask = (col - row) * jnp.where(fwd, 1, -1) <= 0
    tri = jnp.where(mask, 1.0, 0.0).astype(BF16)
    bcum = _exact_lhs_dot(tri, log_a, 2)
    btot = jnp.sum(log_a, axis=0, keepdims=True)
    bmid = bcum[q_len // 2:q_len // 2 + 1, :]
    q_in = q_ref[...].astype(F32) * (dk ** -0.5) * jnp.exp(bcum - bmid)
    k_in = k_ref[...].astype(F32) * jnp.exp(bmid - bcum)
    q_off = (q_in * jnp.exp(bmid)).astype(BF16)
    k_st = (k_in * jnp.exp(btot - bmid)).astype(BF16)
    q_in = q_in.astype(BF16)
    k_in = k_in.astype(BF16)
    e_tot = jnp.exp(btot)
    for hh in range(n_heads):
        ks = slice(hh * dk, (hh + 1) * dk)
        vs = slice(hh * dv, (hh + 1) * dv)
        v_bf = v_ref[:, vs]
        att = lax.dot_general(q_in[:, ks], k_in[:, ks], (((1,), (1,)), ((), ())), preferred_element_type=F32)
        att = jnp.where(mask, att, 0.0).astype(BF16)
        s_prev = state_ref[hh]
        inter = lax.dot_general(q_off[:, ks], s_prev.astype(BF16), (((1,), (1,)), ((), ())),
                                preferred_element_type=F32)
        o_ref[:, vs] = (_dot(att, v_bf) + inter).astype(o_ref.dtype)
        upd = lax.dot_general(v_bf, k_st[:, ks], (((0,), (0,)), ((), ())), preferred_element_type=F32)
        state_ref[hh] = s_prev * e_tot[:, ks] + upd


def _chunk_scans_kernel(x_ref, b_ref, c_ref, q_ref, k_ref, v_ref, sm_ref, dtb_ref, alog_ref, exp_ref, wg_ref, bg_ref,
                        y_ref, o_ref, ssd_state, gla_state, *, n_heads, alr_off, dk, dv):
    rev = pl.program_id(1)
    i = pl.program_id(2)

    @pl.when(i == 0)
    def _():
        ssd_state[...] = jnp.zeros_like(ssd_state)
        gla_state[...] = jnp.zeros_like(gla_state)

    for bi in range(x_ref.shape[0]):
        _ssd_chunk(rev == 0, x_ref.at[bi], b_ref.at[bi], c_ref.at[bi], sm_ref.at[bi], dtb_ref, alog_ref, exp_ref,
                   y_ref.at[bi], ssd_state.at[bi], n_heads)
        _gla_chunk(rev == 0, q_ref.at[bi], k_ref.at[bi], v_ref.at[bi], sm_ref.at[bi], wg_ref, bg_ref,
                   o_ref.at[bi], gla_state.at[bi], alr_off, dk, dv)


def _chunk_scans_call(p, small, layout, dt_bias, a_log, expand, w_gate, b_gate, *, alr_off, n_ctx_chunks):
    n_b, tt, _ = p.shape
    n_heads = dt_bias.shape[1]
    ssd_w = n_heads * SSD_HEAD_DIM
    hg = n_heads // SSD_GROUPS
    key_w = layout["gla_q"][1]
    val_w = layout["gla_v"][1]
    dk = key_w // GLA_HEADS
    dv = val_w // GLA_HEADS
    n_chunks = tt // SCAN_CHUNK
    nb = SCAN_BATCH if n_b % SCAN_BATCH == 0 else 1
    col_spec = functools.partial(_scan_col_spec, layout, nb=nb, rows=SCAN_CHUNK, n_ctx=n_ctx_chunks, n_tot=n_chunks)

    kern = functools.partial(_chunk_scans_kernel, n_heads=n_heads, alr_off=alr_off, dk=dk, dv=dv)
    return pl.pallas_call(
        kern,
        out_shape=(jax.ShapeDtypeStruct((2, n_b, tt, ssd_w), BF16), jax.ShapeDtypeStruct((2, n_b, tt, val_w), BF16)),
        grid=(n_b // nb, 2, n_chunks),
        in_specs=[
            col_spec("ssd_x"), col_spec("ssd_b"), col_spec("ssd_c"),
            col_spec("gla_q"), col_spec("gla_k"), col_spec("gla_v"),
            _scan_small_spec(nb, SCAN_CHUNK, n_ctx_chunks, n_chunks),
            _resident((2, n_heads), lambda b, r, i: (0, 0)),
            _resident((2, n_heads), lambda b, r, i: (0, 0)),
            _resident(expand.shape, lambda b, r, i: (0, 0)),
            pl.BlockSpec((None, GLA_GATE_RANK, key_w), lambda b, r, i: (r, 0, 0)),
            pl.BlockSpec((None, 1, key_w), lambda b, r, i: (r, 0, 0)),
        ],
        out_specs=(_scan_out_spec(nb, SCAN_CHUNK, ssd_w, n_ctx_chunks, n_chunks),
                   _scan_out_spec(nb, SCAN_CHUNK, val_w, n_ctx_chunks, n_chunks)),
        scratch_shapes=[pltpu.VMEM((nb, SSD_GROUPS, SSD_STATE, hg * SSD_HEAD_DIM), F32),
                        pltpu.VMEM((nb, GLA_HEADS, dv, dk), F32)],
        compiler_params=_cparams(3),
        name="ssd_gla_scan",
    )(p, p, p, p, p, p, small, dt_bias, a_log, expand, w_gate, b_gate)


def _merge_kernel(h_ref, m_ref, mctx_ref, ys0_ref, ys1_ref, xs_ref, z_ref, hl0_ref, hl1_ref, gl_ref,
                  og0_ref, og1_ref, gg_ref, gt0_ref, gt1_ref, gt2_ref, dskip_ref, sng_ref, gng_ref,
                  wbr_ref, wout_ref, lng_ref, lnb_ref, o_ref, *, alpha, dv, n_ctx_blocks, skip_blocks):
    is_ctx = pl.program_id(1) + skip_blocks < n_ctx_blocks
    for bi in range(h_ref.shape[0]):
        gate = jnp.where(is_ctx, mctx_ref[5:6, :], m_ref[bi, 5:6, :])
        _merge_block(h_ref.at[bi], gate, ys0_ref.at[bi], ys1_ref.at[bi], xs_ref.at[bi], z_ref.at[bi],
                     hl0_ref.at[bi], hl1_ref.at[bi], gl_ref.at[bi], og0_ref.at[bi], og1_ref.at[bi], gg_ref.at[bi],
                     gt0_ref.at[bi], gt1_ref.at[bi], gt2_ref.at[bi], dskip_ref, sng_ref, gng_ref, wbr_ref, wout_ref,
                     lng_ref, lnb_ref, o_ref.at[bi], alpha, dv)


def _merge_block(h_ref, gate, ys0_ref, ys1_ref, xs_ref, z_ref, hl0_ref, hl1_ref, gl_ref, og0_ref, og1_ref, gg_ref,
                 gt0_ref, gt1_ref, gt2_ref, dskip_ref, sng_ref, gng_ref, wbr_ref, wout_ref, lng_ref, lnb_ref, o_ref,
                 alpha, dv):
    d = h_ref.shape[1]

    def f32(ref, cols=slice(None)):
        return ref[:, cols].astype(F32)

    y = (f32(ys0_ref) + f32(ys1_ref) + dskip_ref[...] * f32(xs_ref)) * f32(z_ref)
    y = y * lax.rsqrt(jnp.mean(y * y, axis=-1, keepdims=True) + NORM_EPS) * sng_ref[...]
    m = f32(gt0_ref) * _dot(y.astype(BF16), wbr_ref[0])
    y = (f32(hl0_ref) + f32(hl1_ref)) * f32(gl_ref)
    m = m + f32(gt1_ref) * _dot(y.astype(BF16), wbr_ref[1])
    parts = []
    for hh in range(d // dv):
        vs = slice(hh * dv, (hh + 1) * dv)
        o = f32(og0_ref, vs) + f32(og1_ref, vs)
        o = o * lax.rsqrt(jnp.mean(o * o, axis=-1, keepdims=True) + NORM_EPS) * gng_ref[...]
        parts.append(o * f32(gg_ref, vs))
    y = jnp.concatenate(parts, axis=1)
    m = m + f32(gt2_ref) * _dot(y.astype(BF16), wbr_ref[2])
    out = _dot(m.astype(BF16), wout_ref[...])
    z = alpha * h_ref[...] + gate * out
    o_ref[...] = _layer_norm(z, lng_ref[...], lnb_ref[...])


def _merge_call(h, mods, p, layout, y_ssd, h_lru, o_gla, d_skip_w, ssd_norm_g, gla_norm_g, w_branch, w_out, layer,
                ln_g, ln_b, *, n_ctx_blocks, skip_blocks, alpha):
    n_b, tt, d = h.shape
    n_blocks = tt // TOKEN_BLOCK - skip_blocks
    ctx_row = mods.shape[0] - 1
    dv = gla_norm_g.shape[-1]
    nb = MERGE_BATCH if n_b % MERGE_BATCH == 0 else 1

    def tok(width, blk):
        return pl.BlockSpec((nb, TOKEN_BLOCK, width), lambda b, t: (b, t + skip_blocks, blk))

    def col_spec(name):
        off, w = layout[name]
        blk = off // w
        assert blk * w == off
        return tok(w, blk)

    def gate_spec(n):
        off = layout["gates"][0] + n * d
        assert off % d == 0
        return tok(d, off // d)

    def dir_spec(rv):
        return pl.BlockSpec((None, nb, TOKEN_BLOCK, d), lambda b, t: (rv, b, t + skip_blocks, 0))

    kern = functools.partial(_merge_kernel, alpha=alpha, dv=dv, n_ctx_blocks=n_ctx_blocks, skip_blocks=skip_blocks)
    return pl.pallas_call(
        kern,
        out_shape=jax.ShapeDtypeStruct((n_b, n_blocks * TOKEN_BLOCK, d), F32),
        grid=(n_b // nb, n_blocks),
        in_specs=[
            tok(d, 0),
            pl.BlockSpec((nb, N_MOD, d), lambda b, t: (b, 0, 0)),
            pl.BlockSpec((None, N_MOD, d), lambda b, t: (ctx_row, 0, 0)),
            dir_spec(0), dir_spec(1), col_spec("ssd_x"), col_spec("ssd_z"),
            dir_spec(0), dir_spec(1), col_spec("lru_g"),
            dir_spec(0), dir_spec(1), col_spec("gla_g"),
            gate_spec(0), gate_spec(1), gate_spec(2),
            _resident((1, d), lambda b, t: (0, 0)),
            _resident((1, d), lambda b, t: (0, 0)),
            _resident((1, dv), lambda b, t: (0, 0)),
            _stacked(w_branch, (layer,)),
            _stacked(w_out, (layer,)),
            _resident((1, d), lambda b, t: (0, 0)),
            _resident((1, d), lambda b, t: (0, 0)),
        ],
        out_specs=pl.BlockSpec((nb, TOKEN_BLOCK, d), lambda b, t: (b, t, 0)),
        compiler_params=_cparams(2),
        name="mixer_merge",
    )(h, mods, mods, y_ssd, y_ssd, p, p, h_lru, h_lru, p, o_gla, o_gla, p, p, p, p,
      d_skip_w, ssd_norm_g.reshape(1, d), gla_norm_g.reshape(1, dv), w_branch, w_out,
      ln_g.reshape(1, d), ln_b.reshape(1, d))


def _permute_w_kernel(w_ref, o_ref, *, moves, tail):
    in_total = w_ref.shape[1]

    def piece(src, width):
        a0 = src // LANES * LANES
        a1 = min(-(-(src + width) // LANES) * LANES, in_total)
        return w_ref[:, a0:a1][:, src - a0:src - a0 + width]

    for src, dst, width in moves:
        o_ref[:, dst:dst + width] = piece(src, width).astype(o_ref.dtype)
    parts = [piece(src, width) for src, width in tail]
    used = sum(width for _, width in tail)
    parts.append(jnp.zeros((w_ref.shape[0], LANES - used), w_ref.dtype))
    o_ref[:, o_ref.shape[1] - LANES:] = jnp.concatenate(parts, axis=1).astype(o_ref.dtype)


def _permute_w_in(w_in, d, ssd_heads, total):
    ssd_w = d
    bc_w = SSD_GROUPS * SSD_STATE
    gla_key = d // 2
    sizes = (ssd_w, ssd_w + 2 * bc_w, 2 * ssd_heads, d, d, gla_key, gla_key, d, d, 2 * GLA_GATE_RANK, N_BRANCH * d)
    src = [0]
    for s in sizes:
        src.append(src[-1] + s)
    z, xbc, dtr, lx, lg, gq, gk, gv, gg, alr, gates = range(len(sizes))
    order = (xbc, lx, z, lg, gq, gk, gv, gg, gates)
    moves, dst = [], 0
    for n in order:
        assert dst % LANES == 0
        moves.append((src[n], dst, sizes[n]))
        dst += sizes[n]
    assert dst == total - LANES
    tail = ((src[dtr], sizes[dtr]), (src[alr], sizes[alr]))
    n_layers, rows, in_total = w_in.shape
    row_block = 256 if rows % 256 == 0 else rows
    kern = functools.partial(_permute_w_kernel, moves=tuple(moves), tail=tail)
    return pl.pallas_call(
        kern,
        out_shape=jax.ShapeDtypeStruct((n_layers, rows, total), BF16),
        grid=(n_layers, rows // row_block),
        in_specs=[pl.BlockSpec((None, row_block, in_total), lambda l, r: (l, r, 0))],
        out_specs=pl.BlockSpec((None, row_block, total), lambda l, r: (l, r, 0)),
        compiler_params=_cparams(2),
        name="permute_w_in",
    )(w_in)


def _lru_gate_tiles(w_a, w_x):
    per = MXU_TILE // LRU_BLOCK
    n_l, n_dir, nb, k, _ = w_a.shape
    n_tiles = nb // per

    def tiles(w):
        w = w.astype(BF16).reshape(n_l, n_dir, n_tiles, per, k, k)
        on_diag = jnp.eye(per, dtype=bool)[:, None, :, None]
        t = jnp.where(on_diag, w[:, :, :, :, :, None, :], jnp.zeros((), BF16))
        return t.reshape(n_l, n_dir, n_tiles, MXU_TILE, MXU_TILE)

    return jnp.concatenate([tiles(w_a), tiles(w_x)], axis=-1)


def kernel(x, c, ctx, c_ctx, w_ada, b_ada, ln_g, ln_b, ffn_w_up, ffn_w_down, w_in, ssd_conv_w, ssd_conv_b,
           ssd_dt_bias, ssd_a_log, ssd_d, ssd_norm_g, lru_conv_w, lru_conv_b, lru_w_a, lru_b_a, lru_w_x, lru_b_x,
           lru_lam, gla_w_gate, gla_b_gate, gla_norm_g, w_branch, w_out):
    n_b, t_lat, d = x.shape
    n_ctx = ctx.shape[1]
    depth = w_ada.shape[0]
    rows = t_lat // GRID_W
    ssd_heads = ssd_dt_bias.shape[-1]
    assert n_ctx % TOKEN_BLOCK == 0 and t_lat % TOKEN_BLOCK == 0
    for line in (n_ctx, GRID_W, rows):
        assert line & (line - 1) == 0 and TOKEN_BLOCK % min(line, TOKEN_BLOCK) == 0
    assert n_ctx <= TOKEN_BLOCK
    alpha = (2.0 * depth) ** 0.25
    n_ctx_blocks = n_ctx // TOKEN_BLOCK
    n_ctx_chunks = n_ctx // SCAN_CHUNK

    layout, total = _proj_layout(d, ssd_heads)
    alr_off = 2 * ssd_heads

    n_rows = -(-(n_b + 1) // SUBLANES) * SUBLANES
    cond = jnp.concatenate([c, jnp.zeros((n_rows - n_b - 1, d), F32), c_ctx[None, :]], axis=0)
    mods_all = _ada_call(cond, w_ada, b_ada).reshape(depth, n_rows, N_MOD, d)

    expand = jnp.repeat(jnp.eye(ssd_heads, dtype=BF16), SSD_HEAD_DIM, axis=1)
    expand = jnp.concatenate([expand] * 3, axis=0)
    wup_all = ffn_w_up.astype(BF16)
    wdn_all = ffn_w_down.astype(BF16)
    w_perm_all = _permute_w_in(w_in, d, ssd_heads, total)
    lru_tiles_all = _lru_gate_tiles(lru_w_a, lru_w_x)
    w_branch_all = w_branch.astype(BF16)
    w_out_all = w_out.astype(BF16)

    h = jnp.concatenate([ctx, x], axis=1)
    for l in range(depth):
        last = l == depth - 1
        col_major = l % 2 == 1
        mods = mods_all[l]

        h = _ffn_call(h, mods, wup_all, wdn_all, (l, 0), ln_g[l, 0], ln_b[l, 0], j=0, n_ctx_blocks=n_ctx_blocks,
                      alpha=alpha, gather_in=col_major)

        conv_w = jnp.concatenate([ssd_conv_w[l], lru_conv_w[l]], axis=1)
        conv_b = jnp.concatenate([ssd_conv_b[l], lru_conv_b[l]], axis=0)[None, :]
        p, small = _proj_call(h, mods, w_perm_all, l, conv_w, conv_b, layout, n_ctx_blocks=n_ctx_blocks,
                              line_ctx=n_ctx, line_lat=rows if col_major else GRID_W)

        y_ssd, o_gla = _chunk_scans_call(p, small, layout, ssd_dt_bias[l], ssd_a_log[l], expand, gla_w_gate[l],
                                         gla_b_gate[l][:, None, :], alr_off=alr_off, n_ctx_chunks=n_ctx_chunks)
        h_lru = _lru_call(p, layout, lru_tiles_all, l, lru_b_a[l][:, None, :], lru_b_x[l][:, None, :],
                          lru_lam[l][:, None, :], n_ctx_blocks=n_ctx_blocks)

        d_skip_w = jnp.repeat(ssd_d[l, 0] + ssd_d[l, 1], SSD_HEAD_DIM)[None, :]
        skip_blocks = n_ctx_blocks if last else 0
        h = _merge_call(h, mods, p, layout, y_ssd, h_lru, o_gla, d_skip_w, ssd_norm_g[l], gla_norm_g[l],
                        w_branch_all, w_out_all, l, ln_g[l, 1], ln_b[l, 1],
                        n_ctx_blocks=n_ctx_blocks, skip_blocks=skip_blocks, alpha=alpha)
        n_ctx_now = 0 if last else n_ctx
        h = _ffn_call(h, mods, wup_all, wdn_all, (l, 1), ln_g[l, 2], ln_b[l, 2], j=2,
                      n_ctx_blocks=n_ctx_now // TOKEN_BLOCK, alpha=alpha, scatter_out=col_major)
    return h
```

```python
import functools
import math

import jax
import jax.numpy as jnp
from jax import lax
from jax.experimental import pallas as pl
from jax.experimental.pallas import tpu as pltpu

F32 = jnp.float32
BF16 = jnp.bfloat16

GRID_W = 64
CONV_W = 4
CONV_LEFT = 2
SSD_HEAD_DIM = 64
SSD_GROUPS = 4
SSD_STATE = 128
LRU_BLOCK = 64
LRU_C = 8.0
GLA_HEADS = 4
GLA_GATE_RANK = 16
GLA_TAU = 16.0
N_MOD = 9
N_BRANCH = 3
FFN_RES_W = 0.5
NORM_EPS = 1e-5

TOKEN_BLOCK = 256
SCAN_CHUNK = 128
SCAN_BATCH = 8
TOKEN_BATCH = 4
MERGE_BATCH = 2
PROJ_BATCH = 1
MXU_TILE = 256
LANES = 128
SUBLANES = 8
VMEM_LIMIT = 56 * 1024 * 1024


def _cparams(n_axes):
    return pltpu.CompilerParams(dimension_semantics=("arbitrary",) * n_axes,
                                vmem_limit_bytes=VMEM_LIMIT)


def _resident(block_shape, index_map):
    return pl.BlockSpec(block_shape, index_map, pipeline_mode=pl.Buffered(1))


def _stacked(arr, lead):
    n = len(lead)
    tail = tuple(arr.shape[n:])
    index = tuple(lead) + (0,) * len(tail)
    return pl.BlockSpec((None,) * n + tail, lambda *_: index, pipeline_mode=pl.Buffered(1))


def _sigmoid(x):
    return 0.5 * jnp.tanh(0.5 * x) + 0.5


def _silu(x):
    hx = 0.5 * x
    return hx * jnp.tanh(hx) + hx


def _softplus(x):
    return jnp.maximum(x, 0.0) + jnp.log1p(jnp.exp(-jnp.abs(x)))


def _log_sigmoid(x):
    return jnp.minimum(x, 0.0) - jnp.log(1.0 + jnp.exp(-jnp.abs(x)))


def _split_terms(x, n):
    terms = []
    for _ in range(n - 1):
        t = x.astype(BF16)
        terms.append(t)
        x = x - t.astype(F32)
    terms.append(x.astype(BF16))
    return terms


_dot = functools.partial(jnp.dot, preferred_element_type=F32)


def _exact_lhs_dot(t_bf16, x, n_terms):
    return _dot(jnp.concatenate([t_bf16] * n_terms, axis=1), jnp.concatenate(_split_terms(x, n_terms), axis=0))


def _exact_rhs_dot(x, e3_bf16):
    return _dot(jnp.concatenate(_split_terms(x, 3), axis=1), e3_bf16)


def _f32_dot_small_k(a, w):
    at = _split_terms(a, 3)
    wt = _split_terms(w, 3)
    pairs = [(0, 0), (0, 1), (0, 2), (1, 0), (1, 1), (2, 0)]
    lhs = jnp.concatenate([at[i] for i, _ in pairs], axis=1)
    rhs = jnp.concatenate([wt[j] for _, j in pairs], axis=0)
    return _dot(lhs, rhs)


def _layer_norm(z, g, b):
    mu = jnp.mean(z, axis=-1, keepdims=True)
    zc = z - mu
    var = jnp.mean(zc * zc, axis=-1, keepdims=True)
    return zc * lax.rsqrt(var + NORM_EPS) * g + b


def _scan_order(i, rev, n_ctx, n_tot):
    back = jnp.where(i < n_ctx, n_ctx - 1 - i, n_ctx + n_tot - 1 - i)
    return jnp.where(rev == 0, i, back)


def _ada_kernel(s_ref, w_ref, b_ref, o_ref):
    s = _silu(s_ref[...])
    o_ref[...] = jnp.dot(s, w_ref[...], preferred_element_type=F32,
                         precision=lax.Precision.HIGHEST) + b_ref[...]


def _ada_call(cond, w_ada, b_ada):
    n_layers, d, nd = w_ada.shape
    rows = cond.shape[0]
    n_col = nd // d
    return pl.pallas_call(
        _ada_kernel,
        out_shape=jax.ShapeDtypeStruct((n_layers, rows, nd), F32),
        grid=(n_layers, n_col),
        in_specs=[
            pl.BlockSpec((rows, d), lambda l, j: (0, 0)),
            pl.BlockSpec((None, d, d), lambda l, j: (l, 0, j)),
            pl.BlockSpec((None, 1, d), lambda l, j: (l, 0, j)),
        ],
        out_specs=pl.BlockSpec((None, rows, d), lambda l, j: (l, 0, j)),
        compiler_params=_cparams(2),
        name="ada_mod",
    )(cond, w_ada, b_ada.reshape(n_layers, 1, nd))


def _grid_block_copies(grid_ref, buf_ref, sem, elem, bi, t, n_ctx_blocks, to_grid):
    n_lines = grid_ref.shape[1]
    lines_per_block = TOKEN_BLOCK // GRID_W
    ctx_lines = n_ctx_blocks * lines_per_block
    rows = n_lines - ctx_lines
    cols_per_block = TOKEN_BLOCK // rows

    def copy(grid_view, buf_view):
        src, dst = (buf_view, grid_view) if to_grid else (grid_view, buf_view)
        return pltpu.make_async_copy(src, dst, sem.at[bi])

    def ctx():
        return [copy(grid_ref.at[elem, t * lines_per_block + r], buf_ref.at[bi, pl.ds(r * GRID_W, GRID_W)])
                for r in range(lines_per_block)]

    def lat():
        col0 = (t - n_ctx_blocks) * cols_per_block
        return [copy(grid_ref.at[elem, pl.ds(ctx_lines, rows), col0 + c], buf_ref.at[bi, pl.ds(c * rows, rows)])
                for c in range(cols_per_block)]

    return ctx, lat


def _ffn_kernel(x_ref, m_ref, mctx_ref, wup_ref, wdn_ref, g_ref, b_ref, o_ref, *scratch, j, alpha, d_ff, ff_chunk,
                n_ctx_blocks, gather_in, scatter_out):
    b = pl.program_id(0)
    t = pl.program_id(1)
    is_ctx = t < n_ctx_blocks
    nb = m_ref.shape[0]
    buf, sem = scratch if (gather_in or scatter_out) else (None, None)

    def copies(bi):
        grid_ref = x_ref if gather_in else o_ref
        return _grid_block_copies(grid_ref, buf, sem, b * nb + bi, bi, t, n_ctx_blocks, to_grid=scatter_out)

    def for_block_kind(bi, action):
        ctx, lat = copies(bi)
        if n_ctx_blocks:
            @pl.when(is_ctx)
            def _():
                for cp in ctx():
                    action(cp)

        @pl.when(jnp.logical_not(is_ctx))
        def _():
            for cp in lat():
                action(cp)

    if gather_in:
        for bi in range(nb):
            for_block_kind(bi, lambda cp: cp.start())
    for bi in range(nb):
        if gather_in:
            for_block_kind(bi, lambda cp: cp.wait())
            x = buf[bi]
        else:
            x = x_ref[bi]
        m = jnp.where(is_ctx, mctx_ref[3 * j:3 * j + 3, :], m_ref[bi, 3 * j:3 * j + 3, :])
        shift, scale, gate = m[0:1], m[1:2], m[2:3]
        u = (x * (1.0 + scale) + shift).astype(BF16)
        acc = None
        for c0 in range(0, d_ff, ff_chunk):
            a = _dot(u, wup_ref[:, c0:c0 + ff_chunk])
            v = _dot(u, wup_ref[:, d_ff + c0:d_ff + c0 + ff_chunk])
            hid = (_silu(a) * v).astype(BF16)
            part = _dot(hid, wdn_ref[c0:c0 + ff_chunk, :])
            acc = part if acc is None else acc + part
        z = alpha * x + (FFN_RES_W * gate) * acc
        out = _layer_norm(z, g_ref[...], b_ref[...])
        if scatter_out:
            buf[bi] = out
            for_block_kind(bi, lambda cp: cp.start())
        else:
            o_ref[bi] = out
    if scatter_out:
        for bi in range(nb):
            for_block_kind(bi, lambda cp: cp.wait())


def _ffn_call(h, mods, w_up, w_dn, lead, ln_g, ln_b, *, j, n_ctx_blocks, alpha, gather_in=False, scatter_out=False):
    assert not (gather_in and scatter_out)
    n_b, tt, d = h.shape
    d_ff = w_dn.shape[-2]
    ff_chunk = d_ff
    ctx_row = mods.shape[0] - 1
    n_blocks = tt // TOKEN_BLOCK
    nb = TOKEN_BATCH if n_b % TOKEN_BATCH == 0 else 1
    grid_view = (n_b, tt // GRID_W, GRID_W, d)
    block_spec = pl.BlockSpec((nb, TOKEN_BLOCK, d), lambda b, t: (b, t, 0))
    any_spec = pl.BlockSpec(memory_space=pl.ANY)
    scratch = []
    if gather_in or scatter_out:
        scratch = [pltpu.VMEM((nb, TOKEN_BLOCK, d), F32), pltpu.SemaphoreType.DMA((nb,))]

    kern = functools.partial(_ffn_kernel, j=j, alpha=alpha, d_ff=d_ff, ff_chunk=ff_chunk, n_ctx_blocks=n_ctx_blocks,
                             gather_in=gather_in, scatter_out=scatter_out)
    out = pl.pallas_call(
        kern,
        out_shape=jax.ShapeDtypeStruct(grid_view if scatter_out else h.shape, F32),
        grid=(n_b // nb, n_blocks),
        in_specs=[
            any_spec if gather_in else block_spec,
            pl.BlockSpec((nb, N_MOD, d), lambda b, t: (b, 0, 0)),
            pl.BlockSpec((None, N_MOD, d), lambda b, t: (ctx_row, 0, 0)),
            _stacked(w_up, lead),
            _stacked(w_dn, lead),
            _resident((1, d), lambda b, t: (0, 0)),
            _resident((1, d), lambda b, t: (0, 0)),
        ],
        out_specs=any_spec if scatter_out else block_spec,
        scratch_shapes=scratch,
        compiler_params=_cparams(2),
        name="ffn_sublayer",
    )(h.reshape(grid_view) if gather_in else h, mods, mods, w_up, w_dn, ln_g.reshape(1, d), ln_b.reshape(1, d))
    return out.reshape(h.shape) if scatter_out else out


def _proj_layout(d, ssd_heads):
    ssd_w = d
    bc_w = SSD_GROUPS * SSD_STATE
    gla_key = d // 2
    names = [("ssd_x", ssd_w), ("ssd_b", bc_w), ("ssd_c", bc_w), ("lru_x", d), ("ssd_z", ssd_w),
             ("lru_g", d), ("gla_q", gla_key), ("gla_k", gla_key), ("gla_v", d), ("gla_g", d),
             ("gates", N_BRANCH * d), ("small", LANES)]
    off, out = 0, {}
    for name, width in names:
        out[name] = (off, width)
        off += width
    return out, off


_ACTIVATIONS = {"none": lambda v: v, "silu": _silu, "sigmoid": _sigmoid, "gelu": jax.nn.gelu}


def _proj_kernel(x_ref, m_ref, mctx_ref, w_ref, cw_ref, cb_ref, o_ref, sm_ref, ybuf, *, segments, main_cols,
                 n_ctx_blocks, line_ctx, line_lat):
    is_ctx = pl.program_id(1) < n_ctx_blocks
    line = jnp.where(is_ctx, line_ctx, line_lat)
    n_strips = ybuf.shape[0] // x_ref.shape[0]
    for bi in range(x_ref.shape[0]):
        m = jnp.where(is_ctx, mctx_ref[3:5, :], m_ref[bi, 3:5, :])
        _proj_block(x_ref.at[bi], m[0:1], m[1:2], w_ref, cw_ref, cb_ref, o_ref.at[bi], sm_ref.at[bi],
                    ybuf.at[bi * n_strips:(bi + 1) * n_strips], line, segments, main_cols)


def _proj_block(x_ref, shift, scale, w_ref, cw_ref, cb_ref, o_ref, sm_ref, ybuf, line, segments, main_cols):
    x = x_ref[...]
    u = (x * (1.0 + scale) + shift).astype(BF16)
    rows = x.shape[0]
    halo = SUBLANES
    sm_ref[...] = _dot(u, w_ref[:, main_cols:main_cols + LANES])
    pos = lax.broadcasted_iota(jnp.int32, (rows, LANES), 0) & (line - 1)
    taps = [(k, k - CONV_LEFT) for k in range(CONV_W) if k != CONV_LEFT]
    valid = {off: (pos + off >= 0) & (pos + off < line) for _, off in taps}
    zeros = jnp.zeros((halo, LANES), F32)
    for c0, cw, conv, act in segments:
        y = _dot(u, w_ref[:, c0:c0 + cw])
        if not conv:
            o_ref[:, c0:c0 + cw] = _ACTIVATIONS[act](y).astype(o_ref.dtype)
            continue
        for j in range(cw // LANES):
            ybuf[j, 0:halo, :] = zeros
            ybuf[j, halo:halo + rows, :] = y[:, j * LANES:(j + 1) * LANES]
            ybuf[j, halo + rows:2 * halo + rows, :] = zeros
        for j in range(cw // LANES):
            cols = slice(c0 + j * LANES, c0 + (j + 1) * LANES)
            acc = cb_ref[:, cols] + ybuf[j, halo:halo + rows, :] * cw_ref[CONV_LEFT:CONV_LEFT + 1, cols]
            for k, off in taps:
                shifted = ybuf[j, halo + off:halo + off + rows, :]
                acc = acc + jnp.where(valid[off], shifted, 0.0) * cw_ref[k:k + 1, cols]
            o_ref[:, cols] = _ACTIVATIONS[act](acc).astype(o_ref.dtype)


def _proj_segments(layout, col_chunk):
    kinds = {"ssd_x": (True, "silu"), "ssd_b": (True, "silu"), "ssd_c": (True, "silu"), "lru_x": (True, "none"),
             "lru_g": (False, "gelu"), "gates": (False, "sigmoid"), "ssd_z": (False, "silu"),
             "gla_g": (False, "silu"), "gla_v": (False, "none"), "gla_q": (False, "none"), "gla_k": (False, "none")}
    conv_segs, plain_segs = [], []
    for name, (conv, act) in kinds.items():
        off, width = layout[name]
        for c0 in range(off, off + width, col_chunk):
            (conv_segs if conv else plain_segs).append((c0, min(col_chunk, off + width - c0), conv, act))
    segs = []
    while conv_segs or plain_segs:
        if conv_segs:
            segs.append(conv_segs.pop(0))
        if plain_segs:
            segs.append(plain_segs.pop(0))
    return tuple(segs)


def _proj_call(h, mods, w_perm, layer, conv_w, conv_b, layout, *, n_ctx_blocks, line_ctx, line_lat):
    n_b, tt, d = h.shape
    total = w_perm.shape[-1]
    conv_cols = conv_w.shape[1]
    ctx_row = mods.shape[0] - 1
    n_blocks = tt // TOKEN_BLOCK
    col_chunk = 1024
    nb = PROJ_BATCH if n_b % PROJ_BATCH == 0 else 1

    main = total - LANES
    kern = functools.partial(_proj_kernel, segments=_proj_segments(layout, col_chunk), main_cols=main,
                             n_ctx_blocks=n_ctx_blocks, line_ctx=line_ctx, line_lat=line_lat)
    return pl.pallas_call(
        kern,
        out_shape=(jax.ShapeDtypeStruct((n_b, tt, main), BF16), jax.ShapeDtypeStruct((n_b, tt, LANES), F32)),
        grid=(n_b // nb, n_blocks),
        in_specs=[
            pl.BlockSpec((nb, TOKEN_BLOCK, d), lambda b, t: (b, t, 0)),
            pl.BlockSpec((nb, N_MOD, d), lambda b, t: (b, 0, 0)),
            pl.BlockSpec((None, N_MOD, d), lambda b, t: (ctx_row, 0, 0)),
            _stacked(w_perm, (layer,)),
            _resident((CONV_W, conv_cols), lambda b, t: (0, 0)),
            _resident((1, conv_cols), lambda b, t: (0, 0)),
        ],
        out_specs=(pl.BlockSpec((nb, TOKEN_BLOCK, main), lambda b, t: (b, t, 0)),
                   pl.BlockSpec((nb, TOKEN_BLOCK, LANES), lambda b, t: (b, t, 0))),
        scratch_shapes=[pltpu.VMEM((nb * (col_chunk // LANES), TOKEN_BLOCK + 2 * SUBLANES, LANES), F32)],
        compiler_params=_cparams(2),
        name="mixer_in_proj",
    )(h, mods, mods, w_perm, conv_w, conv_b)


def _ssd_chunk(fwd, x_ref, b_ref, c_ref, sm_ref, dtb_ref, alog_ref, exp_ref, y_ref, state_ref, n_heads):
    q = x_ref.shape[0]
    hg = n_heads // SSD_GROUPS
    gw = hg * SSD_HEAD_DIM
    sm = sm_ref[...]
    dt_raw = jnp.where(fwd, sm[:, 0:n_heads], sm[:, n_heads:2 * n_heads])
    dt_bias = jnp.where(fwd, dtb_ref[0:1, :], dtb_ref[1:2, :])
    a_log = jnp.where(fwd, alog_ref[0:1, :], alog_ref[1:2, :])
    dt = _softplus(dt_raw + dt_bias)
    a = dt * (-jnp.exp(a_log))
    row = lax.broadcasted_iota(jnp.int32, (q, q), 0)
    col = lax.broadcasted_iota(jnp.int32, (q, q), 1)
    mask = (col - row) * jnp.where(fwd, 1, -1) <= 0
    tri = jnp.where(mask, 1.0, 0.0).astype(BF16)
    cs = _exact_lhs_dot(tri, a, 3)
    tot = jnp.sum(a, axis=0, keepdims=True)
    cs2 = cs * math.log2(math.e)
    src2 = cs2 - jnp.log2(dt)
    pad = jnp.zeros((q, LANES - n_heads), F32)
    src2_t = jnp.concatenate([src2, pad], axis=1).T[0:n_heads]

    w_state = dt * jnp.exp(tot - cs)
    from_start = jnp.exp(cs)
    e_tot = jnp.broadcast_to(jnp.exp(tot), (SUBLANES, n_heads))
    wide = _exact_rhs_dot(jnp.concatenate([w_state, from_start, e_tot], axis=0), exp_ref[...])
    w_state_w = wide[0:q]
    from_start_w = wide[q:2 * q]
    e_tot_w = wide[2 * q:2 * q + 1]

    x_bf = x_ref[...]
    xw_bf = (x_bf.astype(F32) * w_state_w).astype(BF16)
    pair_w = 2 * SSD_HEAD_DIM
    lane = lax.broadcasted_iota(jnp.int32, (1, pair_w), 1)
    keep_lo = jnp.where(lane < SSD_HEAD_DIM, 1.0, 0.0).astype(BF16)
    keep_hi = jnp.where(lane < SSD_HEAD_DIM, 0.0, 1.0).astype(BF16)
    for g in range(SSD_GROUPS):
        bt = b_ref[:, g * SSD_STATE:(g + 1) * SSD_STATE].T
        cg = c_ref[:, g * SSD_STATE:(g + 1) * SSD_STATE]
        scores = _dot(cg, bt)
        s_prev = state_ref[g]
        y_off = _dot(cg, s_prev.astype(BF16)) * from_start_w[:, g * gw:(g + 1) * gw]
        for pp in range(hg // 2):
            ms = []
            for head in (g * hg + 2 * pp, g * hg + 2 * pp + 1):
                seg2 = jnp.where(mask, cs2[:, head:head + 1] - src2_t[head:head + 1, :], -1e30)
                ms.append((scores * jnp.exp2(seg2)).astype(BF16))
            cols = slice(g * gw + pp * pair_w, g * gw + (pp + 1) * pair_w)
            xp = x_bf[:, cols]
            y_pair = _dot(jnp.concatenate(ms, axis=1), jnp.concatenate([xp * keep_lo, xp * keep_hi], axis=0))
            y_ref[:, cols] = (y_off[:, pp * pair_w:(pp + 1) * pair_w] + y_pair).astype(y_ref.dtype)
        upd = _dot(bt, xw_bf[:, g * gw:(g + 1) * gw])
        state_ref[g] = s_prev * e_tot_w[:, g * gw:(g + 1) * gw] + upd


def _scan_col_spec(layout, name, nb, rows, n_ctx, n_tot):
    off, w = layout[name]
    assert off % w == 0
    blk = off // w
    return pl.BlockSpec((nb, rows, w), lambda b, r, i: (b, _scan_order(i, r, n_ctx, n_tot), blk))


def _scan_small_spec(nb, rows, n_ctx, n_tot):
    return pl.BlockSpec((nb, rows, LANES), lambda b, r, i: (b, _scan_order(i, r, n_ctx, n_tot), 0))


def _scan_out_spec(nb, rows, width, n_ctx, n_tot):
    return pl.BlockSpec((None, nb, rows, width), lambda b, r, i: (r, b, _scan_order(i, r, n_ctx, n_tot), 0))


def _lru_scan_block(a_ref, b_ref, o_ref, carry_ref, reverse):
    rows, width = a_ref.shape
    sub = lax.broadcasted_iota(jnp.int32, (SUBLANES, width), 0)
    carry = carry_ref[...]
    n_groups = rows // SUBLANES
    order = range(n_groups - 1, -1, -1) if reverse else range(n_groups)
    for gi in order:
        sl = slice(gi * SUBLANES, (gi + 1) * SUBLANES)
        a = a_ref[sl, :]
        b = b_ref[sl, :]
        k = 1
        while k < SUBLANES:
            if reverse:
                a_sh = pltpu.roll(a, SUBLANES - k, axis=0)
                b_sh = pltpu.roll(b, SUBLANES - k, axis=0)
                ok = sub < SUBLANES - k
            else:
                a_sh = pltpu.roll(a, k, axis=0)
                b_sh = pltpu.roll(b, k, axis=0)
                ok = sub >= k
            b = b + a * jnp.where(ok, b_sh, 0.0)
            a = a * jnp.where(ok, a_sh, 1.0)
            k *= 2
        hblk = b + a * carry
        b_ref[sl, :] = hblk
        carry = hblk[0:1, :] if reverse else hblk[SUBLANES - 1:SUBLANES, :]
    carry_ref[...] = carry
    o_ref[...] = b_ref[...].astype(o_ref.dtype)


def _lru_kernel(x_ref, w_ref, ba_ref, bx_ref, lam_ref, h_ref, a_s, b_s, carry_ref):
    rev = pl.program_id(1)
    i = pl.program_id(2)
    nb = x_ref.shape[0]

    @pl.when(i == 0)
    def _():
        carry_ref[...] = jnp.zeros_like(carry_ref)

    n_tiles = w_ref.shape[0]
    neg_c_sp = (-LRU_C) * _softplus(-lam_ref[...])
    for bi in range(nb):
        x_bf = x_ref[bi]
        pre_a, pre_x = [], []
        for t in range(n_tiles):
            y = _dot(x_bf[:, t * MXU_TILE:(t + 1) * MXU_TILE], w_ref[t])
            pre_a.append(y[:, :MXU_TILE])
            pre_x.append(y[:, MXU_TILE:])
        r = _sigmoid(jnp.concatenate(pre_a, axis=1) + ba_ref[...])
        ig = _sigmoid(jnp.concatenate(pre_x, axis=1) + bx_ref[...])
        log_a = r * neg_c_sp
        a = jnp.exp(log_a)
        a_s[bi] = a
        b_s[bi] = jnp.sqrt(-jnp.tanh(log_a) * (a * a + 1.0)) * (ig * x_bf.astype(F32))

    @pl.when(rev == 0)
    def _():
        for bi in range(nb):
            _lru_scan_block(a_s.at[bi], b_s.at[bi], h_ref.at[bi], carry_ref.at[bi], reverse=False)

    @pl.when(rev == 1)
    def _():
        for bi in range(nb):
            _lru_scan_block(a_s.at[bi], b_s.at[bi], h_ref.at[bi], carry_ref.at[bi], reverse=True)


def _lru_call(p, layout, w_tiles, layer, b_a, b_x, lam, *, n_ctx_blocks):
    n_b, tt, _ = p.shape
    width = layout["lru_x"][1]
    n_blocks = tt // TOKEN_BLOCK
    n_tiles = w_tiles.shape[2]
    nb = SCAN_BATCH if n_b % SCAN_BATCH == 0 else 1

    return pl.pallas_call(
        _lru_kernel,
        out_shape=jax.ShapeDtypeStruct((2, n_b, tt, width), BF16),
        grid=(n_b // nb, 2, n_blocks),
        in_specs=[
            _scan_col_spec(layout, "lru_x", nb, TOKEN_BLOCK, n_ctx_blocks, n_blocks),
            pl.BlockSpec((None, None, n_tiles, MXU_TILE, 2 * MXU_TILE), lambda b, r, i: (layer, r, 0, 0, 0)),
            pl.BlockSpec((None, 1, width), lambda b, r, i: (r, 0, 0)),
            pl.BlockSpec((None, 1, width), lambda b, r, i: (r, 0, 0)),
            pl.BlockSpec((None, 1, width), lambda b, r, i: (r, 0, 0)),
        ],
        out_specs=_scan_out_spec(nb, TOKEN_BLOCK, width, n_ctx_blocks, n_blocks),
        scratch_shapes=[pltpu.VMEM((nb, TOKEN_BLOCK, width), F32), pltpu.VMEM((nb, TOKEN_BLOCK, width), F32),
                        pltpu.VMEM((nb, 1, width), F32)],
        compiler_params=_cparams(3),
        name="rglru_scan",
    )(p, w_tiles, b_a, b_x, lam)


def _gla_chunk(fwd, q_ref, k_ref, v_ref, sm_ref, wg_ref, bg_ref, o_ref, state_ref, alr_off, dk, dv):
    q_len = q_ref.shape[0]
    n_heads = q_ref.shape[1] // dk
    sm = sm_ref[...]
    r = GLA_GATE_RANK
    a_lr = jnp.where(fwd, sm[:, alr_off:alr_off + r], sm[:, alr_off + r:alr_off + 2 * r])
    z = _f32_dot_small_k(a_lr, wg_ref[...]) + bg_ref[...]
    log_a = _log_sigmoid(z) * (1.0 / GLA_TAU)
    row = lax.broadcasted_iota(jnp.int32, (q_len, q_len), 0)
    col = lax.broadcasted_iota(jnp.int32, (q_len, q_len), 1)
    mask = (col - row) * jnp.where(fwd, 1, -1) <= 0
    tri = jnp.where(mask, 1.0, 0.0).astype(BF16)
    bcum = _exact_lhs_dot(tri, log_a, 2)
    btot = jnp.sum(log_a, axis=0, keepdims=True)
    bmid = bcum[q_len // 2:q_len // 2 + 1, :]
    q_in = q_ref[...].astype(F32) * (dk ** -0.5) * jnp.exp(bcum - bmid)
    k_in = k_ref[...].astype(F32) * jnp.exp(bmid - bcum)
    q_off = (q_in * jnp.exp(bmid)).astype(BF16)
    k_st = (k_in * jnp.exp(btot - bmid)).astype(BF16)
    q_in = q_in.astype(BF16)
    k_in = k_in.astype(BF16)
    e_tot = jnp.exp(btot)
    for hh in range(n_heads):
        ks = slice(hh * dk, (hh + 1) * dk)
        vs = slice(hh * dv, (hh + 1) * dv)
        v_bf = v_ref[:, vs]
        att = lax.dot_general(q_in[:, ks], k_in[:, ks], (((1,), (1,)), ((), ())), preferred_element_type=F32)
        att = jnp.where(mask, att, 0.0).astype(BF16)
        s_prev = state_ref[hh]
        inter = lax.dot_general(q_off[:, ks], s_prev.astype(BF16), (((1,), (1,)), ((), ())),
                                preferred_element_type=F32)
        o_ref[:, vs] = (_dot(att, v_bf) + inter).astype(o_ref.dtype)
        upd = lax.dot_general(v_bf, k_st[:, ks], (((0,), (0,)), ((), ())), preferred_element_type=F32)
        state_ref[hh] = s_prev * e_tot[:, ks] + upd


def _chunk_scans_kernel(x_ref, b_ref, c_ref, q_ref, k_ref, v_ref, sm_ref, dtb_ref, alog_ref, exp_ref, wg_ref, bg_ref,
                        y_ref, o_ref, ssd_state, gla_state, *, n_heads, alr_off, dk, dv):
    rev = pl.program_id(1)
    i = pl.program_id(2)

    @pl.when(i == 0)
    def _():
        ssd_state[...] = jnp.zeros_like(ssd_state)
        gla_state[...] = jnp.zeros_like(gla_state)

    for bi in range(x_ref.shape[0]):
        _ssd_chunk(rev == 0, x_ref.at[bi], b_ref.at[bi], c_ref.at[bi], sm_ref.at[bi], dtb_ref, alog_ref, exp_ref,
                   y_ref.at[bi], ssd_state.at[bi], n_heads)
        _gla_chunk(rev == 0, q_ref.at[bi], k_ref.at[bi], v_ref.at[bi], sm_ref.at[bi], wg_ref, bg_ref,
                   o_ref.at[bi], gla_state.at[bi], alr_off, dk, dv)


def _chunk_scans_call(p, small, layout, dt_bias, a_log, expand, w_gate, b_gate, *, alr_off, n_ctx_chunks):
    n_b, tt, _ = p.shape
    n_heads = dt_bias.shape[1]
    ssd_w = n_heads * SSD_HEAD_DIM
    hg = n_heads // SSD_GROUPS
    key_w = layout["gla_q"][1]
    val_w = layout["gla_v"][1]
    dk = key_w // GLA_HEADS
    dv = val_w // GLA_HEADS
    n_chunks = tt // SCAN_CHUNK
    nb = SCAN_BATCH if n_b % SCAN_BATCH == 0 else 1
    col_spec = functools.partial(_scan_col_spec, layout, nb=nb, rows=SCAN_CHUNK, n_ctx=n_ctx_chunks, n_tot=n_chunks)

    kern = functools.partial(_chunk_scans_kernel, n_heads=n_heads, alr_off=alr_off, dk=dk, dv=dv)
    return pl.pallas_call(
        kern,
        out_shape=(jax.ShapeDtypeStruct((2, n_b, tt, ssd_w), BF16), jax.ShapeDtypeStruct((2, n_b, tt, val_w), BF16)),
        grid=(n_b // nb, 2, n_chunks),
        in_specs=[
            col_spec("ssd_x"), col_spec("ssd_b"), col_spec("ssd_c"),
            col_spec("gla_q"), col_spec("gla_k"), col_spec("gla_v"),
            _scan_small_spec(nb, SCAN_CHUNK, n_ctx_chunks, n_chunks),
            _resident((2, n_heads), lambda b, r, i: (0, 0)),
            _resident((2, n_heads), lambda b, r, i: (0, 0)),
            _resident(expand.shape, lambda b, r, i: (0, 0)),
            pl.BlockSpec((None, GLA_GATE_RANK, key_w), lambda b, r, i: (r, 0, 0)),
            pl.BlockSpec((None, 1, key_w), lambda b, r, i: (r, 0, 0)),
        ],
        out_specs=(_scan_out_spec(nb, SCAN_CHUNK, ssd_w, n_ctx_chunks, n_chunks),
                   _scan_out_spec(nb, SCAN_CHUNK, val_w, n_ctx_chunks, n_chunks)),
        scratch_shapes=[pltpu.VMEM((nb, SSD_GROUPS, SSD_STATE, hg * SSD_HEAD_DIM), F32),
                        pltpu.VMEM((nb, GLA_HEADS, dv, dk), F32)],
        compiler_params=_cparams(3),
        name="ssd_gla_scan",
    )(p, p, p, p, p, p, small, dt_bias, a_log, expand, w_gate, b_gate)


def _merge_kernel(h_ref, m_ref, mctx_ref, ys0_ref, ys1_ref, xs_ref, z_ref, hl0_ref, hl1_ref, gl_ref,
                  og0_ref, og1_ref, gg_ref, gt0_ref, gt1_ref, gt2_ref, dskip_ref, sng_ref, gng_ref,
                  wbr_ref, wout_ref, lng_ref, lnb_ref, o_ref, *, alpha, dv, n_ctx_blocks, skip_blocks):
    is_ctx = pl.program_id(1) + skip_blocks < n_ctx_blocks
    for bi in range(h_ref.shape[0]):
        gate = jnp.where(is_ctx, mctx_ref[5:6, :], m_ref[bi, 5:6, :])
        _merge_block(h_ref.at[bi], gate, ys0_ref.at[bi], ys1_ref.at[bi], xs_ref.at[bi], z_ref.at[bi],
                     hl0_ref.at[bi], hl1_ref.at[bi], gl_ref.at[bi], og0_ref.at[bi], og1_ref.at[bi], gg_ref.at[bi],
                     gt0_ref.at[bi], gt1_ref.at[bi], gt2_ref.at[bi], dskip_ref, sng_ref, gng_ref, wbr_ref, wout_ref,
                     lng_ref, lnb_ref, o_ref.at[bi], alpha, dv)


def _merge_block(h_ref, gate, ys0_ref, ys1_ref, xs_ref, z_ref, hl0_ref, hl1_ref, gl_ref, og0_ref, og1_ref, gg_ref,
                 gt0_ref, gt1_ref, gt2_ref, dskip_ref, sng_ref, gng_ref, wbr_ref, wout_ref, lng_ref, lnb_ref, o_ref,
                 alpha, dv):
    d = h_ref.shape[1]

    def f32(ref, cols=slice(None)):
        return ref[:, cols].astype(F32)

    y = (f32(ys0_ref) + f32(ys1_ref) + dskip_ref[...] * f32(xs_ref)) * f32(z_ref)
    y = y * lax.rsqrt(jnp.mean(y * y, axis=-1, keepdims=True) + NORM_EPS) * sng_ref[...]
    m = f32(gt0_ref) * _dot(y.astype(BF16), wbr_ref[0])
    y = (f32(hl0_ref) + f32(hl1_ref)) * f32(gl_ref)
    m = m + f32(gt1_ref) * _dot(y.astype(BF16), wbr_ref[1])
    parts = []
    for hh in range(d // dv):
        vs = slice(hh * dv, (hh + 1) * dv)
        o = f32(og0_ref, vs) + f32(og1_ref, vs)
        o = o * lax.rsqrt(jnp.mean(o * o, axis=-1, keepdims=True) + NORM_EPS) * gng_ref[...]
        parts.append(o * f32(gg_ref, vs))
    y = jnp.concatenate(parts, axis=1)
    m = m + f32(gt2_ref) * _dot(y.astype(BF16), wbr_ref[2])
    out = _dot(m.astype(BF16), wout_ref[...])
    z = alpha * h_ref[...] + gate * out
    o_ref[...] = _layer_norm(z, lng_ref[...], lnb_ref[...])


def _merge_call(h, mods, p, layout, y_ssd, h_lru, o_gla, d_skip_w, ssd_norm_g, gla_norm_g, w_branch, w_out, layer,
                ln_g, ln_b, *, n_ctx_blocks, skip_blocks, alpha):
    n_b, tt, d = h.shape
    n_blocks = tt // TOKEN_BLOCK - skip_blocks
    ctx_row = mods.shape[0] - 1
    dv = gla_norm_g.shape[-1]
    nb = MERGE_BATCH if n_b % MERGE_BATCH == 0 else 1

    def tok(width, blk):
        return pl.BlockSpec((nb, TOKEN_BLOCK, width), lambda b, t: (b, t + skip_blocks, blk))

    def col_spec(name):
        off, w = layout[name]
        blk = off // w
        assert blk * w == off
        return tok(w, blk)

    def gate_spec(n):
        off = layout["gates"][0] + n * d
        assert off % d == 0
        return tok(d, off // d)

    def dir_spec(rv):
        return pl.BlockSpec((None, nb, TOKEN_BLOCK, d), lambda b, t: (rv, b, t + skip_blocks, 0))

    kern = functools.partial(_merge_kernel, alpha=alpha, dv=dv, n_ctx_blocks=n_ctx_blocks, skip_blocks=skip_blocks)
    return pl.pallas_call(
        kern,
        out_shape=jax.ShapeDtypeStruct((n_b, n_blocks * TOKEN_BLOCK, d), F32),
        grid=(n_b // nb, n_blocks),
        in_specs=[
            tok(d, 0),
            pl.BlockSpec((nb, N_MOD, d), lambda b, t: (b, 0, 0)),
            pl.BlockSpec((None, N_MOD, d), lambda b, t: (ctx_row, 0, 0)),
            dir_spec(0), dir_spec(1), col_spec("ssd_x"), col_spec("ssd_z"),
            dir_spec(0), dir_spec(1), col_spec("lru_g"),
            dir_spec(0), dir_spec(1), col_spec("gla_g"),
            gate_spec(0), gate_spec(1), gate_spec(2),
            _resident((1, d), lambda b, t: (0, 0)),
            _resident((1, d), lambda b, t: (0, 0)),
            _resident((1, dv), lambda b, t: (0, 0)),
            _stacked(w_branch, (layer,)),
            _stacked(w_out, (layer,)),
            _resident((1, d), lambda b, t: (0, 0)),
            _resident((1, d), lambda b, t: (0, 0)),
        ],
        out_specs=pl.BlockSpec((nb, TOKEN_BLOCK, d), lambda b, t: (b, t, 0)),
        compiler_params=_cparams(2),
        name="mixer_merge",
    )(h, mods, mods, y_ssd, y_ssd, p, p, h_lru, h_lru, p, o_gla, o_gla, p, p, p, p,
      d_skip_w, ssd_norm_g.reshape(1, d), gla_norm_g.reshape(1, dv), w_branch, w_out,
      ln_g.reshape(1, d), ln_b.reshape(1, d))


def _permute_w_kernel(w_ref, o_ref, *, moves, tail):
    in_total = w_ref.shape[1]

    def piece(src, width):
        a0 = src // LANES * LANES
        a1 = min(-(-(src + width) // LANES) * LANES, in_total)
        return w_ref[:, a0:a1][:, src - a0:src - a0 + width]

    for src, dst, width in moves:
        o_ref[:, dst:dst + width] = piece(src, width).astype(o_ref.dtype)
    parts = [piece(src, width) for src, width in tail]
    used = sum(width for _, width in tail)
    parts.append(jnp.zeros((w_ref.shape[0], LANES - used), w_ref.dtype))
    o_ref[:, o_ref.shape[1] - LANES:] = jnp.concatenate(parts, axis=1).astype(o_ref.dtype)


def _permute_w_in(w_in, d, ssd_heads, total):
    ssd_w = d
    bc_w = SSD_GROUPS * SSD_STATE
    gla_key = d // 2
    sizes = (ssd_w, ssd_w + 2 * bc_w, 2 * ssd_heads, d, d, gla_key, gla_key, d, d, 2 * GLA_GATE_RANK, N_BRANCH * d)
    src = [0]
    for s in sizes:
        src.append(src[-1] + s)
    z, xbc, dtr, lx, lg, gq, gk, gv, gg, alr, gates = range(len(sizes))
    order = (xbc, lx, z, lg, gq, gk, gv, gg, gates)
    moves, dst = [], 0
    for n in order:
        assert dst % LANES == 0
        moves.append((src[n], dst, sizes[n]))
        dst += sizes[n]
    assert dst == total - LANES
    tail = ((src[dtr], sizes[dtr]), (src[alr], sizes[alr]))
    n_layers, rows, in_total = w_in.shape
    row_block = 256 if rows % 256 == 0 else rows
    kern = functools.partial(_permute_w_kernel, moves=tuple(moves), tail=tail)
    return pl.pallas_call(
        kern,
        out_shape=jax.ShapeDtypeStruct((n_layers, rows, total), BF16),
        grid=(n_layers, rows // row_block),
        in_specs=[pl.BlockSpec((None, row_block, in_total), lambda l, r: (l, r, 0))],
        out_specs=pl.BlockSpec((None, row_block, total), lambda l, r: (l, r, 0)),
        compiler_params=_cparams(2),
        name="permute_w_in",
    )(w_in)


def _lru_gate_tiles(w_a, w_x):
    per = MXU_TILE // LRU_BLOCK
    n_l, n_dir, nb, k, _ = w_a.shape
    n_tiles = nb // per

    def tiles(w):
        w = w.astype(BF16).reshape(n_l, n_dir, n_tiles, per, k, k)
        on_diag = jnp.eye(per, dtype=bool)[:, None, :, None]
        t = jnp.where(on_diag, w[:, :, :, :, :, None, :], jnp.zeros((), BF16))
        return t.reshape(n_l, n_dir, n_tiles, MXU_TILE, MXU_TILE)

    return jnp.concatenate([tiles(w_a), tiles(w_x)], axis=-1)


def kernel(x, c, ctx, c_ctx, w_ada, b_ada, ln_g, ln_b, ffn_w_up, ffn_w_down, w_in, ssd_conv_w, ssd_conv_b,
           ssd_dt_bias, ssd_a_log, ssd_d, ssd_norm_g, lru_conv_w, lru_conv_b, lru_w_a, lru_b_a, lru_w_x, lru_b_x,
           lru_lam, gla_w_gate, gla_b_gate, gla_norm_g, w_branch, w_out):
    n_b, t_lat, d = x.shape
    n_ctx = ctx.shape[1]
    depth = w_ada.shape[0]
    rows = t_lat // GRID_W
    ssd_heads = ssd_dt_bias.shape[-1]
    assert n_ctx % TOKEN_BLOCK == 0 and t_lat % TOKEN_BLOCK == 0
    for line in (n_ctx, GRID_W, rows):
        assert line & (line - 1) == 0 and TOKEN_BLOCK % min(line, TOKEN_BLOCK) == 0
    assert n_ctx <= TOKEN_BLOCK
    alpha = (2.0 * depth) ** 0.25
    n_ctx_blocks = n_ctx // TOKEN_BLOCK
    n_ctx_chunks = n_ctx // SCAN_CHUNK

    layout, total = _proj_layout(d, ssd_heads)
    alr_off = 2 * ssd_heads

    n_rows = -(-(n_b + 1) // SUBLANES) * SUBLANES
    cond = jnp.concatenate([c, jnp.zeros((n_rows - n_b - 1, d), F32), c_ctx[None, :]], axis=0)
    mods_all = _ada_call(cond, w_ada, b_ada).reshape(depth, n_rows, N_MOD, d)

    expand = jnp.repeat(jnp.eye(ssd_heads, dtype=BF16), SSD_HEAD_DIM, axis=1)
    expand = jnp.concatenate([expand] * 3, axis=0)
    wup_all = ffn_w_up.astype(BF16)
    wdn_all = ffn_w_down.astype(BF16)
    w_perm_all = _permute_w_in(w_in, d, ssd_heads, total)
    lru_tiles_all = _lru_gate_tiles(lru_w_a, lru_w_x)
    w_branch_all = w_branch.astype(BF16)
    w_out_all = w_out.astype(BF16)

    h = jnp.concatenate([ctx, x], axis=1)
    for l in range(depth):
        last = l == depth - 1
        col_major = l % 2 == 1
        mods = mods_all[l]

        h = _ffn_call(h, mods, wup_all, wdn_all, (l, 0), ln_g[l, 0], ln_b[l, 0], j=0, n_ctx_blocks=n_ctx_blocks,
                      alpha=alpha, gather_in=col_major)

        conv_w = jnp.concatenate([ssd_conv_w[l], lru_conv_w[l]], axis=1)
        conv_b = jnp.concatenate([ssd_conv_b[l], lru_conv_b[l]], axis=0)[None, :]
        p, small = _proj_call(h, mods, w_perm_all, l, conv_w, conv_b, layout, n_ctx_blocks=n_ctx_blocks,
                              line_ctx=n_ctx, line_lat=rows if col_major else GRID_W)

        y_ssd, o_gla = _chunk_scans_call(p, small, layout, ssd_dt_bias[l], ssd_a_log[l], expand, gla_w_gate[l],
                                         gla_b_gate[l][:, None, :], alr_off=alr_off, n_ctx_chunks=n_ctx_chunks)
        h_lru = _lru_call(p, layout, lru_tiles_all, l, lru_b_a[l][:, None, :], lru_b_x[l][:, None, :],
                          lru_lam[l][:, None, :], n_ctx_blocks=n_ctx_blocks)

        d_skip_w = jnp.repeat(ssd_d[l, 0] + ssd_d[l, 1], SSD_HEAD_DIM)[None, :]
        skip_blocks = n_ctx_blocks if last else 0
        h = _merge_call(h, mods, p, layout, y_ssd, h_lru, o_gla, d_skip_w, ssd_norm_g[l], gla_norm_g[l],
                        w_branch_all, w_out_all, l, ln_g[l, 1], ln_b[l, 1],
                        n_ctx_blocks=n_ctx_blocks, skip_blocks=skip_blocks, alpha=alpha)
        n_ctx_now = 0 if last else n_ctx
        h = _ffn_call(h, mods, wup_all, wdn_all, (l, 1), ln_g[l, 2], ln_b[l, 2], j=2,
                      n_ctx_blocks=n_ctx_now // TOKEN_BLOCK, alpha=alpha, scatter_out=col_major)
    return h
```
